```python
import math
import jax, jax.numpy as jnp
from jax import lax
import numpy as np

D_MODEL = 1024
BATCH = 4
SEQ = 8192
DEPTH = 4

N_CONV_LAYERS = DEPTH // 2
N_ATTN_LAYERS = DEPTH - N_CONV_LAYERS
N_HEADS = 8
HEAD_DIM = D_MODEL // (2 * N_HEADS)
V_HEAD_DIM = 2 * HEAD_DIM
CONV_WIDTH = 31
FFN_CONV_WIDTH = 3
D_FF = ((8 * D_MODEL // 3 + 255) // 256) * 256
N_BUCKETS = 32
MAX_DISTANCE = 128
MAX_EXACT = N_BUCKETS // 2
Q_BLOCK = 128
EPS = 1e-6

kernel_name = "yoco_conformer_diffattn_hybrid"


def rms_norm(x, g):
    x32 = x.astype(jnp.float32)
    y = x32 * lax.rsqrt(jnp.mean(x32 * x32, axis=-1, keepdims=True) + EPS)
    return (y * g.astype(jnp.float32)).astype(x.dtype)


def layer_norm(x, g, b):
    x32 = x.astype(jnp.float32)
    mu = jnp.mean(x32, axis=-1, keepdims=True)
    var = jnp.mean(jnp.square(x32 - mu), axis=-1, keepdims=True)
    y = (x32 - mu) * lax.rsqrt(var + EPS) * g.astype(jnp.float32) + b.astype(jnp.float32)
    return y.astype(x.dtype)


def causal_dwconv(x, w, b):
    width = w.shape[0]
    y = lax.conv_general_dilated(
        x, w[:, None, :].astype(x.dtype), window_strides=(1,), padding=[(width - 1, 0)],
        dimension_numbers=("NWC", "WIO", "NWC"), feature_group_count=x.shape[-1])
    return y + b


def conformer_conv(h, w1, b1, dw, dwb, ln_g, ln_b, w2, b2):
    u = h @ w1 + b1
    a, gt = jnp.split(u, 2, axis=-1)
    u = a * jax.nn.sigmoid(gt)
    u = causal_dwconv(u, dw, dwb)
    u = jax.nn.silu(layer_norm(u, ln_g, ln_b))
    return u @ w2 + b2


def conv_ffn(h, w_in, dw, dwb, w_out):
    u = causal_dwconv(h @ w_in, dw, dwb)
    g, v = jnp.split(u, 2, axis=-1)
    return (jax.nn.silu(g) * v) @ w_out


def t5_bucket(n):
    n_f = jnp.maximum(n, 1).astype(jnp.float32)
    large = MAX_EXACT + (jnp.log(n_f / MAX_EXACT) / math.log(MAX_DISTANCE / MAX_EXACT)
                         * (N_BUCKETS - MAX_EXACT)).astype(jnp.int32)
    large = jnp.minimum(large, N_BUCKETS - 1)
    return jnp.where(n < MAX_EXACT, n, large)


def diff_attention(h, k, v, w_q, lam_vec, subln_g, w_o, rel_bias, lambda_init):
    bsz, seq, _ = h.shape
    n_blocks = seq // Q_BLOCK
    q = (h @ w_q).reshape(bsz, n_blocks, Q_BLOCK, N_HEADS, 2, HEAD_DIM)
    q = jnp.moveaxis(q, 1, 0)
    lv = lam_vec.astype(jnp.float32)
    lam = jnp.exp(jnp.sum(lv[0] * lv[1])) - jnp.exp(jnp.sum(lv[2] * lv[3])) + lambda_init
    starts = jnp.arange(n_blocks, dtype=jnp.int32) * Q_BLOCK
    k_pos = jnp.arange(seq, dtype=jnp.int32)
    scale = HEAD_DIM ** -0.5

    def one_block(args):
        q_blk, start = args
        q_pos = start + jnp.arange(Q_BLOCK, dtype=jnp.int32)
        rel = q_pos[:, None] - k_pos[None, :]
        bias = jnp.transpose(rel_bias[t5_bucket(jnp.maximum(rel, 0))], (2, 0, 1)).astype(jnp.float32)
        s = jnp.einsum("bqhcd,bkhcd->bhcqk", q_blk, k).astype(jnp.float32) * scale + bias[None, :, None]
        s = jnp.where(rel >= 0, s, -jnp.inf)
        p = jax.nn.softmax(s, axis=-1)
        a = (p[:, :, 0] - lam * p[:, :, 1]).astype(v.dtype)
        return jnp.einsum("bhqk,bkhe->bqhe", a, v)

    o = lax.map(one_block, (q, starts))
    o = jnp.moveaxis(o, 0, 1).reshape(bsz, seq, N_HEADS, V_HEAD_DIM)
    o = rms_norm(o, subln_g) * (1.0 - lambda_init)
    return o.reshape(bsz, seq, N_HEADS * V_HEAD_DIM) @ w_o


def setup_inputs(seed: int = 0) -> dict:
    key = jax.random.key(seed)
    ks = jax.random.split(key, 25)
    f32 = jnp.float32

    def nrm(k, shape, s):
        return jax.random.normal(k, shape, f32) * s

    D, F, H, A, Bn = D_MODEL, D_FF, N_HEADS, N_CONV_LAYERS, N_ATTN_LAYERS
    return {
        "x": nrm(ks[0], (BATCH, SEQ, D), 1.0),
        "c": nrm(ks[1], (BATCH, D), 1.0),
        "mod_w": nrm(ks[2], (DEPTH, D, 6 * D), D ** -0.5),
        "mod_b": nrm(ks[3], (DEPTH, 6 * D), 0.02),
        "norm_g": 1.0 + nrm(ks[4], (DEPTH, 4, D), 0.05),
        "cm_w1": nrm(ks[5], (A, D, 2 * D), D ** -0.5),
        "cm_b1": nrm(ks[6], (A, 2 * D), 0.02),
        "cm_dw": nrm(ks[7], (A, CONV_WIDTH, D), CONV_WIDTH ** -0.5),
        "cm_dwb": nrm(ks[8], (A, D), 0.02),
        "cm_ln_g": 1.0 + nrm(ks[9], (A, D), 0.05),
        "cm_ln_b": nrm(ks[10], (A, D), 0.02),
        "cm_w2": nrm(ks[11], (A, D, D), D ** -0.5),
        "cm_b2": nrm(ks[12], (A, D), 0.02),
        "kv_norm_g": 1.0 + nrm(ks[13], (D,), 0.05),
        "w_k": nrm(ks[14], (D, H * 2 * HEAD_DIM), D ** -0.5),
        "w_v": nrm(ks[15], (D, H * V_HEAD_DIM), D ** -0.5),
        "w_q": nrm(ks[16], (Bn, D, H * 2 * HEAD_DIM), D ** -0.5),
        "lam": nrm(ks[17], (Bn, 4, HEAD_DIM), 0.1),
        "subln_g": 1.0 + nrm(ks[18], (Bn, V_HEAD_DIM), 0.05),
        "w_o": nrm(ks[19], (Bn, H * V_HEAD_DIM, D), (H * V_HEAD_DIM) ** -0.5),
        "rel_bias": nrm(ks[20], (N_BUCKETS, H), 0.5),
        "ffn_w_in": nrm(ks[21], (DEPTH, D, 2 * F), D ** -0.5),
        "ffn_dw": nrm(ks[22], (DEPTH, FFN_CONV_WIDTH, 2 * F), FFN_CONV_WIDTH ** -0.5),
        "ffn_dwb": nrm(ks[23], (DEPTH, 2 * F), 0.02),
        "ffn_w_out": nrm(ks[24], (DEPTH, F, D), F ** -0.5),
    }


def reference(x, c, mod_w, mod_b, norm_g, cm_w1, cm_b1, cm_dw, cm_dwb, cm_ln_g, cm_ln_b, cm_w2, cm_b2,
              kv_norm_g, w_k, w_v, w_q, lam, subln_g, w_o, rel_bias, ffn_w_in, ffn_dw, ffn_dwb, ffn_w_out):
    bsz, seq, _ = x.shape
    c_act = jax.nn.silu(c)
    k_shared = None
    v_shared = None
    for l in range(DEPTH):
        mod = c_act @ mod_w[l] + mod_b[l]
        sh_m, sc_m, g_m, sh_f, sc_f, g_f = [m[:, None, :] for m in jnp.split(mod, 6, axis=-1)]

        h = rms_norm(x, norm_g[l, 0]) * (1.0 + sc_m) + sh_m
        if l < N_CONV_LAYERS:
            y = conformer_conv(h, cm_w1[l], cm_b1[l], cm_dw[l], cm_dwb[l], cm_ln_g[l], cm_ln_b[l],
                               cm_w2[l], cm_b2[l])
        else:
            if l == N_CONV_LAYERS:
                hkv = rms_norm(x, kv_norm_g)
                k_shared = (hkv @ w_k).reshape(bsz, seq, N_HEADS, 2, HEAD_DIM)
                v_shared = (hkv @ w_v).reshape(bsz, seq, N_HEADS, V_HEAD_DIM)
            j = l - N_CONV_LAYERS
            lambda_init = 0.8 - 0.6 * math.exp(-0.3 * l)
            y = diff_attention(h, k_shared, v_shared, w_q[j], lam[j], subln_g[j], w_o[j], rel_bias, lambda_init)
        x = x + g_m * rms_norm(y, norm_g[l, 1])

        h = rms_norm(x, norm_g[l, 2]) * (1.0 + sc_f) + sh_f
        y = conv_ffn(h, ffn_w_in[l], ffn_dw[l], ffn_dwb[l], ffn_w_out[l])
        x = x + g_f * rms_norm(y, norm_g[l, 3])
    return x
```

```python
import functools
import math

import jax
import jax.numpy as jnp
from jax import lax
from jax.experimental import pallas as pl
from jax.experimental.pallas import tpu as pltpu

N_HEADS = 8
HEAD_DIM = 64
V_HEAD_DIM = 128
CONV_WIDTH = 31
FFN_CONV_WIDTH = 3
N_BUCKETS = 32
MAX_DISTANCE = 128
MAX_EXACT = 16
EPS = 1e-6

ROW_TILE = 512
ATTN_TILE = 256
CONV_HALO = 32
CONV_ROWS = 64
CONV_LANES = 128
FFN_CHUNK = 256
FFN_HALO = 8
MASK_VALUE = -1e30
VMEM_LIMIT = 56 * 1024 * 1024

_BF16 = jnp.bfloat16
_F32 = jnp.float32


def _bucket_of_distance(n):
    if n < MAX_EXACT:
        return n
    large = MAX_EXACT + int(math.log(n / MAX_EXACT) / math.log(MAX_DISTANCE / MAX_EXACT) * (N_BUCKETS - MAX_EXACT))
    return min(large, N_BUCKETS - 1)


def _bucket_segments():
    segs = []
    for n in range(MAX_DISTANCE):
        b = _bucket_of_distance(n)
        if not segs or segs[-1][1] != b:
            segs.append((n, b))
    assert all(_bucket_of_distance(n) == N_BUCKETS - 1 for n in range(segs[-1][0], 4 * MAX_DISTANCE))
    return segs


def _rms(x):
    return x * lax.rsqrt(jnp.mean(x * x, axis=-1, keepdims=True) + EPS)


def _dot(a, b):
    return jnp.dot(a, b, preferred_element_type=_F32)


def _params(*sem):
    return pltpu.CompilerParams(dimension_semantics=sem, vmem_limit_bytes=VMEM_LIMIT)


def _const_spec(shape):
    nd = len(shape)
    return pl.BlockSpec(shape, lambda *_: (0,) * nd)


def _mod_kernel(c_ref, w_ref, b_ref, o_ref):
    c = c_ref[...]
    c_act = c * jax.nn.sigmoid(c)
    o_ref[0] = _dot(c_act, w_ref[0]) + b_ref[0]


def _modulation(c, mod_w, mod_b):
    depth, d, n = mod_w.shape
    bsz = c.shape[0]
    tn = 1024
    return pl.pallas_call(
        _mod_kernel,
        grid=(depth, n // tn),
        in_specs=[
            pl.BlockSpec((bsz, d), lambda l, j: (0, 0)),
            pl.BlockSpec((1, d, tn), lambda l, j: (l, 0, j)),
            pl.BlockSpec((1, 1, tn), lambda l, j: (l, 0, j)),
        ],
        out_specs=pl.BlockSpec((1, bsz, tn), lambda l, j: (l, 0, j)),
        out_shape=jax.ShapeDtypeStruct((depth, bsz, n), _F32),
        compiler_params=_params("arbitrary", "arbitrary"),
        name="mod",
    )(c, mod_w, mod_b.reshape(depth, 1, n))


def _conv_a_kernel(x_ref, g_ref, sc_ref, sh_ref, w1_ref, b1_ref, u_ref):
    d = x_ref.shape[-1]
    h = _rms(x_ref[0]) * g_ref[...] * (1.0 + sc_ref[0]) + sh_ref[0]
    hb = h.astype(_BF16)
    a = _dot(hb, w1_ref[:, :d]) + b1_ref[:, :d]
    gt = _dot(hb, w1_ref[:, d:]) + b1_ref[:, d:]
    u_ref[0] = a * jax.nn.sigmoid(gt)


def _conv_a(x, g, sc, sh, w1, b1):
    bsz, seq, d = x.shape
    tm = ROW_TILE
    vec = pl.BlockSpec((1, 1, d), lambda b, i: (b, 0, 0))
    return pl.pallas_call(
        _conv_a_kernel,
        grid=(bsz, seq // tm),
        in_specs=[
            pl.BlockSpec((1, tm, d), lambda b, i: (b, i, 0)),
            _const_spec((1, d)), vec, vec,
            _const_spec((d, 2 * d)), _const_spec((1, 2 * d)),
        ],
        out_specs=pl.BlockSpec((1, tm, d), lambda b, i: (b, i, 0)),
        out_shape=jax.ShapeDtypeStruct((bsz, seq, d), _F32),
        compiler_params=_params("parallel", "parallel"),
        name="conv_a",
    )(x, g, sc, sh, w1, b1)


def _conv_b_kernel(x_ref, ucur_ref, uprev_ref, dw_ref, dwb_ref, lng_ref, lnb_ref, w2_ref, b2_ref,
                   g_ref, gate_ref, o_ref, ubuf, cbuf):
    tm, d = ucur_ref.shape[1], ucur_ref.shape[2]
    i = pl.program_id(1)
    prev = uprev_ref[0]
    ubuf[0:CONV_HALO, :] = jnp.where(i > 0, prev, jnp.zeros_like(prev))
    ubuf[CONV_HALO:, :] = ucur_ref[0]

    rows = CONV_ROWS + CONV_HALO
    first_shift = CONV_HALO - (CONV_WIDTH - 1)

    def row_block(r, carry):
        r0 = pl.multiple_of(r * CONV_ROWS, CONV_ROWS)
        for c0 in range(0, d, CONV_LANES):
            xs = ubuf[pl.ds(r0, rows), c0:c0 + CONV_LANES]
            acc = jnp.broadcast_to(dwb_ref[:, c0:c0 + CONV_LANES], (CONV_ROWS, CONV_LANES))
            for sub in range(8):
                taps = [j for j in range(CONV_WIDTH) if (first_shift + j) % 8 == sub]
                if not taps:
                    continue
                xb = xs if sub == 0 else pltpu.roll(xs, rows - sub, 0)
                for j in taps:
                    a8 = (first_shift + j) - sub
                    acc = acc + xb[a8:a8 + CONV_ROWS, :] * dw_ref[j:j + 1, c0:c0 + CONV_LANES]
            cbuf[pl.ds(r0, CONV_ROWS), c0:c0 + CONV_LANES] = acc
        return carry

    lax.fori_loop(0, tm // CONV_ROWS, row_block, 0)

    cv = cbuf[...]
    mu = jnp.mean(cv, axis=-1, keepdims=True)
    cc = cv - mu
    var = jnp.mean(cc * cc, axis=-1, keepdims=True)
    z = cc * lax.rsqrt(var + EPS) * lng_ref[...] + lnb_ref[...]
    z = z * jax.nn.sigmoid(z)
    y = _dot(z.astype(_BF16), w2_ref[...]) + b2_ref[...]
    o_ref[0] = x_ref[0] + gate_ref[0] * (_rms(y) * g_ref[...])


def _conv_b(x, u, dw, dwb, lng, lnb, w2, b2, g, gate):
    bsz, seq, d = x.shape
    tm = ROW_TILE
    ratio = tm // CONV_HALO
    vec = pl.BlockSpec((1, 1, d), lambda b, i: (b, 0, 0))
    tile = pl.BlockSpec((1, tm, d), lambda b, i: (b, i, 0))
    return pl.pallas_call(
        _conv_b_kernel,
        grid=(bsz, seq // tm),
        in_specs=[
            tile, tile,
            pl.BlockSpec((1, CONV_HALO, d), lambda b, i: (b, jnp.maximum(i * ratio - 1, 0), 0)),
            _const_spec((CONV_WIDTH, d)), _const_spec((1, d)), _const_spec((1, d)), _const_spec((1, d)),
            _const_spec((d, d)), _const_spec((1, d)), _const_spec((1, d)), vec,
        ],
        out_specs=tile,
        out_shape=jax.ShapeDtypeStruct((bsz, seq, d), _F32),
        scratch_shapes=[pltpu.VMEM((tm + CONV_HALO, d), _F32), pltpu.VMEM((tm, d), _F32)],
        compiler_params=_params("parallel", "parallel"),
        name="conv_b",
    )(x, u, u, dw, dwb, lng, lnb, w2, b2, g, gate)


def _ffn_kernel(x_ref, g_ref, sc_ref, sh_ref, win_ref, dw_ref, dwb_ref, wout_ref, g3_ref, gate_ref, o_ref,
                hbuf, ubuf, carry, pbuf):
    tm = x_ref.shape[1]
    f = wout_ref.shape[0]
    i = pl.program_id(1)

    @pl.when(i == 0)
    def _():
        carry[...] = jnp.zeros_like(carry)

    h = _rms(x_ref[0]) * g_ref[...] * (1.0 + sc_ref[0]) + sh_ref[0]
    hbuf[...] = h.astype(_BF16)

    def conv3(col0):
        cols = slice(col0, col0 + FFN_CHUNK)
        u = _dot(hbuf[...], win_ref[:, cols])
        ubuf[0:FFN_HALO, :] = carry[:, cols]
        ubuf[FFN_HALO:, :] = u
        carry[:, cols] = u[tm - FFN_HALO:, :]
        out = dwb_ref[:, cols]
        for j in range(FFN_CONV_WIDTH):
            shift = FFN_CONV_WIDTH - 1 - j
            out = out + ubuf[pl.ds(FFN_HALO - shift, tm), :] * dw_ref[j:j + 1, cols]
        return out

    for c0 in range(0, f, FFN_CHUNK):
        gpart = conv3(c0)
        vpart = conv3(f + c0)
        pbuf[:, c0:c0 + FFN_CHUNK] = (gpart * jax.nn.sigmoid(gpart) * vpart).astype(_BF16)

    y = _dot(pbuf[...], wout_ref[...])
    o_ref[0] = x_ref[0] + gate_ref[0] * (_rms(y) * g3_ref[...])


def _ffn(x, g, sc, sh, w_in, dw, dwb, w_out, g3, gate):
    bsz, seq, d = x.shape
    f = w_out.shape[0]
    tm = ROW_TILE
    vec = pl.BlockSpec((1, 1, d), lambda b, i: (b, 0, 0))
    tile = pl.BlockSpec((1, tm, d), lambda b, i: (b, i, 0))
    single = dict(pipeline_mode=pl.Buffered(1))
    return pl.pallas_call(
        _ffn_kernel,
        grid=(bsz, seq // tm),
        in_specs=[
            tile, _const_spec((1, d)), vec, vec,
            pl.BlockSpec((d, 2 * f), lambda b, i: (0, 0), **single),
            _const_spec((FFN_CONV_WIDTH, 2 * f)), _const_spec((1, 2 * f)),
            pl.BlockSpec((f, d), lambda b, i: (0, 0), **single),
            _const_spec((1, d)), vec,
        ],
        out_specs=tile,
        out_shape=jax.ShapeDtypeStruct((bsz, seq, d), _F32),
        scratch_shapes=[
            pltpu.VMEM((tm, d), _BF16),
            pltpu.VMEM((tm + FFN_HALO, FFN_CHUNK), _F32),
            pltpu.VMEM((FFN_HALO, 2 * f), _F32),
            pltpu.VMEM((tm, f), _BF16),
        ],
        compiler_params=_params("arbitrary", "arbitrary"),
        name="ffn",
    )(x, g, sc, sh, w_in, dw, dwb, w_out, g3, gate)


def _kv_kernel(x_ref, g_ref, wkt_ref, wv_ref, kt_ref, v_ref):
    tm = x_ref.shape[1]
    tk = kt_ref.shape[-1]
    hb = (_rms(x_ref[0]) * g_ref[...]).astype(_BF16)
    kt = lax.dot_general(wkt_ref[...], hb, (((1,), (1,)), ((), ())), preferred_element_type=_F32)
    v = _dot(hb, wv_ref[...])
    for hd in range(N_HEADS):
        rows = slice(hd * V_HEAD_DIM, (hd + 1) * V_HEAD_DIM)
        for c in range(tm // tk):
            kt_ref[0, hd, c] = kt[rows, c * tk:(c + 1) * tk].astype(_BF16)
        v_ref[0, hd] = v[:, rows].astype(_BF16)


def _kv(x, g, wkt, wv):
    bsz, seq, d = x.shape
    tm, tk = ROW_TILE, ATTN_TILE
    return pl.pallas_call(
        _kv_kernel,
        grid=(bsz, seq // tm),
        in_specs=[
            pl.BlockSpec((1, tm, d), lambda b, i: (b, i, 0)),
            _const_spec((1, d)), _const_spec((d, d)), _const_spec((d, d)),
        ],
        out_specs=[
            pl.BlockSpec((1, N_HEADS, tm // tk, V_HEAD_DIM, tk), lambda b, i: (b, 0, i, 0, 0)),
            pl.BlockSpec((1, N_HEADS, tm, V_HEAD_DIM), lambda b, i: (b, 0, i, 0)),
        ],
        out_shape=[
            jax.ShapeDtypeStruct((bsz, N_HEADS, seq // tk, V_HEAD_DIM, tk), _BF16),
            jax.ShapeDtypeStruct((bsz, N_HEADS, seq, V_HEAD_DIM), _BF16),
        ],
        compiler_params=_params("parallel", "parallel"),
        name="kv",
    )(x, g, wkt, wv)


def _q_kernel(x_ref, g_ref, sc_ref, sh_ref, wq_ref, q_ref):
    tm = x_ref.shape[1]
    h = _rms(x_ref[0]) * g_ref[...] * (1.0 + sc_ref[0]) + sh_ref[0]
    q = _dot(h.astype(_BF16), wq_ref[...]) * (HEAD_DIM ** -0.5)
    lane = lax.broadcasted_iota(jnp.int32, (tm, V_HEAD_DIM), 1)
    first = lane < HEAD_DIM
    for hd in range(N_HEADS):
        qh = q[:, hd * V_HEAD_DIM:(hd + 1) * V_HEAD_DIM]
        q_ref[0, hd, 0] = jnp.where(first, qh, 0.0).astype(_BF16)
        q_ref[0, hd, 1] = jnp.where(first, 0.0, qh).astype(_BF16)


def _qproj(x, g, sc, sh, wq):
    bsz, seq, d = x.shape
    tm = ROW_TILE
    vec = pl.BlockSpec((1, 1, d), lambda b, i: (b, 0, 0))
    return pl.pallas_call(
        _q_kernel,
        grid=(bsz, seq // tm),
        in_specs=[pl.BlockSpec((1, tm, d), lambda b, i: (b, i, 0)), _const_spec((1, d)), vec, vec,
                  _const_spec((d, d))],
        out_specs=pl.BlockSpec((1, N_HEADS, 2, tm, V_HEAD_DIM), lambda b, i: (b, 0, 0, i, 0)),
        out_shape=jax.ShapeDtypeStruct((bsz, N_HEADS, 2, seq, V_HEAD_DIM), _BF16),
        compiler_params=_params("parallel", "parallel"),
        name="qproj",
    )(x, g, sc, sh, wq)


def _bias_kernel(rb_ref, o_ref):
    hd = pl.program_id(0)
    t = o_ref.shape[-1]
    row = lax.broadcasted_iota(jnp.int32, (t, t), 0)
    col = lax.broadcasted_iota(jnp.int32, (t, t), 1)
    segs = _bucket_segments()
    far = rb_ref[N_BUCKETS - 1, hd]
    for which in range(2):
        rel = row - col + which * t
        val = jnp.full((t, t), 0.0, _F32)
        for k in range(len(segs) - 2, -1, -1):
            val = jnp.where(rel < segs[k + 1][0], rb_ref[segs[k][1], hd] - far, val)
        o_ref[0, which] = jnp.where(rel >= 0, val, MASK_VALUE)


def _bias_tiles(rel_bias):
    t = ATTN_TILE
    return pl.pallas_call(
        _bias_kernel,
        grid=(N_HEADS,),
        in_specs=[pl.BlockSpec(memory_space=pltpu.SMEM)],
        out_specs=pl.BlockSpec((1, 2, t, t), lambda h: (h, 0, 0, 0)),
        out_shape=jax.ShapeDtypeStruct((N_HEADS, 2, t, t), _F32),
        compiler_params=_params("arbitrary"),
        name="bias",
    )(rel_bias)


def _attn_kernel(q_ref, kt_ref, v_ref, bias_ref, lam_ref, sg_ref, o_ref, m_sc, l_sc, acc_sc, *, lambda_init):
    t = ATTN_TILE
    i = pl.program_id(2)
    q = q_ref[0, 0].reshape(2 * t, V_HEAD_DIM)
    m_sc[...] = jnp.full_like(m_sc, MASK_VALUE)
    l_sc[...] = jnp.zeros_like(l_sc)
    acc_sc[...] = jnp.zeros_like(acc_sc)

    def step(j, bias):
        s = _dot(q, kt_ref[0, 0, j])
        if bias is not None:
            s = (s.reshape(2, t, t) + bias[None]).reshape(2 * t, t)
        m_prev = m_sc[...]
        m_new = jnp.maximum(m_prev, jnp.max(s, axis=-1, keepdims=True))
        alpha = jnp.exp(m_prev - m_new)
        p = jnp.exp(s - m_new)
        l_sc[...] = alpha * l_sc[...] + jnp.sum(p, axis=-1, keepdims=True)
        vv = v_ref[0, 0, pl.ds(pl.multiple_of(j * t, t), t), :]
        acc_sc[...] = alpha * acc_sc[...] + _dot(p.astype(_BF16), vv)
        m_sc[...] = m_new

    def far_step(j, carry):
        step(j, None)
        return carry

    lax.fori_loop(0, jnp.maximum(i - 1, 0), far_step, 0)

    @pl.when(i >= 1)
    def _():
        step(i - 1, bias_ref[0, 1])

    step(i, bias_ref[0, 0])

    lv = lam_ref[...]
    lam = (jnp.exp(jnp.sum(lv[0:1] * lv[1:2], axis=-1, keepdims=True))
           - jnp.exp(jnp.sum(lv[2:3] * lv[3:4], axis=-1, keepdims=True)) + lambda_init)
    o = acc_sc[...] / l_sc[...]
    o = o[:t] - lam * o[t:]
    o_ref[0] = (_rms(o) * sg_ref[...] * (1.0 - lambda_init)).astype(o_ref.dtype)


def _attention(q2, kt, v, bias, lam, sg, lambda_init):
    bsz, n_heads, _, seq, dv = q2.shape
    t = ATTN_TILE
    nt = seq // t
    return pl.pallas_call(
        functools.partial(_attn_kernel, lambda_init=lambda_init),
        grid=(bsz, n_heads, nt),
        in_specs=[
            pl.BlockSpec((1, 1, 2, t, dv), lambda b, h, i: (b, h, 0, i, 0)),
            pl.BlockSpec((1, 1, nt, dv, t), lambda b, h, i: (b, h, 0, 0, 0)),
            pl.BlockSpec((1, 1, seq, dv), lambda b, h, i: (b, h, 0, 0)),
            pl.BlockSpec((1, 2, t, t), lambda b, h, i: (h, 0, 0, 0)),
            pl.BlockSpec((4, HEAD_DIM), lambda b, h, i: (0, 0)),
            pl.BlockSpec((1, dv), lambda b, h, i: (0, 0)),
        ],
        out_specs=pl.BlockSpec((1, t, dv), lambda b, h, i: (b, i, h)),
        out_shape=jax.ShapeDtypeStruct((bsz, seq, n_heads * dv), _BF16),
        scratch_shapes=[pltpu.VMEM((2 * t, 1), _F32), pltpu.VMEM((2 * t, 1), _F32),
                        pltpu.VMEM((2 * t, dv), _F32)],
        compiler_params=_params("parallel", "parallel", "arbitrary"),
        name="attn",
    )(q2, kt, v, bias, lam, sg)


def _o_kernel(x_ref, a_ref, wo_ref, g_ref, gate_ref, o_ref):
    y = _dot(a_ref[0], wo_ref[...])
    o_ref[0] = x_ref[0] + gate_ref[0] * (_rms(y) * g_ref[...])


def _oproj(x, a, wo, g, gate):
    bsz, seq, d = x.shape
    tm = ROW_TILE
    tile = pl.BlockSpec((1, tm, d), lambda b, i: (b, i, 0))
    return pl.pallas_call(
        _o_kernel,
        grid=(bsz, seq // tm),
        in_specs=[tile, tile, _const_spec((d, d)), _const_spec((1, d)),
                  pl.BlockSpec((1, 1, d), lambda b, i: (b, 0, 0))],
        out_specs=tile,
        out_shape=jax.ShapeDtypeStruct((bsz, seq, d), _F32),
        compiler_params=_params("parallel", "parallel"),
        name="oproj",
    )(x, a, wo, g, gate)


def kernel(x, c, mod_w, mod_b, norm_g, cm_w1, cm_b1, cm_dw, cm_dwb, cm_ln_g, cm_ln_b, cm_w2, cm_b2, kv_norm_g,
           w_k, w_v, w_q, lam, subln_g, w_o, rel_bias, ffn_w_in, ffn_dw, ffn_dwb, ffn_w_out):
    depth = mod_w.shape[0]
    n_conv = cm_w1.shape[0]
    bsz, seq, d = x.shape

    mod = _modulation(c, mod_w, mod_b)

    def row(v):
        return v.reshape(1, -1)

    kt = vv = bias = None
    for l in range(depth):
        sh_m, sc_m, g_m, sh_f, sc_f, g_f = [m.reshape(bsz, 1, d) for m in jnp.split(mod[l], 6, axis=-1)]
        g0, g1, g2, g3 = [row(norm_g[l, k]) for k in range(4)]
        if l < n_conv:
            u = _conv_a(x, g0, sc_m, sh_m, cm_w1[l].astype(_BF16), row(cm_b1[l]))
            x = _conv_b(x, u, cm_dw[l], row(cm_dwb[l]), row(cm_ln_g[l]), row(cm_ln_b[l]),
                        cm_w2[l].astype(_BF16), row(cm_b2[l]), g1, g_m)
        else:
            if l == n_conv:
                kt, vv = _kv(x, row(kv_norm_g), w_k.T.astype(_BF16), w_v.astype(_BF16))
                bias = _bias_tiles(rel_bias)
            j = l - n_conv
            lambda_init = 0.8 - 0.6 * math.exp(-0.3 * l)
            q2 = _qproj(x, g0, sc_m, sh_m, w_q[j].astype(_BF16))
            a = _attention(q2, kt, vv, bias, lam[j], row(subln_g[j]), lambda_init)
            x = _oproj(x, a, w_o[j].astype(_BF16), g1, g_m)
        x = _ffn(x, g2, sc_f, sh_f, ffn_w_in[l].astype(_BF16), ffn_dw[l], row(ffn_dwb[l]),
                 ffn_w_out[l].astype(_BF16), g3, g_f)
    return x
```

```python
import functools
import math

import jax
import jax.numpy as jnp
from jax import lax
from jax.experimental import pallas as pl
from jax.experimental.pallas import tpu as pltpu

N_HEADS = 8
HEAD_DIM = 64
V_HEAD_DIM = 128
CONV_WIDTH = 31
FFN_CONV_WIDTH = 3
N_BUCKETS = 32
MAX_DISTANCE = 128
MAX_EXACT = 16
EPS = 1e-6

ROW_TILE = 512
ATTN_TILE = 256
CONV_HALO = 32
CONV_ROWS = 64
CONV_LANES = 128
FFN_CHUNK = 256
FFN_HALO = 8
MASK_VALUE = -1e30
VMEM_LIMIT = 56 * 1024 * 1024

_BF16 = jnp.bfloat16
_F32 = jnp.float32


def _bucket_of_distance(n):
    if n < MAX_EXACT:
        return n
    large = MAX_EXACT + int(math.log(n / MAX_EXACT) / math.log(MAX_DISTANCE / MAX_EXACT) * (N_BUCKETS - MAX_EXACT))
    return min(large, N_BUCKETS - 1)


def _bucket_segments():
    segs = []
    for n in range(MAX_DISTANCE):
        b = _bucket_of_distance(n)
        if not segs or segs[-1][1] != b:
            segs.append((n, b))
    assert all(_bucket_of_distance(n) == N_BUCKETS - 1 for n in range(segs[-1][0], 4 * MAX_DISTANCE))
    return segs


def _rms(x):
    return x * lax.rsqrt(jnp.mean(x * x, axis=-1, keepdims=True) + EPS)


def _dot(a, b):
    return jnp.dot(a, b, preferred_element_type=_F32)


def _params(*sem):
    return pltpu.CompilerParams(dimension_semantics=sem, vmem_limit_bytes=VMEM_LIMIT)


def _const_spec(shape):
    nd = len(shape)
    return pl.BlockSpec(shape, lambda *_: (0,) * nd)


def _mod_kernel(c_ref, w_ref, b_ref, o_ref):
    c = c_ref[...]
    c_act = c * jax.nn.sigmoid(c)
    o_ref[0] = _dot(c_act, w_ref[0]) + b_ref[0]


def _modulation(c, mod_w, mod_b):
    depth, d, n = mod_w.shape
    bsz = c.shape[0]
    tn = 1024
    return pl.pallas_call(
        _mod_kernel,
        grid=(depth, n // tn),
        in_specs=[
            pl.BlockSpec((bsz, d), lambda l, j: (0, 0)),
            pl.BlockSpec((1, d, tn), lambda l, j: (l, 0, j)),
            pl.BlockSpec((1, 1, tn), lambda l, j: (l, 0, j)),
        ],
        out_specs=pl.BlockSpec((1, bsz, tn), lambda l, j: (l, 0, j)),
        out_shape=jax.ShapeDtypeStruct((depth, bsz, n), _F32),
        compiler_params=_params("arbitrary", "arbitrary"),
        name="mod",
    )(c, mod_w, mod_b.reshape(depth, 1, n))


def _conv_a_kernel(x_ref, g_ref, sc_ref, sh_ref, w1_ref, b1_ref, u_ref):
    d = x_ref.shape[-1]
    h = _rms(x_ref[0]) * g_ref[...] * (1.0 + sc_ref[0]) + sh_ref[0]
    hb = h.astype(_BF16)
    a = _dot(hb, w1_ref[:, :d]) + b1_ref[:, :d]
    gt = _dot(hb, w1_ref[:, d:]) + b1_ref[:, d:]
    u_ref[0] = a * jax.nn.sigmoid(gt)


def _conv_a(x, g, sc, sh, w1, b1):
    bsz, seq, d = x.shape
    tm = ROW_TILE
    vec = pl.BlockSpec((1, 1, d), lambda b, i: (b, 0, 0))
    return pl.pallas_call(
        _conv_a_kernel,
        grid=(bsz, seq // tm),
        in_specs=[
            pl.BlockSpec((1, tm, d), lambda b, i: (b, i, 0)),
            _const_spec((1, d)), vec, vec,
            _const_spec((d, 2 * d)), _const_spec((1, 2 * d)),
        ],
        out_specs=pl.BlockSpec((1, tm, d), lambda b, i: (b, i, 0)),
        out_shape=jax.ShapeDtypeStruct((bsz, seq, d), _F32),
        compiler_params=_params("parallel", "parallel"),
        name="conv_a",
    )(x, g, sc, sh, w1, b1)


def _conv_b_kernel(x_ref, ucur_ref, uprev_ref, dw_ref, dwb_ref, lng_ref, lnb_ref, w2_ref, b2_ref,
                   g_ref, gate_ref, o_ref, ubuf, cbuf):
    tm, d = ucur_ref.shape[1], ucur_ref.shape[2]
    i = pl.program_id(1)
    prev = uprev_ref[0]
    ubuf[0:CONV_HALO, :] = jnp.where(i > 0, prev, jnp.zeros_like(prev))
    ubuf[CONV_HALO:, :] = ucur_ref[0]

    rows = CONV_ROWS + CONV_HALO
    first_shift = CONV_HALO - (CONV_WIDTH - 1)

    def row_block(r, carry):
        r0 = pl.multiple_of(r * CONV_ROWS, CONV_ROWS)
        for c0 in range(0, d, CONV_LANES):
            xs = ubuf[pl.ds(r0, rows), c0:c0 + CONV_LANES]
            acc = jnp.broadcast_to(dwb_ref[:, c0:c0 + CONV_LANES], (CONV_ROWS, CONV_LANES))
            for sub in range(8):
                taps = [j for j in range(CONV_WIDTH) if (first_shift + j) % 8 == sub]
                if not taps:
                    continue
                xb = xs if sub == 0 else pltpu.roll(xs, rows - sub, 0)
                for j in taps:
                    a8 = (first_shift + j) - sub
                    acc = acc + xb[a8:a8 + CONV_ROWS, :] * dw_ref[j:j + 1, c0:c0 + CONV_LANES]
            cbuf[pl.ds(r0, CONV_ROWS), c0:c0 + CONV_LANES] = acc
        return carry

    lax.fori_loop(0, tm // CONV_ROWS, row_block, 0)

    cv = cbuf[...]
    mu = jnp.mean(cv, axis=-1, keepdims=True)
    cc = cv - mu
    var = jnp.mean(cc * cc, axis=-1, keepdims=True)
    z = cc * lax.rsqrt(var + EPS) * lng_ref[...] + lnb_ref[...]
    z = z * jax.nn.sigmoid(z)
    y = _dot(z.astype(_BF16), w2_ref[...]) + b2_ref[...]
    o_ref[0] = x_ref[0] + gate_ref[0] * (_rms(y) * g_ref[...])


def _conv_b(x, u, dw, dwb, lng, lnb, w2, b2, g, gate):
    bsz, seq, d = x.shape
    tm = ROW_TILE
    ratio = tm // CONV_HALO
    vec = pl.BlockSpec((1, 1, d), lambda b, i: (b, 0, 0))
    tile = pl.BlockSpec((1, tm, d), lambda b, i: (b, i, 0))
    return pl.pallas_call(
        _conv_b_kernel,
        grid=(bsz, seq // tm),
        in_specs=[
            tile, tile,
            pl.BlockSpec((1, CONV_HALO, d), lambda b, i: (b, jnp.maximum(i * ratio - 1, 0), 0)),
            _const_spec((CONV_WIDTH, d)), _const_spec((1, d)), _const_spec((1, d)), _const_spec((1, d)),
            _const_spec((d, d)), _const_spec((1, d)), _const_spec((1, d)), vec,
        ],
        out_specs=tile,
        out_shape=jax.ShapeDtypeStruct((bsz, seq, d), _F32),
        scratch_shapes=[pltpu.VMEM((tm + CONV_HALO, d), _F32), pltpu.VMEM((tm, d), _F32)],
        compiler_params=_params("parallel", "parallel"),
        name="conv_b",
    )(x, u, u, dw, dwb, lng, lnb, w2, b2, g, gate)


def _ffn_kernel(x_ref, g_ref, sc_ref, sh_ref, win_ref, dw_ref, dwb_ref, wout_ref, g3_ref, gate_ref, o_ref,
                hbuf, ubuf, carry, pbuf):
    tm = x_ref.shape[1]
    f = wout_ref.shape[0]
    i = pl.program_id(1)

    @pl.when(i == 0)
    def _():
        carry[...] = jnp.zeros_like(carry)

    h = _rms(x_ref[0]) * g_ref[...] * (1.0 + sc_ref[0]) + sh_ref[0]
    hbuf[...] = h.astype(_BF16)

    def conv3(col0):
        cols = slice(col0, col0 + FFN_CHUNK)
        u = _dot(hbuf[...], win_ref[:, cols])
        ubuf[0:FFN_HALO, :] = carry[:, cols]
        ubuf[FFN_HALO:, :] = u
        carry[:, cols] = u[tm - FFN_HALO:, :]
        out = dwb_ref[:, cols]
        for j in range(FFN_CONV_WIDTH):
            shift = FFN_CONV_WIDTH - 1 - j
            out = out + ubuf[pl.ds(FFN_HALO - shift, tm), :] * dw_ref[j:j + 1, cols]
        return out

    for c0 in range(0, f, FFN_CHUNK):
        gpart = conv3(c0)
        vpart = conv3(f + c0)
        pbuf[:, c0:c0 + FFN_CHUNK] = (gpart * jax.nn.sigmoid(gpart) * vpart).astype(_BF16)

    y = _dot(pbuf[...], wout_ref[...])
    o_ref[0] = x_ref[0] + gate_ref[0] * (_rms(y) * g3_ref[...])


def _ffn(x, g, sc, sh, w_in, dw, dwb, w_out, g3, gate):
    bsz, seq, d = x.shape
    f = w_out.shape[0]
    tm = ROW_TILE
    vec = pl.BlockSpec((1, 1, d), lambda b, i: (b, 0, 0))
    tile = pl.BlockSpec((1, tm, d), lambda b, i: (b, i, 0))
    single = dict(pipeline_mode=pl.Buffered(1))
    return pl.pallas_call(
        _ffn_kernel,
        grid=(bsz, seq // tm),
        in_specs=[
            tile, _const_spec((1, d)), vec, vec,
            pl.BlockSpec((d, 2 * f), lambda b, i: (0, 0), **single),
            _const_spec((FFN_CONV_WIDTH, 2 * f)), _const_spec((1, 2 * f)),
            pl.BlockSpec((f, d), lambda b, i: (0, 0), **single),
            _const_spec((1, d)), vec,
        ],
        out_specs=tile,
        out_shape=jax.ShapeDtypeStruct((bsz, seq, d), _F32),
        scratch_shapes=[
            pltpu.VMEM((tm, d), _BF16),
            pltpu.VMEM((tm + FFN_HALO, FFN_CHUNK), _F32),
            pltpu.VMEM((FFN_HALO, 2 * f), _F32),
            pltpu.VMEM((tm, f), _BF16),
        ],
        compiler_params=_params("arbitrary", "arbitrary"),
        name="ffn",
    )(x, g, sc, sh, w_in, dw, dwb, w_out, g3, gate)


def _kv_kernel(x_ref, g_ref, wk_ref, wvt_ref, k_ref, vt_ref):
    tm = x_ref.shape[1]
    tk = vt_ref.shape[-1]
    hb = (_rms(x_ref[0]) * g_ref[...]).astype(_BF16)
    k = _dot(hb, wk_ref[...])
    vt = lax.dot_general(wvt_ref[...], hb, (((1,), (1,)), ((), ())), preferred_element_type=_F32)
    for hd in range(N_HEADS):
        rows = slice(hd * V_HEAD_DIM, (hd + 1) * V_HEAD_DIM)
        k_ref[0, hd] = k[:, rows].astype(_BF16)
        for c in range(tm // tk):
            vt_ref[0, hd, c] = vt[rows, c * tk:(c + 1) * tk].astype(_BF16)


def _kv(x, g, wk, wvt):
    bsz, seq, d = x.shape
    tm, tk = ROW_TILE, ATTN_TILE
    return pl.pallas_call(
        _kv_kernel,
        grid=(bsz, seq // tm),
        in_specs=[
            pl.BlockSpec((1, tm, d), lambda b, i: (b, i, 0)),
            _const_spec((1, d)), _const_spec((d, d)), _const_spec((d, d)),
        ],
        out_specs=[
            pl.BlockSpec((1, N_HEADS, tm, V_HEAD_DIM), lambda b, i: (b, 0, i, 0)),
            pl.BlockSpec((1, N_HEADS, tm // tk, V_HEAD_DIM, tk), lambda b, i: (b, 0, i, 0, 0)),
        ],
        out_shape=[
            jax.ShapeDtypeStruct((bsz, N_HEADS, seq, V_HEAD_DIM), _BF16),
            jax.ShapeDtypeStruct((bsz, N_HEADS, seq // tk, V_HEAD_DIM, tk), _BF16),
        ],
        compiler_params=_params("parallel", "parallel"),
        name="kv",
    )(x, g, wk, wvt)


def _q_kernel(x_ref, g_ref, sc_ref, sh_ref, wqt_ref, q_ref):
    tm = x_ref.shape[1]
    tq = q_ref.shape[-1] // 2
    h = _rms(x_ref[0]) * g_ref[...] * (1.0 + sc_ref[0]) + sh_ref[0]
    qt = lax.dot_general(wqt_ref[...], h.astype(_BF16), (((1,), (1,)), ((), ())),
                         preferred_element_type=_F32) * (HEAD_DIM ** -0.5)
    first = lax.broadcasted_iota(jnp.int32, (V_HEAD_DIM, tq), 0) < HEAD_DIM
    for hd in range(N_HEADS):
        for c in range(tm // tq):
            qh = qt[hd * V_HEAD_DIM:(hd + 1) * V_HEAD_DIM, c * tq:(c + 1) * tq]
            q_ref[0, hd, c, :, :tq] = jnp.where(first, qh, 0.0).astype(_BF16)
            q_ref[0, hd, c, :, tq:] = jnp.where(first, 0.0, qh).astype(_BF16)


def _qproj(x, g, sc, sh, wqt):
    bsz, seq, d = x.shape
    tm, tq = ROW_TILE, ATTN_TILE
    vec = pl.BlockSpec((1, 1, d), lambda b, i: (b, 0, 0))
    return pl.pallas_call(
        _q_kernel,
        grid=(bsz, seq // tm),
        in_specs=[pl.BlockSpec((1, tm, d), lambda b, i: (b, i, 0)), _const_spec((1, d)), vec, vec,
                  _const_spec((d, d))],
        out_specs=pl.BlockSpec((1, N_HEADS, tm // tq, V_HEAD_DIM, 2 * tq), lambda b, i: (b, 0, i, 0, 0)),
        out_shape=jax.ShapeDtypeStruct((bsz, N_HEADS, seq // tq, V_HEAD_DIM, 2 * tq), _BF16),
        compiler_params=_params("parallel", "parallel"),
        name="qproj",
    )(x, g, sc, sh, wqt)


def _bias_kernel(rb_ref, o_ref):
    hd = pl.program_id(0)
    t = o_ref.shape[-1]
    key = lax.broadcasted_iota(jnp.int32, (t, t), 0)
    qry = lax.broadcasted_iota(jnp.int32, (t, t), 1)
    segs = _bucket_segments()
    far = rb_ref[N_BUCKETS - 1, hd]
    for which in range(2):
        rel = qry - key + which * t
        val = jnp.full((t, t), 0.0, _F32)
        for k in range(len(segs) - 2, -1, -1):
            val = jnp.where(rel < segs[k + 1][0], rb_ref[segs[k][1], hd] - far, val)
        o_ref[0, which] = jnp.where(rel >= 0, val, MASK_VALUE)


def _bias_tiles(rel_bias):
    t = ATTN_TILE
    return pl.pallas_call(
        _bias_kernel,
        grid=(N_HEADS,),
        in_specs=[pl.BlockSpec(memory_space=pltpu.SMEM)],
        out_specs=pl.BlockSpec((1, 2, t, t), lambda h: (h, 0, 0, 0)),
        out_shape=jax.ShapeDtypeStruct((N_HEADS, 2, t, t), _F32),
        compiler_params=_params("arbitrary"),
        name="bias",
    )(rel_bias)


def _attn_kernel(q_ref, k_ref, vt_ref, bias_ref, lam_ref, sg_ref, o_ref, m_sc, l_sc, acc_sc, *, lambda_init):
    t = ATTN_TILE
    i = pl.program_id(2)
    qt = q_ref[0, 0, 0]
    m_sc[...] = jnp.full_like(m_sc, MASK_VALUE)
    l_sc[...] = jnp.zeros_like(l_sc)
    acc_sc[...] = jnp.zeros_like(acc_sc)

    def step(j, bias):
        kj = k_ref[0, 0, pl.ds(pl.multiple_of(j * t, t), t), :]
        s = _dot(kj, qt)
        if bias is not None:
            s = jnp.concatenate([s[:, :t] + bias, s[:, t:] + bias], axis=1)
        m_prev = m_sc[...]
        m_new = jnp.maximum(m_prev, jnp.max(s, axis=0, keepdims=True))
        alpha = jnp.exp(m_prev - m_new)
        p = jnp.exp(s - m_new)
        l_sc[...] = alpha * l_sc[...] + jnp.sum(p, axis=0, keepdims=True)
        acc_sc[...] = alpha * acc_sc[...] + _dot(vt_ref[0, 0, j], p.astype(_BF16))
        m_sc[...] = m_new

    def far_step(j, carry):
        step(j, None)
        return carry

    lax.fori_loop(0, jnp.maximum(i - 1, 0), far_step, 0)

    @pl.when(i >= 1)
    def _():
        step(i - 1, bias_ref[0, 1])

    step(i, bias_ref[0, 0])

    lv = lam_ref[...]
    lam = (jnp.exp(jnp.sum(lv[0:1] * lv[1:2], axis=-1, keepdims=True))
           - jnp.exp(jnp.sum(lv[2:3] * lv[3:4], axis=-1, keepdims=True)) + lambda_init)
    o = acc_sc[...] / l_sc[...]
    o = o[:, :t] - lam * o[:, t:]
    o = o * lax.rsqrt(jnp.mean(o * o, axis=0, keepdims=True) + EPS) * sg_ref[...] * (1.0 - lambda_init)
    o_ref[0] = o.T.astype(o_ref.dtype)


def _attention(qt, k, vt, bias, lam, sg, lambda_init):
    bsz, n_heads, seq, dv = k.shape
    t = ATTN_TILE
    nt = seq // t
    return pl.pallas_call(
        functools.partial(_attn_kernel, lambda_init=lambda_init),
        grid=(bsz, n_heads, nt),
        in_specs=[
            pl.BlockSpec((1, 1, 1, dv, 2 * t), lambda b, h, i: (b, h, i, 0, 0)),
            pl.BlockSpec((1, 1, seq, dv), lambda b, h, i: (b, h, 0, 0)),
            pl.BlockSpec((1, 1, nt, dv, t), lambda b, h, i: (b, h, 0, 0, 0)),
            pl.BlockSpec((1, 2, t, t), lambda b, h, i: (h, 0, 0, 0)),
            pl.BlockSpec((4, HEAD_DIM), lambda b, h, i: (0, 0)),
            pl.BlockSpec((dv, 1), lambda b, h, i: (0, 0)),
        ],
        out_specs=pl.BlockSpec((1, t, dv), lambda b, h, i: (b, i, h)),
        out_shape=jax.ShapeDtypeStruct((bsz, seq, n_heads * dv), _BF16),
        scratch_shapes=[pltpu.VMEM((1, 2 * t), _F32), pltpu.VMEM((1, 2 * t), _F32),
                        pltpu.VMEM((dv, 2 * t), _F32)],
        compiler_params=_params("parallel", "parallel", "arbitrary"),
        name="attn",
    )(qt, k, vt, bias, lam, sg)


def _o_kernel(x_ref, a_ref, wo_ref, g_ref, gate_ref, o_ref):
    y = _dot(a_ref[0], wo_ref[...])
    o_ref[0] = x_ref[0] + gate_ref[0] * (_rms(y) * g_ref[...])


def _oproj(x, a, wo, g, gate):
    bsz, seq, d = x.shape
    tm = ROW_TILE
    tile = pl.BlockSpec((1, tm, d), lambda b, i: (b, i, 0))
    return pl.pallas_call(
        _o_kernel,
        grid=(bsz, seq // tm),
        in_specs=[tile, tile, _const_spec((d, d)), _const_spec((1, d)),
                  pl.BlockSpec((1, 1, d), lambda b, i: (b, 0, 0))],
        out_specs=tile,
        out_shape=jax.ShapeDtypeStruct((bsz, seq, d), _F32),
        compiler_params=_params("parallel", "parallel"),
        name="oproj",
    )(x, a, wo, g, gate)


def kernel(x, c, mod_w, mod_b, norm_g, cm_w1, cm_b1, cm_dw, cm_dwb, cm_ln_g, cm_ln_b, cm_w2, cm_b2, kv_norm_g,
           w_k, w_v, w_q, lam, subln_g, w_o, rel_bias, ffn_w_in, ffn_dw, ffn_dwb, ffn_w_out):
    depth = mod_w.shape[0]
    n_conv = cm_w1.shape[0]
    bsz, seq, d = x.shape

    mod = _modulation(c, mod_w, mod_b)

    def row(v):
        return v.reshape(1, -1)

    kk = vt = bias = None
    for l in range(depth):
        sh_m, sc_m, g_m, sh_f, sc_f, g_f = [m.reshape(bsz, 1, d) for m in jnp.split(mod[l], 6, axis=-1)]
        g0, g1, g2, g3 = [row(norm_g[l, k]) for k in range(4)]
        if l < n_conv:
            u = _conv_a(x, g0, sc_m, sh_m, cm_w1[l].astype(_BF16), row(cm_b1[l]))
            x = _conv_b(x, u, cm_dw[l], row(cm_dwb[l]), row(cm_ln_g[l]), row(cm_ln_b[l]),
                        cm_w2[l].astype(_BF16), row(cm_b2[l]), g1, g_m)
        else:
            if l == n_conv:
                kk, vt = _kv(x, row(kv_norm_g), w_k.astype(_BF16), w_v.T.astype(_BF16))
                bias = _bias_tiles(rel_bias)
            j = l - n_conv
            lambda_init = 0.8 - 0.6 * math.exp(-0.3 * l)
            qt = _qproj(x, g0, sc_m, sh_m, w_q[j].T.astype(_BF16))
            a = _attention(qt, kk, vt, bias, lam[j], subln_g[j].reshape(-1, 1), lambda_init)
            x = _oproj(x, a, w_o[j].astype(_BF16), g1, g_m)
        x = _ffn(x, g2, sc_f, sh_f, ffn_w_in[l].astype(_BF16), ffn_dw[l], row(ffn_dwb[l]),
                 ffn_w_out[l].astype(_BF16), g3, g_f)
    return x
```

```python
import functools
import math

import jax
import jax.numpy as jnp
from jax import lax
from jax.experimental import pallas as pl
from jax.experimental.pallas import tpu as pltpu

N_HEADS = 8
HEAD_DIM = 64
V_HEAD_DIM = 128
CONV_WIDTH = 31
FFN_CONV_WIDTH = 3
N_BUCKETS = 32
MAX_DISTANCE = 128
MAX_EXACT = 16
EPS = 1e-6

ROW_TILE = 512
ATTN_TILE = 256
ATTN_HEADS_PER_STEP = 4
ONES_ROWS = 16
V_ROWS = V_HEAD_DIM + ONES_ROWS
LOG2E = math.log2(math.e)
CONV_HALO = 32
CONV_ROWS = 64
CONV_LANES = 128
FFN_CHUNK = 256
FFN_HALO = 8
MASK_VALUE = -1e30
VMEM_LIMIT = 56 * 1024 * 1024

_BF16 = jnp.bfloat16
_F32 = jnp.float32


def _bucket_of_distance(n):
    if n < MAX_EXACT:
        return n
    large = MAX_EXACT + int(math.log(n / MAX_EXACT) / math.log(MAX_DISTANCE / MAX_EXACT) * (N_BUCKETS - MAX_EXACT))
    return min(large, N_BUCKETS - 1)


def _bucket_segments():
    segs = []
    for n in range(MAX_DISTANCE):
        b = _bucket_of_distance(n)
        if not segs or segs[-1][1] != b:
            segs.append((n, b))
    assert all(_bucket_of_distance(n) == N_BUCKETS - 1 for n in range(segs[-1][0], 4 * MAX_DISTANCE))
    return segs


def _rms(x):
    return x * lax.rsqrt(jnp.mean(x * x, axis=-1, keepdims=True) + EPS)


def _dot(a, b):
    return jnp.dot(a, b, preferred_element_type=_F32)


def _params(*sem):
    return pltpu.CompilerParams(dimension_semantics=sem, vmem_limit_bytes=VMEM_LIMIT)


def _const_spec(shape):
    nd = len(shape)
    return pl.BlockSpec(shape, lambda *_: (0,) * nd)


def _mod_kernel(c_ref, w_ref, b_ref, o_ref):
    c = c_ref[...]
    c_act = c * jax.nn.sigmoid(c)
    o_ref[0] = _dot(c_act, w_ref[0]) + b_ref[0]


def _modulation(c, mod_w, mod_b):
    depth, d, n = mod_w.shape
    bsz = c.shape[0]
    tn = 1024
    return pl.pallas_call(
        _mod_kernel,
        grid=(depth, n // tn),
        in_specs=[
            pl.BlockSpec((bsz, d), lambda l, j: (0, 0)),
            pl.BlockSpec((1, d, tn), lambda l, j: (l, 0, j)),
            pl.BlockSpec((1, 1, tn), lambda l, j: (l, 0, j)),
        ],
        out_specs=pl.BlockSpec((1, bsz, tn), lambda l, j: (l, 0, j)),
        out_shape=jax.ShapeDtypeStruct((depth, bsz, n), _F32),
        compiler_params=_params("arbitrary", "arbitrary"),
        name="mod",
    )(c, mod_w, mod_b.reshape(depth, 1, n))


def _conv_a_kernel(x_ref, g_ref, sc_ref, sh_ref, w1_ref, b1_ref, u_ref):
    d = x_ref.shape[-1]
    h = _rms(x_ref[0]) * g_ref[...] * (1.0 + sc_ref[0]) + sh_ref[0]
    hb = h.astype(_BF16)
    a = _dot(hb, w1_ref[:, :d]) + b1_ref[:, :d]
    gt = _dot(hb, w1_ref[:, d:]) + b1_ref[:, d:]
    u_ref[0] = a * jax.nn.sigmoid(gt)


def _conv_a(x, g, sc, sh, w1, b1):
    bsz, seq, d = x.shape
    tm = ROW_TILE
    vec = pl.BlockSpec((1, 1, d), lambda b, i: (b, 0, 0))
    return pl.pallas_call(
        _conv_a_kernel,
        grid=(bsz, seq // tm),
        in_specs=[
            pl.BlockSpec((1, tm, d), lambda b, i: (b, i, 0)),
            _const_spec((1, d)), vec, vec,
            _const_spec((d, 2 * d)), _const_spec((1, 2 * d)),
        ],
        out_specs=pl.BlockSpec((1, tm, d), lambda b, i: (b, i, 0)),
        out_shape=jax.ShapeDtypeStruct((bsz, seq, d), _F32),
        compiler_params=_params("parallel", "parallel"),
        name="conv_a",
    )(x, g, sc, sh, w1, b1)


def _conv_b_kernel(x_ref, ucur_ref, uprev_ref, dw_ref, dwb_ref, lng_ref, lnb_ref, w2_ref, b2_ref,
                   g_ref, gate_ref, o_ref, ubuf, cbuf):
    tm, d = ucur_ref.shape[1], ucur_ref.shape[2]
    i = pl.program_id(1)
    prev = uprev_ref[0]
    ubuf[0:CONV_HALO, :] = jnp.where(i > 0, prev, jnp.zeros_like(prev))
    ubuf[CONV_HALO:, :] = ucur_ref[0]

    rows = CONV_ROWS + CONV_HALO
    first_shift = CONV_HALO - (CONV_WIDTH - 1)

    def row_block(r, carry):
        r0 = pl.multiple_of(r * CONV_ROWS, CONV_ROWS)
        for c0 in range(0, d, CONV_LANES):
            xs = ubuf[pl.ds(r0, rows), c0:c0 + CONV_LANES]
            acc = jnp.broadcast_to(dwb_ref[:, c0:c0 + CONV_LANES], (CONV_ROWS, CONV_LANES))
            for sub in range(8):
                taps = [j for j in range(CONV_WIDTH) if (first_shift + j) % 8 == sub]
                if not taps:
                    continue
                xb = xs if sub == 0 else pltpu.roll(xs, rows - sub, 0)
                for j in taps:
                    a8 = (first_shift + j) - sub
                    acc = acc + xb[a8:a8 + CONV_ROWS, :] * dw_ref[j:j + 1, c0:c0 + CONV_LANES]
            cbuf[pl.ds(r0, CONV_ROWS), c0:c0 + CONV_LANES] = acc
        return carry

    lax.fori_loop(0, tm // CONV_ROWS, row_block, 0)

    cv = cbuf[...]
    mu = jnp.mean(cv, axis=-1, keepdims=True)
    cc = cv - mu
    var = jnp.mean(cc * cc, axis=-1, keepdims=True)
    z = cc * lax.rsqrt(var + EPS) * lng_ref[...] + lnb_ref[...]
    z = z * jax.nn.sigmoid(z)
    y = _dot(z.astype(_BF16), w2_ref[...]) + b2_ref[...]
    o_ref[0] = x_ref[0] + gate_ref[0] * (_rms(y) * g_ref[...])


def _conv_b(x, u, dw, dwb, lng, lnb, w2, b2, g, gate):
    bsz, seq, d = x.shape
    tm = ROW_TILE
    ratio = tm // CONV_HALO
    vec = pl.BlockSpec((1, 1, d), lambda b, i: (b, 0, 0))
    tile = pl.BlockSpec((1, tm, d), lambda b, i: (b, i, 0))
    return pl.pallas_call(
        _conv_b_kernel,
        grid=(bsz, seq // tm),
        in_specs=[
            tile, tile,
            pl.BlockSpec((1, CONV_HALO, d), lambda b, i: (b, jnp.maximum(i * ratio - 1, 0), 0)),
            _const_spec((CONV_WIDTH, d)), _const_spec((1, d)), _const_spec((1, d)), _const_spec((1, d)),
            _const_spec((d, d)), _const_spec((1, d)), _const_spec((1, d)), vec,
        ],
        out_specs=tile,
        out_shape=jax.ShapeDtypeStruct((bsz, seq, d), _F32),
        scratch_shapes=[pltpu.VMEM((tm + CONV_HALO, d), _F32), pltpu.VMEM((tm, d), _F32)],
        compiler_params=_params("parallel", "parallel"),
        name="conv_b",
    )(x, u, u, dw, dwb, lng, lnb, w2, b2, g, gate)


def _ffn_kernel(x_ref, g_ref, sc_ref, sh_ref, win_ref, dw_ref, dwb_ref, wout_ref, g3_ref, gate_ref, o_ref,
                hbuf, ubuf, carry, pbuf):
    tm = x_ref.shape[1]
    f = wout_ref.shape[0]
    i = pl.program_id(1)

    @pl.when(i == 0)
    def _():
        carry[...] = jnp.zeros_like(carry)

    h = _rms(x_ref[0]) * g_ref[...] * (1.0 + sc_ref[0]) + sh_ref[0]
    hbuf[...] = h.astype(_BF16)

    def conv3(col0):
        cols = slice(col0, col0 + FFN_CHUNK)
        u = _dot(hbuf[...], win_ref[:, cols])
        ubuf[0:FFN_HALO, :] = carry[:, cols]
        ubuf[FFN_HALO:, :] = u
        carry[:, cols] = u[tm - FFN_HALO:, :]
        out = dwb_ref[:, cols]
        for j in range(FFN_CONV_WIDTH):
            shift = FFN_CONV_WIDTH - 1 - j
            out = out + ubuf[pl.ds(FFN_HALO - shift, tm), :] * dw_ref[j:j + 1, cols]
        return out

    for c0 in range(0, f, FFN_CHUNK):
        gpart = conv3(c0)
        vpart = conv3(f + c0)
        pbuf[:, c0:c0 + FFN_CHUNK] = (gpart * jax.nn.sigmoid(gpart) * vpart).astype(_BF16)

    y = _dot(pbuf[...], wout_ref[...])
    o_ref[0] = x_ref[0] + gate_ref[0] * (_rms(y) * g3_ref[...])


def _ffn(x, g, sc, sh, w_in, dw, dwb, w_out, g3, gate):
    bsz, seq, d = x.shape
    f = w_out.shape[0]
    tm = ROW_TILE
    vec = pl.BlockSpec((1, 1, d), lambda b, i: (b, 0, 0))
    tile = pl.BlockSpec((1, tm, d), lambda b, i: (b, i, 0))
    single = dict(pipeline_mode=pl.Buffered(1))
    return pl.pallas_call(
        _ffn_kernel,
        grid=(bsz, seq // tm),
        in_specs=[
            tile, _const_spec((1, d)), vec, vec,
            pl.BlockSpec((d, 2 * f), lambda b, i: (0, 0), **single),
            _const_spec((FFN_CONV_WIDTH, 2 * f)), _const_spec((1, 2 * f)),
            pl.BlockSpec((f, d), lambda b, i: (0, 0), **single),
            _const_spec((1, d)), vec,
        ],
        out_specs=tile,
        out_shape=jax.ShapeDtypeStruct((bsz, seq, d), _F32),
        scratch_shapes=[
            pltpu.VMEM((tm, d), _BF16),
            pltpu.VMEM((tm + FFN_HALO, FFN_CHUNK), _F32),
            pltpu.VMEM((FFN_HALO, 2 * f), _F32),
            pltpu.VMEM((tm, f), _BF16),
        ],
        compiler_params=_params("arbitrary", "arbitrary"),
        name="ffn",
    )(x, g, sc, sh, w_in, dw, dwb, w_out, g3, gate)


def _kv_kernel(x_ref, g_ref, wk_ref, wvt_ref, k_ref, vt_ref):
    tm = x_ref.shape[1]
    tk = vt_ref.shape[-1]
    hb = (_rms(x_ref[0]) * g_ref[...]).astype(_BF16)
    k = _dot(hb, wk_ref[...])
    vt = lax.dot_general(wvt_ref[...], hb, (((1,), (1,)), ((), ())), preferred_element_type=_F32)
    for hd in range(N_HEADS):
        rows = slice(hd * V_HEAD_DIM, (hd + 1) * V_HEAD_DIM)
        k_ref[0, hd] = k[:, rows].astype(_BF16)
        for c in range(tm // tk):
            vt_ref[0, hd, c, :V_HEAD_DIM, :] = vt[rows, c * tk:(c + 1) * tk].astype(_BF16)
            vt_ref[0, hd, c, V_HEAD_DIM:, :] = jnp.ones((ONES_ROWS, tk), _BF16)


def _kv(x, g, wk, wvt):
    bsz, seq, d = x.shape
    tm, tk = ROW_TILE, ATTN_TILE
    return pl.pallas_call(
        _kv_kernel,
        grid=(bsz, seq // tm),
        in_specs=[
            pl.BlockSpec((1, tm, d), lambda b, i: (b, i, 0)),
            _const_spec((1, d)), _const_spec((d, d)), _const_spec((d, d)),
        ],
        out_specs=[
            pl.BlockSpec((1, N_HEADS, tm, V_HEAD_DIM), lambda b, i: (b, 0, i, 0)),
            pl.BlockSpec((1, N_HEADS, tm // tk, V_ROWS, tk), lambda b, i: (b, 0, i, 0, 0)),
        ],
        out_shape=[
            jax.ShapeDtypeStruct((bsz, N_HEADS, seq, V_HEAD_DIM), _BF16),
            jax.ShapeDtypeStruct((bsz, N_HEADS, seq // tk, V_ROWS, tk), _BF16),
        ],
        compiler_params=_params("parallel", "parallel"),
        name="kv",
    )(x, g, wk, wvt)


def _q_kernel(x_ref, g_ref, sc_ref, sh_ref, wqt_ref, q_ref):
    tm = x_ref.shape[1]
    tq = q_ref.shape[-1] // 2
    h = _rms(x_ref[0]) * g_ref[...] * (1.0 + sc_ref[0]) + sh_ref[0]
    qt = lax.dot_general(wqt_ref[...], h.astype(_BF16), (((1,), (1,)), ((), ())),
                         preferred_element_type=_F32) * (HEAD_DIM ** -0.5 * LOG2E)
    first = lax.broadcasted_iota(jnp.int32, (V_HEAD_DIM, tq), 0) < HEAD_DIM
    for hd in range(N_HEADS):
        for c in range(tm // tq):
            qh = qt[hd * V_HEAD_DIM:(hd + 1) * V_HEAD_DIM, c * tq:(c + 1) * tq]
            q_ref[0, hd, c, :, :tq] = jnp.where(first, qh, 0.0).astype(_BF16)
            q_ref[0, hd, c, :, tq:] = jnp.where(first, 0.0, qh).astype(_BF16)


def _qproj(x, g, sc, sh, wqt):
    bsz, seq, d = x.shape
    tm, tq = ROW_TILE, ATTN_TILE
    vec = pl.BlockSpec((1, 1, d), lambda b, i: (b, 0, 0))
    return pl.pallas_call(
        _q_kernel,
        grid=(bsz, seq // tm),
        in_specs=[pl.BlockSpec((1, tm, d), lambda b, i: (b, i, 0)), _const_spec((1, d)), vec, vec,
                  _const_spec((d, d))],
        out_specs=pl.BlockSpec((1, N_HEADS, tm // tq, V_HEAD_DIM, 2 * tq), lambda b, i: (b, 0, i, 0, 0)),
        out_shape=jax.ShapeDtypeStruct((bsz, N_HEADS, seq // tq, V_HEAD_DIM, 2 * tq), _BF16),
        compiler_params=_params("parallel", "parallel"),
        name="qproj",
    )(x, g, sc, sh, wqt)


def _bias_kernel(rb_ref, o_ref):
    hd = pl.program_id(0)
    t = o_ref.shape[-1]
    key = lax.broadcasted_iota(jnp.int32, (t, t), 0)
    qry = lax.broadcasted_iota(jnp.int32, (t, t), 1)
    segs = _bucket_segments()
    far = rb_ref[N_BUCKETS - 1, hd]
    for which in range(2):
        rel = qry - key + which * t
        val = jnp.full((t, t), 0.0, _F32)
        for k in range(len(segs) - 2, -1, -1):
            val = jnp.where(rel < segs[k + 1][0], (rb_ref[segs[k][1], hd] - far) * LOG2E, val)
        o_ref[0, which] = jnp.where(rel >= 0, val, MASK_VALUE)


def _bias_tiles(rel_bias):
    t = ATTN_TILE
    return pl.pallas_call(
        _bias_kernel,
        grid=(N_HEADS,),
        in_specs=[pl.BlockSpec(memory_space=pltpu.SMEM)],
        out_specs=pl.BlockSpec((1, 2, t, t), lambda h: (h, 0, 0, 0)),
        out_shape=jax.ShapeDtypeStruct((N_HEADS, 2, t, t), _F32),
        compiler_params=_params("arbitrary"),
        name="bias",
    )(rel_bias)


def _attn_kernel(q_ref, k_ref, vt_ref, bias_ref, lam_ref, sg_ref, o_ref, m_sc, acc_sc, s_a, s_b, *, lambda_init):
    t = ATTN_TILE
    dv = V_HEAD_DIM
    heads = q_ref.shape[1]
    i = pl.program_id(2)
    m_sc[...] = jnp.full_like(m_sc, MASK_VALUE)
    acc_sc[...] = jnp.zeros_like(acc_sc)

    def scores_into(buf, j):
        rows = pl.ds(pl.multiple_of(j * t, t), t)
        for hh in range(heads):
            buf[hh] = _dot(k_ref[0, hh, rows, :], q_ref[0, hh, 0])

    def step(j, which, cur, nxt):
        if nxt is not None:
            scores_into(nxt, j + 1)
        for hh in range(heads):
            s = cur[hh]
            if which is not None:
                bias = bias_ref[hh, which]
                s = jnp.concatenate([s[:, :t] + bias, s[:, t:] + bias], axis=1)
            m_prev = m_sc[hh]
            m_new = jnp.maximum(m_prev, jnp.max(s, axis=0, keepdims=True))
            alpha = jnp.exp2(m_prev - m_new)
            p = jnp.exp2(s - m_new)
            acc_sc[hh] = alpha * acc_sc[hh] + _dot(vt_ref[0, hh, j], p.astype(_BF16))
            m_sc[hh] = m_new

    def far_pair(jj, carry):
        step(2 * jj, None, s_a, s_b)
        step(2 * jj + 1, None, s_b, s_a)
        return carry

    scores_into(s_a, 0)
    lax.fori_loop(0, jnp.maximum((i - 1) // 2, 0), far_pair, 0)

    @pl.when(i % 2 == 0)
    def _():
        @pl.when(i >= 2)
        def _():
            step(i - 2, None, s_a, s_b)
            step(i - 1, 1, s_b, s_a)

        step(i, 0, s_a, None)

    @pl.when(i % 2 == 1)
    def _():
        step(i - 1, 1, s_a, s_b)
        step(i, 0, s_b, None)

    lv = lam_ref[...]
    lam = (jnp.exp(jnp.sum(lv[0:1] * lv[1:2], axis=-1, keepdims=True))
           - jnp.exp(jnp.sum(lv[2:3] * lv[3:4], axis=-1, keepdims=True)) + lambda_init)
    for hh in range(heads):
        acc = acc_sc[hh]
        o = acc[:dv] / acc[dv:dv + 1]
        o = o[:, :t] - lam * o[:, t:]
        o = o * lax.rsqrt(jnp.mean(o * o, axis=0, keepdims=True) + EPS) * sg_ref[...] * (1.0 - lambda_init)
        o_ref[0, :, hh * dv:(hh + 1) * dv] = o.T.astype(o_ref.dtype)


def _attention(qt, k, vt, bias, lam, sg, lambda_init):
    bsz, n_heads, seq, dv = k.shape
    t = ATTN_TILE
    nt = seq // t
    hp = ATTN_HEADS_PER_STEP
    return pl.pallas_call(
        functools.partial(_attn_kernel, lambda_init=lambda_init),
        grid=(bsz, n_heads // hp, nt),
        in_specs=[
            pl.BlockSpec((1, hp, 1, dv, 2 * t), lambda b, h, i: (b, h, i, 0, 0)),
            pl.BlockSpec((1, hp, seq, dv), lambda b, h, i: (b, h, 0, 0)),
            pl.BlockSpec((1, hp, nt, V_ROWS, t), lambda b, h, i: (b, h, 0, 0, 0)),
            pl.BlockSpec((hp, 2, t, t), lambda b, h, i: (h, 0, 0, 0)),
            pl.BlockSpec((4, HEAD_DIM), lambda b, h, i: (0, 0)),
            pl.BlockSpec((dv, 1), lambda b, h, i: (0, 0)),
        ],
        out_specs=pl.BlockSpec((1, t, hp * dv), lambda b, h, i: (b, i, h)),
        out_shape=jax.ShapeDtypeStruct((bsz, seq, n_heads * dv), _BF16),
        scratch_shapes=[pltpu.VMEM((hp, 1, 2 * t), _F32), pltpu.VMEM((hp, V_ROWS, 2 * t), _F32),
                        pltpu.VMEM((hp, t, 2 * t), _F32), pltpu.VMEM((hp, t, 2 * t), _F32)],
        compiler_params=_params("parallel", "parallel", "arbitrary"),
        name="attn",
    )(qt, k, vt, bias, lam, sg)


def _o_kernel(x_ref, a_ref, wo_ref, g_ref, gate_ref, o_ref):
    y = _dot(a_ref[0], wo_ref[...])
    o_ref[0] = x_ref[0] + gate_ref[0] * (_rms(y) * g_ref[...])


def _oproj(x, a, wo, g, gate):
    bsz, seq, d = x.shape
    tm = ROW_TILE
    tile = pl.BlockSpec((1, tm, d), lambda b, i: (b, i, 0))
    return pl.pallas_call(
        _o_kernel,
        grid=(bsz, seq // tm),
        in_specs=[tile, tile, _const_spec((d, d)), _const_spec((1, d)),
                  pl.BlockSpec((1, 1, d), lambda b, i: (b, 0, 0))],
        out_specs=tile,
        out_shape=jax.ShapeDtypeStruct((bsz, seq, d), _F32),
        compiler_params=_params("parallel", "parallel"),
        name="oproj",
    )(x, a, wo, g, gate)


def kernel(x, c, mod_w, mod_b, norm_g, cm_w1, cm_b1, cm_dw, cm_dwb, cm_ln_g, cm_ln_b, cm_w2, cm_b2, kv_norm_g,
           w_k, w_v, w_q, lam, subln_g, w_o, rel_bias, ffn_w_in, ffn_dw, ffn_dwb, ffn_w_out):
    depth = mod_w.shape[0]
    n_conv = cm_w1.shape[0]
    bsz, seq, d = x.shape

    mod = _modulation(c, mod_w, mod_b)

    def row(v):
        return v.reshape(1, -1)

    kk = vt = bias = None
    for l in range(depth):
        sh_m, sc_m, g_m, sh_f, sc_f, g_f = [m.reshape(bsz, 1, d) for m in jnp.split(mod[l], 6, axis=-1)]
        g0, g1, g2, g3 = [row(norm_g[l, k]) for k in range(4)]
        if l < n_conv:
            u = _conv_a(x, g0, sc_m, sh_m, cm_w1[l].astype(_BF16), row(cm_b1[l]))
            x = _conv_b(x, u, cm_dw[l], row(cm_dwb[l]), row(cm_ln_g[l]), row(cm_ln_b[l]),
                        cm_w2[l].astype(_BF16), row(cm_b2[l]), g1, g_m)
        else:
            if l == n_conv:
                kk, vt = _kv(x, row(kv_norm_g), w_k.astype(_BF16), w_v.T.astype(_BF16))
                bias = _bias_tiles(rel_bias)
            j = l - n_conv
            lambda_init = 0.8 - 0.6 * math.exp(-0.3 * l)
            qt = _qproj(x, g0, sc_m, sh_m, w_q[j].T.astype(_BF16))
            a = _attention(qt, kk, vt, bias, lam[j], subln_g[j].reshape(-1, 1), lambda_init)
            x = _oproj(x, a, w_o[j].astype(_BF16), g1, g_m)
        x = _ffn(x, g2, sc_f, sh_f, ffn_w_in[l].astype(_BF16), ffn_dw[l], row(ffn_dwb[l]),
                 ffn_w_out[l].astype(_BF16), g3, g_f)
    return x
```

```python
import functools
import math

import jax
import jax.numpy as jnp
from jax import lax
from jax.experimental import pallas as pl
from jax.experimental.pallas import tpu as pltpu

N_HEADS = 8
HEAD_DIM = 64
V_HEAD_DIM = 128
CONV_WIDTH = 31
FFN_CONV_WIDTH = 3
N_BUCKETS = 32
MAX_DISTANCE = 128
MAX_EXACT = 16
EPS = 1e-6

ROW_TILE = 512
ATTN_TILE = 256
ATTN_HEADS_PER_STEP = 4
ONES_ROWS = 16
V_ROWS = V_HEAD_DIM + ONES_ROWS
LOG2E = math.log2(math.e)
CONV_HALO = 32
CONV_ROWS = 64
CONV_LANES = 128
FFN_CHUNK = 256
FFN_HALO = 8
MASK_VALUE = -1e30
VMEM_LIMIT = 56 * 1024 * 1024

_BF16 = jnp.bfloat16
_F32 = jnp.float32


def _bucket_of_distance(n):
    if n < MAX_EXACT:
        return n
    large = MAX_EXACT + int(math.log(n / MAX_EXACT) / math.log(MAX_DISTANCE / MAX_EXACT) * (N_BUCKETS - MAX_EXACT))
    return min(large, N_BUCKETS - 1)


def _bucket_segments():
    segs = []
    for n in range(MAX_DISTANCE):
        b = _bucket_of_distance(n)
        if not segs or segs[-1][1] != b:
            segs.append((n, b))
    assert all(_bucket_of_distance(n) == N_BUCKETS - 1 for n in range(segs[-1][0], 4 * MAX_DISTANCE))
    return segs


def _rms(x):
    return x * lax.rsqrt(jnp.mean(x * x, axis=-1, keepdims=True) + EPS)


def _dot(a, b):
    return jnp.dot(a, b, preferred_element_type=_F32)


def _params(*sem):
    return pltpu.CompilerParams(dimension_semantics=sem, vmem_limit_bytes=VMEM_LIMIT)


def _const_spec(shape):
    nd = len(shape)
    return pl.BlockSpec(shape, lambda *_: (0,) * nd)


def _mod_kernel(c_ref, w_ref, b_ref, o_ref):
    c = c_ref[...]
    c_act = c * jax.nn.sigmoid(c)
    o_ref[0] = _dot(c_act, w_ref[0]) + b_ref[0]


def _modulation(c, mod_w, mod_b):
    depth, d, n = mod_w.shape
    bsz = c.shape[0]
    tn = 1024
    return pl.pallas_call(
        _mod_kernel,
        grid=(depth, n // tn),
        in_specs=[
            pl.BlockSpec((bsz, d), lambda l, j: (0, 0)),
            pl.BlockSpec((1, d, tn), lambda l, j: (l, 0, j)),
            pl.BlockSpec((1, 1, tn), lambda l, j: (l, 0, j)),
        ],
        out_specs=pl.BlockSpec((1, bsz, tn), lambda l, j: (l, 0, j)),
        out_shape=jax.ShapeDtypeStruct((depth, bsz, n), _F32),
        compiler_params=_params("arbitrary", "arbitrary"),
        name="mod",
    )(c, mod_w, mod_b.reshape(depth, 1, n))


def _conv_a_kernel(x_ref, g_ref, sc_ref, sh_ref, w1_ref, b1_ref, u_ref):
    d = x_ref.shape[-1]
    h = _rms(x_ref[0]) * g_ref[...] * (1.0 + sc_ref[0]) + sh_ref[0]
    hb = h.astype(_BF16)
    a = _dot(hb, w1_ref[:, :d]) + b1_ref[:, :d]
    gt = _dot(hb, w1_ref[:, d:]) + b1_ref[:, d:]
    u_ref[0] = a * jax.nn.sigmoid(gt)


def _conv_a(x, g, sc, sh, w1, b1):
    bsz, seq, d = x.shape
    tm = ROW_TILE
    vec = pl.BlockSpec((1, 1, d), lambda b, i: (b, 0, 0))
    return pl.pallas_call(
        _conv_a_kernel,
        grid=(bsz, seq // tm),
        in_specs=[
            pl.BlockSpec((1, tm, d), lambda b, i: (b, i, 0)),
            _const_spec((1, d)), vec, vec,
            _const_spec((d, 2 * d)), _const_spec((1, 2 * d)),
        ],
        out_specs=pl.BlockSpec((1, tm, d), lambda b, i: (b, i, 0)),
        out_shape=jax.ShapeDtypeStruct((bsz, seq, d), _F32),
        compiler_params=_params("parallel", "parallel"),
        name="conv_a",
    )(x, g, sc, sh, w1, b1)


def _conv_b_kernel(x_ref, ucur_ref, uprev_ref, dw_ref, dwb_ref, lng_ref, lnb_ref, w2_ref, b2_ref,
                   g_ref, gate_ref, o_ref, ubuf, cbuf):
    tm, d = ucur_ref.shape[1], ucur_ref.shape[2]
    i = pl.program_id(1)
    prev = uprev_ref[0]
    ubuf[0:CONV_HALO, :] = jnp.where(i > 0, prev, jnp.zeros_like(prev))
    ubuf[CONV_HALO:, :] = ucur_ref[0]

    rows = CONV_ROWS + CONV_HALO
    first_shift = CONV_HALO - (CONV_WIDTH - 1)

    def row_block(r, carry):
        r0 = pl.multiple_of(r * CONV_ROWS, CONV_ROWS)
        for c0 in range(0, d, CONV_LANES):
            xs = ubuf[pl.ds(r0, rows), c0:c0 + CONV_LANES]
            acc = jnp.broadcast_to(dwb_ref[:, c0:c0 + CONV_LANES], (CONV_ROWS, CONV_LANES))
            for sub in range(8):
                taps = [j for j in range(CONV_WIDTH) if (first_shift + j) % 8 == sub]
                if not taps:
                    continue
                xb = xs if sub == 0 else pltpu.roll(xs, rows - sub, 0)
                for j in taps:
                    a8 = (first_shift + j) - sub
                    acc = acc + xb[a8:a8 + CONV_ROWS, :] * dw_ref[j:j + 1, c0:c0 + CONV_LANES]
            cbuf[pl.ds(r0, CONV_ROWS), c0:c0 + CONV_LANES] = acc
        return carry

    lax.fori_loop(0, tm // CONV_ROWS, row_block, 0)

    cv = cbuf[...]
    mu = jnp.mean(cv, axis=-1, keepdims=True)
    cc = cv - mu
    var = jnp.mean(cc * cc, axis=-1, keepdims=True)
    z = cc * lax.rsqrt(var + EPS) * lng_ref[...] + lnb_ref[...]
    z = z * jax.nn.sigmoid(z)
    y = _dot(z.astype(_BF16), w2_ref[...]) + b2_ref[...]
    o_ref[0] = x_ref[0] + gate_ref[0] * (_rms(y) * g_ref[...])


def _conv_b(x, u, dw, dwb, lng, lnb, w2, b2, g, gate):
    bsz, seq, d = x.shape
    tm = ROW_TILE
    ratio = tm // CONV_HALO
    vec = pl.BlockSpec((1, 1, d), lambda b, i: (b, 0, 0))
    tile = pl.BlockSpec((1, tm, d), lambda b, i: (b, i, 0))
    return pl.pallas_call(
        _conv_b_kernel,
        grid=(bsz, seq // tm),
        in_specs=[
            tile, tile,
            pl.BlockSpec((1, CONV_HALO, d), lambda b, i: (b, jnp.maximum(i * ratio - 1, 0), 0)),
            _const_spec((CONV_WIDTH, d)), _const_spec((1, d)), _const_spec((1, d)), _const_spec((1, d)),
            _const_spec((d, d)), _const_spec((1, d)), _const_spec((1, d)), vec,
        ],
        out_specs=tile,
        out_shape=jax.ShapeDtypeStruct((bsz, seq, d), _F32),
        scratch_shapes=[pltpu.VMEM((tm + CONV_HALO, d), _F32), pltpu.VMEM((tm, d), _F32)],
        compiler_params=_params("parallel", "parallel"),
        name="conv_b",
    )(x, u, u, dw, dwb, lng, lnb, w2, b2, g, gate)


def _ffn_kernel(x_ref, g_ref, sc_ref, sh_ref, win_ref, dw_ref, dwb_ref, wout_ref, g3_ref, gate_ref, o_ref,
                hbuf, ubuf, carry, pbuf):
    tm = x_ref.shape[1]
    f = wout_ref.shape[0]
    i = pl.program_id(1)

    @pl.when(i == 0)
    def _():
        carry[...] = jnp.zeros_like(carry)

    h = _rms(x_ref[0]) * g_ref[...] * (1.0 + sc_ref[0]) + sh_ref[0]
    hbuf[...] = h.astype(_BF16)

    def conv3(col0):
        cols = slice(col0, col0 + FFN_CHUNK)
        u = _dot(hbuf[...], win_ref[:, cols])
        ubuf[0:FFN_HALO, :] = carry[:, cols]
        ubuf[FFN_HALO:, :] = u
        carry[:, cols] = u[tm - FFN_HALO:, :]
        out = dwb_ref[:, cols]
        for j in range(FFN_CONV_WIDTH):
            shift = FFN_CONV_WIDTH - 1 - j
            out = out + ubuf[pl.ds(FFN_HALO - shift, tm), :] * dw_ref[j:j + 1, cols]
        return out

    for c0 in range(0, f, FFN_CHUNK):
        gpart = conv3(c0)
        vpart = conv3(f + c0)
        pbuf[:, c0:c0 + FFN_CHUNK] = (gpart * jax.nn.sigmoid(gpart) * vpart).astype(_BF16)

    y = _dot(pbuf[...], wout_ref[...])
    o_ref[0] = x_ref[0] + gate_ref[0] * (_rms(y) * g3_ref[...])


def _ffn(x, g, sc, sh, w_in, dw, dwb, w_out, g3, gate):
    bsz, seq, d = x.shape
    f = w_out.shape[0]
    tm = ROW_TILE
    vec = pl.BlockSpec((1, 1, d), lambda b, i: (b, 0, 0))
    tile = pl.BlockSpec((1, tm, d), lambda b, i: (b, i, 0))
    single = dict(pipeline_mode=pl.Buffered(1))
    return pl.pallas_call(
        _ffn_kernel,
        grid=(bsz, seq // tm),
        in_specs=[
            tile, _const_spec((1, d)), vec, vec,
            pl.BlockSpec((d, 2 * f), lambda b, i: (0, 0), **single),
            _const_spec((FFN_CONV_WIDTH, 2 * f)), _const_spec((1, 2 * f)),
            pl.BlockSpec((f, d), lambda b, i: (0, 0), **single),
            _const_spec((1, d)), vec,
        ],
        out_specs=tile,
        out_shape=jax.ShapeDtypeStruct((bsz, seq, d), _F32),
        scratch_shapes=[
            pltpu.VMEM((tm, d), _BF16),
            pltpu.VMEM((tm + FFN_HALO, FFN_CHUNK), _F32),
            pltpu.VMEM((FFN_HALO, 2 * f), _F32),
            pltpu.VMEM((tm, f), _BF16),
        ],
        compiler_params=_params("arbitrary", "arbitrary"),
        name="ffn",
    )(x, g, sc, sh, w_in, dw, dwb, w_out, g3, gate)


def _kv_kernel(x_ref, g_ref, wk_ref, wvt_ref, k_ref, vt_ref):
    tm = x_ref.shape[1]
    tk = vt_ref.shape[-1]
    hb = (_rms(x_ref[0]) * g_ref[...]).astype(_BF16)
    k = _dot(hb, wk_ref[...])
    vt = lax.dot_general(wvt_ref[...], hb, (((1,), (1,)), ((), ())), preferred_element_type=_F32)
    for hd in range(N_HEADS):
        rows = slice(hd * V_HEAD_DIM, (hd + 1) * V_HEAD_DIM)
        k_ref[0, hd] = k[:, rows].astype(_BF16)
        for c in range(tm // tk):
            vt_ref[0, hd, c, :V_HEAD_DIM, :] = vt[rows, c * tk:(c + 1) * tk].astype(_BF16)
            vt_ref[0, hd, c, V_HEAD_DIM:, :] = jnp.ones((ONES_ROWS, tk), _BF16)


def _kv(x, g, wk, wvt):
    bsz, seq, d = x.shape
    tm, tk = ROW_TILE, ATTN_TILE
    return pl.pallas_call(
        _kv_kernel,
        grid=(bsz, seq // tm),
        in_specs=[
            pl.BlockSpec((1, tm, d), lambda b, i: (b, i, 0)),
            _const_spec((1, d)), _const_spec((d, d)), _const_spec((d, d)),
        ],
        out_specs=[
            pl.BlockSpec((1, N_HEADS, tm, V_HEAD_DIM), lambda b, i: (b, 0, i, 0)),
            pl.BlockSpec((1, N_HEADS, tm // tk, V_ROWS, tk), lambda b, i: (b, 0, i, 0, 0)),
        ],
        out_shape=[
            jax.ShapeDtypeStruct((bsz, N_HEADS, seq, V_HEAD_DIM), _BF16),
            jax.ShapeDtypeStruct((bsz, N_HEADS, seq // tk, V_ROWS, tk), _BF16),
        ],
        compiler_params=_params("parallel", "parallel"),
        name="kv",
    )(x, g, wk, wvt)


def _q_kernel(x_ref, g_ref, sc_ref, sh_ref, wqt_ref, q_ref):
    tm = x_ref.shape[1]
    tq = q_ref.shape[-1] // 2
    h = _rms(x_ref[0]) * g_ref[...] * (1.0 + sc_ref[0]) + sh_ref[0]
    qt = lax.dot_general(wqt_ref[...], h.astype(_BF16), (((1,), (1,)), ((), ())),
                         preferred_element_type=_F32) * (HEAD_DIM ** -0.5 * LOG2E)
    first = lax.broadcasted_iota(jnp.int32, (V_HEAD_DIM, tq), 0) < HEAD_DIM
    for hd in range(N_HEADS):
        for c in range(tm // tq):
            qh = qt[hd * V_HEAD_DIM:(hd + 1) * V_HEAD_DIM, c * tq:(c + 1) * tq]
            q_ref[0, hd, c, :, :tq] = jnp.where(first, qh, 0.0).astype(_BF16)
            q_ref[0, hd, c, :, tq:] = jnp.where(first, 0.0, qh).astype(_BF16)


def _qproj(x, g, sc, sh, wqt):
    bsz, seq, d = x.shape
    tm, tq = ROW_TILE, ATTN_TILE
    vec = pl.BlockSpec((1, 1, d), lambda b, i: (b, 0, 0))
    return pl.pallas_call(
        _q_kernel,
        grid=(bsz, seq // tm),
        in_specs=[pl.BlockSpec((1, tm, d), lambda b, i: (b, i, 0)), _const_spec((1, d)), vec, vec,
                  _const_spec((d, d))],
        out_specs=pl.BlockSpec((1, N_HEADS, tm // tq, V_HEAD_DIM, 2 * tq), lambda b, i: (b, 0, i, 0, 0)),
        out_shape=jax.ShapeDtypeStruct((bsz, N_HEADS, seq // tq, V_HEAD_DIM, 2 * tq), _BF16),
        compiler_params=_params("parallel", "parallel"),
        name="qproj",
    )(x, g, sc, sh, wqt)


def _bias_kernel(rb_ref, o_ref):
    hd = pl.program_id(0)
    t = o_ref.shape[-1]
    key = lax.broadcasted_iota(jnp.int32, (t, t), 0)
    qry = lax.broadcasted_iota(jnp.int32, (t, t), 1)
    segs = _bucket_segments()
    far = rb_ref[N_BUCKETS - 1, hd]
    for which in range(2):
        rel = qry - key + which * t
        val = jnp.full((t, t), 0.0, _F32)
        for k in range(len(segs) - 2, -1, -1):
            val = jnp.where(rel < segs[k + 1][0], (rb_ref[segs[k][1], hd] - far) * LOG2E, val)
        o_ref[0, which] = jnp.where(rel >= 0, val, MASK_VALUE)


def _bias_tiles(rel_bias):
    t = ATTN_TILE
    return pl.pallas_call(
        _bias_kernel,
        grid=(N_HEADS,),
        in_specs=[pl.BlockSpec(memory_space=pltpu.SMEM)],
        out_specs=pl.BlockSpec((1, 2, t, t), lambda h: (h, 0, 0, 0)),
        out_shape=jax.ShapeDtypeStruct((N_HEADS, 2, t, t), _F32),
        compiler_params=_params("arbitrary"),
        name="bias",
    )(rel_bias)


def _attn_kernel(q_ref, k_ref, vt_ref, bias_ref, lam_ref, sg_ref, o_ref, m_sc, acc_sc, s_a, s_b, x_a, x_b,
                 *, lambda_init):
    t = ATTN_TILE
    dv = V_HEAD_DIM
    heads = q_ref.shape[1]
    i = pl.program_id(2)
    m_sc[...] = jnp.full_like(m_sc, MASK_VALUE)
    acc_sc[...] = jnp.zeros_like(acc_sc)

    def scores_into(buf, j, hh):
        s_buf, x_buf = buf
        rows = pl.ds(pl.multiple_of(j * t, t), t)
        s = _dot(k_ref[0, hh, rows, :], q_ref[0, hh, 0])
        s_buf[hh] = s
        x_buf[hh] = jnp.max(s, axis=0, keepdims=True)

    def step(j, which, cur, nxt):
        if nxt is not None:
            scores_into(nxt, j + 1, 0)
        for hh in range(heads):
            if nxt is not None and hh + 1 < heads:
                scores_into(nxt, j + 1, hh + 1)
            s = cur[0][hh]
            if which is None:
                m_cur = cur[1][hh]
            else:
                bias = bias_ref[hh, which]
                s = jnp.concatenate([s[:, :t] + bias, s[:, t:] + bias], axis=1)
                m_cur = jnp.max(s, axis=0, keepdims=True)
            m_prev = m_sc[hh]
            m_new = jnp.maximum(m_prev, m_cur)
            alpha = jnp.exp2(m_prev - m_new)
            p = jnp.exp2(s - m_new)
            acc_sc[hh] = alpha * acc_sc[hh] + _dot(vt_ref[0, hh, j], p.astype(_BF16))
            m_sc[hh] = m_new

    buf_a, buf_b = (s_a, x_a), (s_b, x_b)

    def far_pair(jj, carry):
        step(2 * jj, None, buf_a, buf_b)
        step(2 * jj + 1, None, buf_b, buf_a)
        return carry

    for hh in range(heads):
        scores_into(buf_a, 0, hh)
    lax.fori_loop(0, jnp.maximum((i - 1) // 2, 0), far_pair, 0)

    @pl.when(i % 2 == 0)
    def _():
        @pl.when(i >= 2)
        def _():
            step(i - 2, None, buf_a, buf_b)
            step(i - 1, 1, buf_b, buf_a)

        step(i, 0, buf_a, None)

    @pl.when(i % 2 == 1)
    def _():
        step(i - 1, 1, buf_a, buf_b)
        step(i, 0, buf_b, None)

    lv = lam_ref[...]
    lam = (jnp.exp(jnp.sum(lv[0:1] * lv[1:2], axis=-1, keepdims=True))
           - jnp.exp(jnp.sum(lv[2:3] * lv[3:4], axis=-1, keepdims=True)) + lambda_init)
    for hh in range(heads):
        acc = acc_sc[hh]
        o = acc[:dv] / acc[dv:dv + 1]
        o = o[:, :t] - lam * o[:, t:]
        o = o * lax.rsqrt(jnp.mean(o * o, axis=0, keepdims=True) + EPS) * sg_ref[...] * (1.0 - lambda_init)
        o_ref[0, :, hh * dv:(hh + 1) * dv] = o.T.astype(o_ref.dtype)


def _attention(qt, k, vt, bias, lam, sg, lambda_init):
    bsz, n_heads, seq, dv = k.shape
    t = ATTN_TILE
    nt = seq // t
    hp = ATTN_HEADS_PER_STEP
    return pl.pallas_call(
        functools.partial(_attn_kernel, lambda_init=lambda_init),
        grid=(bsz, n_heads // hp, nt),
        in_specs=[
            pl.BlockSpec((1, hp, 1, dv, 2 * t), lambda b, h, i: (b, h, i, 0, 0)),
            pl.BlockSpec((1, hp, seq, dv), lambda b, h, i: (b, h, 0, 0)),
            pl.BlockSpec((1, hp, nt, V_ROWS, t), lambda b, h, i: (b, h, 0, 0, 0)),
            pl.BlockSpec((hp, 2, t, t), lambda b, h, i: (h, 0, 0, 0)),
            pl.BlockSpec((4, HEAD_DIM), lambda b, h, i: (0, 0)),
            pl.BlockSpec((dv, 1), lambda b, h, i: (0, 0)),
        ],
        out_specs=pl.BlockSpec((1, t, hp * dv), lambda b, h, i: (b, i, h)),
        out_shape=jax.ShapeDtypeStruct((bsz, seq, n_heads * dv), _BF16),
        scratch_shapes=[pltpu.VMEM((hp, 1, 2 * t), _F32), pltpu.VMEM((hp, V_ROWS, 2 * t), _F32),
                        pltpu.VMEM((hp, t, 2 * t), _F32), pltpu.VMEM((hp, t, 2 * t), _F32),
                        pltpu.VMEM((hp, 1, 2 * t), _F32), pltpu.VMEM((hp, 1, 2 * t), _F32)],
        compiler_params=_params("parallel", "parallel", "arbitrary"),
        name="attn",
    )(qt, k, vt, bias, lam, sg)


def _o_kernel(x_ref, a_ref, wo_ref, g_ref, gate_ref, o_ref):
    y = _dot(a_ref[0], wo_ref[...])
    o_ref[0] = x_ref[0] + gate_ref[0] * (_rms(y) * g_ref[...])


def _oproj(x, a, wo, g, gate):
    bsz, seq, d = x.shape
    tm = ROW_TILE
    tile = pl.BlockSpec((1, tm, d), lambda b, i: (b, i, 0))
    return pl.pallas_call(
        _o_kernel,
        grid=(bsz, seq // tm),
        in_specs=[tile, tile, _const_spec((d, d)), _const_spec((1, d)),
                  pl.BlockSpec((1, 1, d), lambda b, i: (b, 0, 0))],
        out_specs=tile,
        out_shape=jax.ShapeDtypeStruct((bsz, seq, d), _F32),
        compiler_params=_params("parallel", "parallel"),
        name="oproj",
    )(x, a, wo, g, gate)


def kernel(x, c, mod_w, mod_b, norm_g, cm_w1, cm_b1, cm_dw, cm_dwb, cm_ln_g, cm_ln_b, cm_w2, cm_b2, kv_norm_g,
           w_k, w_v, w_q, lam, subln_g, w_o, rel_bias, ffn_w_in, ffn_dw, ffn_dwb, ffn_w_out):
    depth = mod_w.shape[0]
    n_conv = cm_w1.shape[0]
    bsz, seq, d = x.shape

    mod = _modulation(c, mod_w, mod_b)

    def row(v):
        return v.reshape(1, -1)

    kk = vt = bias = None
    for l in range(depth):
        sh_m, sc_m, g_m, sh_f, sc_f, g_f = [m.reshape(bsz, 1, d) for m in jnp.split(mod[l], 6, axis=-1)]
        g0, g1, g2, g3 = [row(norm_g[l, k]) for k in range(4)]
        if l < n_conv:
            u = _conv_a(x, g0, sc_m, sh_m, cm_w1[l].astype(_BF16), row(cm_b1[l]))
            x = _conv_b(x, u, cm_dw[l], row(cm_dwb[l]), row(cm_ln_g[l]), row(cm_ln_b[l]),
                        cm_w2[l].astype(_BF16), row(cm_b2[l]), g1, g_m)
        else:
            if l == n_conv:
                kk, vt = _kv(x, row(kv_norm_g), w_k.astype(_BF16), w_v.T.astype(_BF16))
                bias = _bias_tiles(rel_bias)
            j = l - n_conv
            lambda_init = 0.8 - 0.6 * math.exp(-0.3 * l)
            qt = _qproj(x, g0, sc_m, sh_m, w_q[j].T.astype(_BF16))
            a = _attention(qt, kk, vt, bias, lam[j], subln_g[j].reshape(-1, 1), lambda_init)
            x = _oproj(x, a, w_o[j].astype(_BF16), g1, g_m)
        x = _ffn(x, g2, sc_f, sh_f, ffn_w_in[l].astype(_BF16), ffn_dw[l], row(ffn_dwb[l]),
                 ffn_w_out[l].astype(_BF16), g3, g_f)
    return x
```

```python
import functools
import math

import jax
import jax.numpy as jnp
from jax import lax
from jax.experimental import pallas as pl
from jax.experimental.pallas import tpu as pltpu

N_HEADS = 8
HEAD_DIM = 64
V_HEAD_DIM = 128
CONV_WIDTH = 31
FFN_CONV_WIDTH = 3
N_BUCKETS = 32
MAX_DISTANCE = 128
MAX_EXACT = 16
EPS = 1e-6

ROW_TILE = 512
ATTN_TILE = 256
ATTN_HEADS_PER_STEP = 4
ONES_ROWS = 16
V_ROWS = V_HEAD_DIM + ONES_ROWS
LOG2E = math.log2(math.e)
CONV_HALO = 32
CONV_ROWS = 64
LANES = 128
FFN_CHUNK = 256
FFN_HALO = 8
MASK_VALUE = -1e30
VMEM_LIMIT = 56 * 1024 * 1024

_BF16 = jnp.bfloat16
_F32 = jnp.float32


def _bucket_of_distance(n):
    if n < MAX_EXACT:
        return n
    large = MAX_EXACT + int(math.log(n / MAX_EXACT) / math.log(MAX_DISTANCE / MAX_EXACT) * (N_BUCKETS - MAX_EXACT))
    return min(large, N_BUCKETS - 1)


def _bucket_segments():
    segs = []
    for n in range(MAX_DISTANCE):
        b = _bucket_of_distance(n)
        if not segs or segs[-1][1] != b:
            segs.append((n, b))
    assert all(_bucket_of_distance(n) == N_BUCKETS - 1 for n in range(segs[-1][0], 4 * MAX_DISTANCE))
    return segs


def _rms(x):
    return x * lax.rsqrt(jnp.mean(x * x, axis=-1, keepdims=True) + EPS)


def _dot(a, b):
    return jnp.dot(a, b, preferred_element_type=_F32)


def _params(*sem):
    return pltpu.CompilerParams(dimension_semantics=sem, vmem_limit_bytes=VMEM_LIMIT)


def _const_spec(shape):
    nd = len(shape)
    return pl.BlockSpec(shape, lambda *_: (0,) * nd)


def _mod_kernel(c_ref, w_ref, b_ref, o_ref):
    c = c_ref[...]
    c_act = c * jax.nn.sigmoid(c)
    o_ref[0] = _dot(c_act, w_ref[0]) + b_ref[0]


def _modulation(c, mod_w, mod_b):
    depth, d, n = mod_w.shape
    bsz = c.shape[0]
    tn = 1024
    return pl.pallas_call(
        _mod_kernel,
        grid=(depth, n // tn),
        in_specs=[
            pl.BlockSpec((bsz, d), lambda l, j: (0, 0)),
            pl.BlockSpec((1, d, tn), lambda l, j: (l, 0, j)),
            pl.BlockSpec((1, 1, tn), lambda l, j: (l, 0, j)),
        ],
        out_specs=pl.BlockSpec((1, bsz, tn), lambda l, j: (l, 0, j)),
        out_shape=jax.ShapeDtypeStruct((depth, bsz, n), _F32),
        compiler_params=_params("arbitrary", "arbitrary"),
        name="mod",
    )(c, mod_w, mod_b.reshape(depth, 1, n))


def _conv_a_kernel(x_ref, g_ref, sc_ref, sh_ref, w1_ref, b1_ref, u_ref):
    d = x_ref.shape[-1]
    h = _rms(x_ref[0]) * g_ref[...] * (1.0 + sc_ref[0]) + sh_ref[0]
    hb = h.astype(_BF16)
    a = _dot(hb, w1_ref[:, :d]) + b1_ref[:, :d]
    gt = _dot(hb, w1_ref[:, d:]) + b1_ref[:, d:]
    u_ref[0] = a * jax.nn.sigmoid(gt)


def _conv_a(x, g, sc, sh, w1, b1):
    bsz, seq, d = x.shape
    tm = ROW_TILE
    vec = pl.BlockSpec((1, 1, d), lambda b, i: (b, 0, 0))
    return pl.pallas_call(
        _conv_a_kernel,
        grid=(bsz, seq // tm),
        in_specs=[
            pl.BlockSpec((1, tm, d), lambda b, i: (b, i, 0)),
            _const_spec((1, d)), vec, vec,
            _const_spec((d, 2 * d)), _const_spec((1, 2 * d)),
        ],
        out_specs=pl.BlockSpec((1, tm, d), lambda b, i: (b, i, 0)),
        out_shape=jax.ShapeDtypeStruct((bsz, seq, d), _F32),
        compiler_params=_params("parallel", "parallel"),
        name="conv_a",
    )(x, g, sc, sh, w1, b1)


def _conv_b_kernel(x_ref, ucur_ref, uprev_ref, dw_ref, dwb_ref, lng_ref, lnb_ref, w2_ref, b2_ref,
                   g_ref, gate_ref, o_ref, ubuf, cbuf):
    tm, d = ucur_ref.shape[1], ucur_ref.shape[2]
    i = pl.program_id(1)
    for c in range(d // LANES):
        lanes = slice(c * LANES, (c + 1) * LANES)
        prev = uprev_ref[0, :, lanes]
        ubuf[c, 0:CONV_HALO, :] = jnp.where(i > 0, prev, jnp.zeros_like(prev))
        ubuf[c, CONV_HALO:, :] = ucur_ref[0, :, lanes]

    first_shift = CONV_HALO - (CONV_WIDTH - 1)

    def row_block(r, carry):
        r0 = pl.multiple_of(r * CONV_ROWS, CONV_ROWS)
        for c in range(d // LANES):
            lanes = slice(c * LANES, (c + 1) * LANES)
            acc = jnp.broadcast_to(dwb_ref[:, lanes], (CONV_ROWS, LANES))
            for j in range(CONV_WIDTH):
                acc = acc + ubuf[c, pl.ds(r0 + first_shift + j, CONV_ROWS), :] * dw_ref[j:j + 1, lanes]
            cbuf[pl.ds(r0, CONV_ROWS), lanes] = acc
        return carry

    lax.fori_loop(0, tm // CONV_ROWS, row_block, 0)

    cv = cbuf[...]
    mu = jnp.mean(cv, axis=-1, keepdims=True)
    cc = cv - mu
    var = jnp.mean(cc * cc, axis=-1, keepdims=True)
    z = cc * lax.rsqrt(var + EPS) * lng_ref[...] + lnb_ref[...]
    z = z * jax.nn.sigmoid(z)
    y = _dot(z.astype(_BF16), w2_ref[...]) + b2_ref[...]
    o_ref[0] = x_ref[0] + gate_ref[0] * (_rms(y) * g_ref[...])


def _conv_b(x, u, dw, dwb, lng, lnb, w2, b2, g, gate):
    bsz, seq, d = x.shape
    tm = ROW_TILE
    ratio = tm // CONV_HALO
    vec = pl.BlockSpec((1, 1, d), lambda b, i: (b, 0, 0))
    tile = pl.BlockSpec((1, tm, d), lambda b, i: (b, i, 0))
    return pl.pallas_call(
        _conv_b_kernel,
        grid=(bsz, seq // tm),
        in_specs=[
            tile, tile,
            pl.BlockSpec((1, CONV_HALO, d), lambda b, i: (b, jnp.maximum(i * ratio - 1, 0), 0)),
            _const_spec((CONV_WIDTH, d)), _const_spec((1, d)), _const_spec((1, d)), _const_spec((1, d)),
            _const_spec((d, d)), _const_spec((1, d)), _const_spec((1, d)), vec,
        ],
        out_specs=tile,
        out_shape=jax.ShapeDtypeStruct((bsz, seq, d), _F32),
        scratch_shapes=[pltpu.VMEM((d // LANES, tm + CONV_HALO, LANES), _F32), pltpu.VMEM((tm, d), _F32)],
        compiler_params=_params("parallel", "parallel"),
        name="conv_b",
    )(x, u, u, dw, dwb, lng, lnb, w2, b2, g, gate)


def _ffn_kernel(x_ref, g_ref, sc_ref, sh_ref, win_ref, dw_ref, dwb_ref, wout_ref, g3_ref, gate_ref, o_ref,
                hbuf, ubuf, carry, pbuf):
    tm = x_ref.shape[1]
    f = wout_ref.shape[0]
    i = pl.program_id(1)

    @pl.when(i == 0)
    def _():
        carry[...] = jnp.zeros_like(carry)

    h = _rms(x_ref[0]) * g_ref[...] * (1.0 + sc_ref[0]) + sh_ref[0]
    hbuf[...] = h.astype(_BF16)

    def conv3(col0, slab0):
        u = _dot(hbuf[...], win_ref[:, col0:col0 + FFN_CHUNK])
        outs = []
        for k in range(FFN_CHUNK // LANES):
            cols = slice(col0 + k * LANES, col0 + (k + 1) * LANES)
            uk = u[:, k * LANES:(k + 1) * LANES]
            ubuf[slab0 + k, 0:FFN_HALO, :] = carry[:, cols]
            ubuf[slab0 + k, FFN_HALO:, :] = uk
            carry[:, cols] = uk[tm - FFN_HALO:, :]
            out = dwb_ref[:, cols]
            for j in range(FFN_CONV_WIDTH):
                shift = FFN_CONV_WIDTH - 1 - j
                out = out + ubuf[slab0 + k, pl.ds(FFN_HALO - shift, tm), :] * dw_ref[j:j + 1, cols]
            outs.append(out)
        return outs

    slabs = FFN_CHUNK // LANES
    for idx, c0 in enumerate(range(0, f, FFN_CHUNK)):
        base = (idx % 2) * 2 * slabs
        gparts = conv3(c0, base)
        vparts = conv3(f + c0, base + slabs)
        for k in range(slabs):
            gk, vk = gparts[k], vparts[k]
            pbuf[:, c0 + k * LANES:c0 + (k + 1) * LANES] = (gk * jax.nn.sigmoid(gk) * vk).astype(_BF16)

    y = _dot(pbuf[...], wout_ref[...])
    o_ref[0] = x_ref[0] + gate_ref[0] * (_rms(y) * g3_ref[...])


def _ffn(x, g, sc, sh, w_in, dw, dwb, w_out, g3, gate):
    bsz, seq, d = x.shape
    f = w_out.shape[0]
    tm = ROW_TILE
    vec = pl.BlockSpec((1, 1, d), lambda b, i: (b, 0, 0))
    tile = pl.BlockSpec((1, tm, d), lambda b, i: (b, i, 0))
    single = dict(pipeline_mode=pl.Buffered(1))
    return pl.pallas_call(
        _ffn_kernel,
        grid=(bsz, seq // tm),
        in_specs=[
            tile, _const_spec((1, d)), vec, vec,
            pl.BlockSpec((d, 2 * f), lambda b, i: (0, 0), **single),
            _const_spec((FFN_CONV_WIDTH, 2 * f)), _const_spec((1, 2 * f)),
            pl.BlockSpec((f, d), lambda b, i: (0, 0), **single),
            _const_spec((1, d)), vec,
        ],
        out_specs=tile,
        out_shape=jax.ShapeDtypeStruct((bsz, seq, d), _F32),
        scratch_shapes=[
            pltpu.VMEM((tm, d), _BF16),
            pltpu.VMEM((4 * (FFN_CHUNK // LANES), tm + FFN_HALO, LANES), _F32),
            pltpu.VMEM((FFN_HALO, 2 * f), _F32),
            pltpu.VMEM((tm, f), _BF16),
        ],
        compiler_params=_params("arbitrary", "arbitrary"),
        name="ffn",
    )(x, g, sc, sh, w_in, dw, dwb, w_out, g3, gate)


def _kv_kernel(x_ref, g_ref, wk_ref, wvt_ref, k_ref, vt_ref):
    tm = x_ref.shape[1]
    tk = vt_ref.shape[-1]
    hb = (_rms(x_ref[0]) * g_ref[...]).astype(_BF16)
    k = _dot(hb, wk_ref[...])
    vt = lax.dot_general(wvt_ref[...], hb, (((1,), (1,)), ((), ())), preferred_element_type=_F32)
    for hd in range(N_HEADS):
        rows = slice(hd * V_HEAD_DIM, (hd + 1) * V_HEAD_DIM)
        k_ref[0, hd] = k[:, rows].astype(_BF16)
        for c in range(tm // tk):
            vt_ref[0, hd, c, :V_HEAD_DIM, :] = vt[rows, c * tk:(c + 1) * tk].astype(_BF16)
            vt_ref[0, hd, c, V_HEAD_DIM:, :] = jnp.ones((ONES_ROWS, tk), _BF16)


def _kv(x, g, wk, wvt):
    bsz, seq, d = x.shape
    tm, tk = ROW_TILE, ATTN_TILE
    return pl.pallas_call(
        _kv_kernel,
        grid=(bsz, seq // tm),
        in_specs=[
            pl.BlockSpec((1, tm, d), lambda b, i: (b, i, 0)),
            _const_spec((1, d)), _const_spec((d, d)), _const_spec((d, d)),
        ],
        out_specs=[
            pl.BlockSpec((1, N_HEADS, tm, V_HEAD_DIM), lambda b, i: (b, 0, i, 0)),
            pl.BlockSpec((1, N_HEADS, tm // tk, V_ROWS, tk), lambda b, i: (b, 0, i, 0, 0)),
        ],
        out_shape=[
            jax.ShapeDtypeStruct((bsz, N_HEADS, seq, V_HEAD_DIM), _BF16),
            jax.ShapeDtypeStruct((bsz, N_HEADS, seq // tk, V_ROWS, tk), _BF16),
        ],
        compiler_params=_params("parallel", "parallel"),
        name="kv",
    )(x, g, wk, wvt)


def _q_kernel(x_ref, g_ref, sc_ref, sh_ref, wqt_ref, q_ref):
    tm = x_ref.shape[1]
    tq = q_ref.shape[-1] // 2
    h = _rms(x_ref[0]) * g_ref[...] * (1.0 + sc_ref[0]) + sh_ref[0]
    qt = lax.dot_general(wqt_ref[...], h.astype(_BF16), (((1,), (1,)), ((), ())),
                         preferred_element_type=_F32) * (HEAD_DIM ** -0.5 * LOG2E)
    first = lax.broadcasted_iota(jnp.int32, (V_HEAD_DIM, tq), 0) < HEAD_DIM
    for hd in range(N_HEADS):
        for c in range(tm // tq):
            qh = qt[hd * V_HEAD_DIM:(hd + 1) * V_HEAD_DIM, c * tq:(c + 1) * tq]
            q_ref[0, hd, c, :, :tq] = jnp.where(first, qh, 0.0).astype(_BF16)
            q_ref[0, hd, c, :, tq:] = jnp.where(first, 0.0, qh).astype(_BF16)


def _qproj(x, g, sc, sh, wqt):
    bsz, seq, d = x.shape
    tm, tq = ROW_TILE, ATTN_TILE
    vec = pl.BlockSpec((1, 1, d), lambda b, i: (b, 0, 0))
    return pl.pallas_call(
        _q_kernel,
        grid=(bsz, seq // tm),
        in_specs=[pl.BlockSpec((1, tm, d), lambda b, i: (b, i, 0)), _const_spec((1, d)), vec, vec,
                  _const_spec((d, d))],
        out_specs=pl.BlockSpec((1, N_HEADS, tm // tq, V_HEAD_DIM, 2 * tq), lambda b, i: (b, 0, i, 0, 0)),
        out_shape=jax.ShapeDtypeStruct((bsz, N_HEADS, seq // tq, V_HEAD_DIM, 2 * tq), _BF16),
        compiler_params=_params("parallel", "parallel"),
        name="qproj",
    )(x, g, sc, sh, wqt)


def _bias_kernel(rb_ref, o_ref):
    hd = pl.program_id(0)
    t = o_ref.shape[-1]
    key = lax.broadcasted_iota(jnp.int32, (t, t), 0)
    qry = lax.broadcasted_iota(jnp.int32, (t, t), 1)
    segs = _bucket_segments()
    far = rb_ref[N_BUCKETS - 1, hd]
    for which in range(2):
        rel = qry - key + which * t
        val = jnp.full((t, t), 0.0, _F32)
        for k in range(len(segs) - 2, -1, -1):
            val = jnp.where(rel < segs[k + 1][0], (rb_ref[segs[k][1], hd] - far) * LOG2E, val)
        o_ref[0, which] = jnp.where(rel >= 0, val, MASK_VALUE)


def _bias_tiles(rel_bias):
    t = ATTN_TILE
    return pl.pallas_call(
        _bias_kernel,
        grid=(N_HEADS,),
        in_specs=[pl.BlockSpec(memory_space=pltpu.SMEM)],
        out_specs=pl.BlockSpec((1, 2, t, t), lambda h: (h, 0, 0, 0)),
        out_shape=jax.ShapeDtypeStruct((N_HEADS, 2, t, t), _F32),
        compiler_params=_params("arbitrary"),
        name="bias",
    )(rel_bias)


def _attn_kernel(q_ref, k_ref, vt_ref, bias_ref, lam_ref, sg_ref, o_ref, m_sc, acc_sc, s_a, s_b, x_a, x_b,
                 *, lambda_init):
    t = ATTN_TILE
    dv = V_HEAD_DIM
    heads = q_ref.shape[1]
    i = pl.program_id(2)
    m_sc[...] = jnp.full_like(m_sc, MASK_VALUE)
    acc_sc[...] = jnp.zeros_like(acc_sc)

    def scores_into(buf, j, hh):
        s_buf, x_buf = buf
        rows = pl.ds(pl.multiple_of(j * t, t), t)
        s = _dot(k_ref[0, hh, rows, :], q_ref[0, hh, 0])
        s_buf[hh] = s
        x_buf[hh] = jnp.max(s, axis=0, keepdims=True)

    def step(j, which, cur, nxt):
        if nxt is not None:
            scores_into(nxt, j + 1, 0)
        for hh in range(heads):
            if nxt is not None and hh + 1 < heads:
                scores_into(nxt, j + 1, hh + 1)
            s = cur[0][hh]
            if which is None:
                m_cur = cur[1][hh]
            else:
                bias = bias_ref[hh, which]
                s = jnp.concatenate([s[:, :t] + bias, s[:, t:] + bias], axis=1)
                m_cur = jnp.max(s, axis=0, keepdims=True)
            m_prev = m_sc[hh]
            m_new = jnp.maximum(m_prev, m_cur)
            alpha = jnp.exp2(m_prev - m_new)
            p = jnp.exp2(s - m_new)
            acc_sc[hh] = alpha * acc_sc[hh] + _dot(vt_ref[0, hh, j], p.astype(_BF16))
            m_sc[hh] = m_new

    buf_a, buf_b = (s_a, x_a), (s_b, x_b)

    def far_pair(jj, carry):
        step(2 * jj, None, buf_a, buf_b)
        step(2 * jj + 1, None, buf_b, buf_a)
        return carry

    for hh in range(heads):
        scores_into(buf_a, 0, hh)
    lax.fori_loop(0, jnp.maximum((i - 1) // 2, 0), far_pair, 0)

    @pl.when(i % 2 == 0)
    def _():
        @pl.when(i >= 2)
        def _():
            step(i - 2, None, buf_a, buf_b)
            step(i - 1, 1, buf_b, buf_a)

        step(i, 0, buf_a, None)

    @pl.when(i % 2 == 1)
    def _():
        step(i - 1, 1, buf_a, buf_b)
        step(i, 0, buf_b, None)

    lv = lam_ref[...]
    lam = (jnp.exp(jnp.sum(lv[0:1] * lv[1:2], axis=-1, keepdims=True))
           - jnp.exp(jnp.sum(lv[2:3] * lv[3:4], axis=-1, keepdims=True)) + lambda_init)
    for hh in range(heads):
        acc = acc_sc[hh]
        o = acc[:dv] / acc[dv:dv + 1]
        o = o[:, :t] - lam * o[:, t:]
        o = o * lax.rsqrt(jnp.mean(o * o, axis=0, keepdims=True) + EPS) * sg_ref[...] * (1.0 - lambda_init)
        o_ref[0, :, hh * dv:(hh + 1) * dv] = o.T.astype(o_ref.dtype)


def _attention(qt, k, vt, bias, lam, sg, lambda_init):
    bsz, n_heads, seq, dv = k.shape
    t = ATTN_TILE
    nt = seq // t
    hp = ATTN_HEADS_PER_STEP
    return pl.pallas_call(
        functools.partial(_attn_kernel, lambda_init=lambda_init),
        grid=(bsz, n_heads // hp, nt),
        in_specs=[
            pl.BlockSpec((1, hp, 1, dv, 2 * t), lambda b, h, i: (b, h, i, 0, 0)),
            pl.BlockSpec((1, hp, seq, dv), lambda b, h, i: (b, h, 0, 0)),
            pl.BlockSpec((1, hp, nt, V_ROWS, t), lambda b, h, i: (b, h, 0, 0, 0)),
            pl.BlockSpec((hp, 2, t, t), lambda b, h, i: (h, 0, 0, 0)),
            pl.BlockSpec((4, HEAD_DIM), lambda b, h, i: (0, 0)),
            pl.BlockSpec((dv, 1), lambda b, h, i: (0, 0)),
        ],
        out_specs=pl.BlockSpec((1, t, hp * dv), lambda b, h, i: (b, i, h)),
        out_shape=jax.ShapeDtypeStruct((bsz, seq, n_heads * dv), _BF16),
        scratch_shapes=[pltpu.VMEM((hp, 1, 2 * t), _F32), pltpu.VMEM((hp, V_ROWS, 2 * t), _F32),
                        pltpu.VMEM((hp, t, 2 * t), _F32), pltpu.VMEM((hp, t, 2 * t), _F32),
                        pltpu.VMEM((hp, 1, 2 * t), _F32), pltpu.VMEM((hp, 1, 2 * t), _F32)],
        compiler_params=_params("parallel", "parallel", "arbitrary"),
        name="attn",
    )(qt, k, vt, bias, lam, sg)


def _o_kernel(x_ref, a_ref, wo_ref, g_ref, gate_ref, o_ref):
    y = _dot(a_ref[0], wo_ref[...])
    o_ref[0] = x_ref[0] + gate_ref[0] * (_rms(y) * g_ref[...])


def _oproj(x, a, wo, g, gate):
    bsz, seq, d = x.shape
    tm = ROW_TILE
    tile = pl.BlockSpec((1, tm, d), lambda b, i: (b, i, 0))
    return pl.pallas_call(
        _o_kernel,
        grid=(bsz, seq // tm),
        in_specs=[tile, tile, _const_spec((d, d)), _const_spec((1, d)),
                  pl.BlockSpec((1, 1, d), lambda b, i: (b, 0, 0))],
        out_specs=tile,
        out_shape=jax.ShapeDtypeStruct((bsz, seq, d), _F32),
        compiler_params=_params("parallel", "parallel"),
        name="oproj",
    )(x, a, wo, g, gate)


def kernel(x, c, mod_w, mod_b, norm_g, cm_w1, cm_b1, cm_dw, cm_dwb, cm_ln_g, cm_ln_b, cm_w2, cm_b2, kv_norm_g,
           w_k, w_v, w_q, lam, subln_g, w_o, rel_bias, ffn_w_in, ffn_dw, ffn_dwb, ffn_w_out):
    depth = mod_w.shape[0]
    n_conv = cm_w1.shape[0]
    bsz, seq, d = x.shape

    mod = _modulation(c, mod_w, mod_b)

    def row(v):
        return v.reshape(1, -1)

    kk = vt = bias = None
    for l in range(depth):
        sh_m, sc_m, g_m, sh_f, sc_f, g_f = [m.reshape(bsz, 1, d) for m in jnp.split(mod[l], 6, axis=-1)]
        g0, g1, g2, g3 = [row(norm_g[l, k]) for k in range(4)]
        if l < n_conv:
            u = _conv_a(x, g0, sc_m, sh_m, cm_w1[l].astype(_BF16), row(cm_b1[l]))
            x = _conv_b(x, u, cm_dw[l], row(cm_dwb[l]), row(cm_ln_g[l]), row(cm_ln_b[l]),
                        cm_w2[l].astype(_BF16), row(cm_b2[l]), g1, g_m)
        else:
            if l == n_conv:
                kk, vt = _kv(x, row(kv_norm_g), w_k.astype(_BF16), w_v.T.astype(_BF16))
                bias = _bias_tiles(rel_bias)
            j = l - n_conv
            lambda_init = 0.8 - 0.6 * math.exp(-0.3 * l)
            qt = _qproj(x, g0, sc_m, sh_m, w_q[j].T.astype(_BF16))
            a = _attention(qt, kk, vt, bias, lam[j], subln_g[j].reshape(-1, 1), lambda_init)
            x = _oproj(x, a, w_o[j].astype(_BF16), g1, g_m)
        x = _ffn(x, g2, sc_f, sh_f, ffn_w_in[l].astype(_BF16), ffn_dw[l], row(ffn_dwb[l]),
                 ffn_w_out[l].astype(_BF16), g3, g_f)
    return x
```

```python
import functools
import math

import jax
import jax.numpy as jnp
from jax import lax
from jax.experimental import pallas as pl
from jax.experimental.pallas import tpu as pltpu

N_HEADS = 8
HEAD_DIM = 64
V_HEAD_DIM = 128
CONV_WIDTH = 31
FFN_CONV_WIDTH = 3
N_BUCKETS = 32
MAX_DISTANCE = 128
MAX_EXACT = 16
EPS = 1e-6

ROW_TILE = 512
ATTN_TILE = 256
ATTN_HEADS_PER_STEP = 4
ONES_ROWS = 16
V_ROWS = V_HEAD_DIM + ONES_ROWS
LOG2E = math.log2(math.e)
CONV_HALO = 32
CONV_ROWS = 64
LANES = 128
FFN_CHUNK = 256
FFN_HALO = 8
MASK_VALUE = -1e30
VMEM_LIMIT = 56 * 1024 * 1024

_BF16 = jnp.bfloat16
_F32 = jnp.float32


def _bucket_of_distance(n):
    if n < MAX_EXACT:
        return n
    large = MAX_EXACT + int(math.log(n / MAX_EXACT) / math.log(MAX_DISTANCE / MAX_EXACT) * (N_BUCKETS - MAX_EXACT))
    return min(large, N_BUCKETS - 1)


def _bucket_segments():
    segs = []
    for n in range(MAX_DISTANCE):
        b = _bucket_of_distance(n)
        if not segs or segs[-1][1] != b:
            segs.append((n, b))
    assert all(_bucket_of_distance(n) == N_BUCKETS - 1 for n in range(segs[-1][0], 4 * MAX_DISTANCE))
    return segs


def _rms(x):
    return x * lax.rsqrt(jnp.mean(x * x, axis=-1, keepdims=True) + EPS)


def _dot(a, b):
    return jnp.dot(a, b, preferred_element_type=_F32)


def _params(*sem):
    return pltpu.CompilerParams(dimension_semantics=sem, vmem_limit_bytes=VMEM_LIMIT)


def _const_spec(shape):
    nd = len(shape)
    return pl.BlockSpec(shape, lambda *_: (0,) * nd)


def _mod_kernel(c_ref, w_ref, b_ref, o_ref):
    c = c_ref[...]
    c_act = c * jax.nn.sigmoid(c)
    o_ref[0] = _dot(c_act, w_ref[0]) + b_ref[0]


def _modulation(c, mod_w, mod_b):
    depth, d, n = mod_w.shape
    bsz = c.shape[0]
    tn = 1024
    return pl.pallas_call(
        _mod_kernel,
        grid=(depth, n // tn),
        in_specs=[
            pl.BlockSpec((bsz, d), lambda l, j: (0, 0)),
            pl.BlockSpec((1, d, tn), lambda l, j: (l, 0, j)),
            pl.BlockSpec((1, 1, tn), lambda l, j: (l, 0, j)),
        ],
        out_specs=pl.BlockSpec((1, bsz, tn), lambda l, j: (l, 0, j)),
        out_shape=jax.ShapeDtypeStruct((depth, bsz, n), _F32),
        compiler_params=_params("arbitrary", "arbitrary"),
        name="mod",
    )(c, mod_w, mod_b.reshape(depth, 1, n))


def _conv_a_kernel(x_ref, g_ref, sc_ref, sh_ref, w1_ref, b1_ref, u_ref):
    d = x_ref.shape[-1]
    h = _rms(x_ref[0]) * g_ref[...] * (1.0 + sc_ref[0]) + sh_ref[0]
    hb = h.astype(_BF16)
    a = _dot(hb, w1_ref[:, :d]) + b1_ref[:, :d]
    gt = _dot(hb, w1_ref[:, d:]) + b1_ref[:, d:]
    u_ref[0] = a * jax.nn.sigmoid(gt)


def _conv_a(x, g, sc, sh, w1, b1):
    bsz, seq, d = x.shape
    tm = ROW_TILE
    vec = pl.BlockSpec((1, 1, d), lambda b, i: (b, 0, 0))
    return pl.pallas_call(
        _conv_a_kernel,
        grid=(bsz, seq // tm),
        in_specs=[
            pl.BlockSpec((1, tm, d), lambda b, i: (b, i, 0)),
            _const_spec((1, d)), vec, vec,
            _const_spec((d, 2 * d)), _const_spec((1, 2 * d)),
        ],
        out_specs=pl.BlockSpec((1, tm, d), lambda b, i: (b, i, 0)),
        out_shape=jax.ShapeDtypeStruct((bsz, seq, d), _F32),
        compiler_params=_params("parallel", "parallel"),
        name="conv_a",
    )(x, g, sc, sh, w1, b1)


def _conv_b_kernel(x_ref, ucur_ref, uprev_ref, dw_ref, dwb_ref, lng_ref, lnb_ref, w2_ref, b2_ref,
                   g_ref, gate_ref, o_ref, ubuf, cbuf):
    tm, d = ucur_ref.shape[1], ucur_ref.shape[2]
    i = pl.program_id(1)
    for c in range(d // LANES):
        lanes = slice(c * LANES, (c + 1) * LANES)
        prev = uprev_ref[0, :, lanes]
        ubuf[c, 0:CONV_HALO, :] = jnp.where(i > 0, prev, jnp.zeros_like(prev))
        ubuf[c, CONV_HALO:, :] = ucur_ref[0, :, lanes]

    first_shift = CONV_HALO - (CONV_WIDTH - 1)

    def row_block(r, carry):
        r0 = pl.multiple_of(r * CONV_ROWS, CONV_ROWS)
        for c in range(d // LANES):
            lanes = slice(c * LANES, (c + 1) * LANES)
            acc = jnp.broadcast_to(dwb_ref[:, lanes], (CONV_ROWS, LANES))
            for j in range(CONV_WIDTH):
                acc = acc + ubuf[c, pl.ds(r0 + first_shift + j, CONV_ROWS), :] * dw_ref[j:j + 1, lanes]
            cbuf[pl.ds(r0, CONV_ROWS), lanes] = acc
        return carry

    lax.fori_loop(0, tm // CONV_ROWS, row_block, 0)

    cv = cbuf[...]
    mu = jnp.mean(cv, axis=-1, keepdims=True)
    cc = cv - mu
    var = jnp.mean(cc * cc, axis=-1, keepdims=True)
    z = cc * lax.rsqrt(var + EPS) * lng_ref[...] + lnb_ref[...]
    z = z * jax.nn.sigmoid(z)
    y = _dot(z.astype(_BF16), w2_ref[...]) + b2_ref[...]
    o_ref[0] = x_ref[0] + gate_ref[0] * (_rms(y) * g_ref[...])


def _conv_b(x, u, dw, dwb, lng, lnb, w2, b2, g, gate):
    bsz, seq, d = x.shape
    tm = ROW_TILE
    ratio = tm // CONV_HALO
    vec = pl.BlockSpec((1, 1, d), lambda b, i: (b, 0, 0))
    tile = pl.BlockSpec((1, tm, d), lambda b, i: (b, i, 0))
    return pl.pallas_call(
        _conv_b_kernel,
        grid=(bsz, seq // tm),
        in_specs=[
            tile, tile,
            pl.BlockSpec((1, CONV_HALO, d), lambda b, i: (b, jnp.maximum(i * ratio - 1, 0), 0)),
            _const_spec((CONV_WIDTH, d)), _const_spec((1, d)), _const_spec((1, d)), _const_spec((1, d)),
            _const_spec((d, d)), _const_spec((1, d)), _const_spec((1, d)), vec,
        ],
        out_specs=tile,
        out_shape=jax.ShapeDtypeStruct((bsz, seq, d), _F32),
        scratch_shapes=[pltpu.VMEM((d // LANES, tm + CONV_HALO, LANES), _F32), pltpu.VMEM((tm, d), _F32)],
        compiler_params=_params("parallel", "parallel"),
        name="conv_b",
    )(x, u, u, dw, dwb, lng, lnb, w2, b2, g, gate)


def _ffn_kernel(x_ref, g_ref, sc_ref, sh_ref, win_ref, dw_ref, dwb_ref, wout_ref, g3_ref, gate_ref, o_ref,
                hbuf, ubuf, carry, pbuf):
    tm = x_ref.shape[1]
    f = wout_ref.shape[0]
    i = pl.program_id(1)

    @pl.when(i == 0)
    def _():
        carry[...] = jnp.zeros_like(carry)

    h = _rms(x_ref[0]) * g_ref[...] * (1.0 + sc_ref[0]) + sh_ref[0]
    hbuf[...] = h.astype(_BF16)

    def conv3(col0, slab0):
        u = _dot(hbuf[...], win_ref[:, col0:col0 + FFN_CHUNK])
        outs = []
        for k in range(FFN_CHUNK // LANES):
            cols = slice(col0 + k * LANES, col0 + (k + 1) * LANES)
            uk = u[:, k * LANES:(k + 1) * LANES]
            ubuf[slab0 + k, 0:FFN_HALO, :] = carry[:, cols]
            ubuf[slab0 + k, FFN_HALO:, :] = uk
            carry[:, cols] = uk[tm - FFN_HALO:, :]
            out = dwb_ref[:, cols]
            for j in range(FFN_CONV_WIDTH):
                shift = FFN_CONV_WIDTH - 1 - j
                out = out + ubuf[slab0 + k, pl.ds(FFN_HALO - shift, tm), :] * dw_ref[j:j + 1, cols]
            outs.append(out)
        return outs

    slabs = FFN_CHUNK // LANES
    for idx, c0 in enumerate(range(0, f, FFN_CHUNK)):
        base = (idx % 2) * 2 * slabs
        gparts = conv3(c0, base)
        vparts = conv3(f + c0, base + slabs)
        for k in range(slabs):
            gk, vk = gparts[k], vparts[k]
            pbuf[:, c0 + k * LANES:c0 + (k + 1) * LANES] = (gk * jax.nn.sigmoid(gk) * vk).astype(_BF16)

    y = _dot(pbuf[...], wout_ref[...])
    o_ref[0] = x_ref[0] + gate_ref[0] * (_rms(y) * g3_ref[...])


def _ffn(x, g, sc, sh, w_in, dw, dwb, w_out, g3, gate):
    bsz, seq, d = x.shape
    f = w_out.shape[0]
    tm = ROW_TILE
    vec = pl.BlockSpec((1, 1, d), lambda b, i: (b, 0, 0))
    tile = pl.BlockSpec((1, tm, d), lambda b, i: (b, i, 0))
    single = dict(pipeline_mode=pl.Buffered(1))
    return pl.pallas_call(
        _ffn_kernel,
        grid=(bsz, seq // tm),
        in_specs=[
            tile, _const_spec((1, d)), vec, vec,
            pl.BlockSpec((d, 2 * f), lambda b, i: (0, 0), **single),
            _const_spec((FFN_CONV_WIDTH, 2 * f)), _const_spec((1, 2 * f)),
            pl.BlockSpec((f, d), lambda b, i: (0, 0), **single),
            _const_spec((1, d)), vec,
        ],
        out_specs=tile,
        out_shape=jax.ShapeDtypeStruct((bsz, seq, d), _F32),
        scratch_shapes=[
            pltpu.VMEM((tm, d), _BF16),
            pltpu.VMEM((4 * (FFN_CHUNK // LANES), tm + FFN_HALO, LANES), _F32),
            pltpu.VMEM((FFN_HALO, 2 * f), _F32),
            pltpu.VMEM((tm, f), _BF16),
        ],
        compiler_params=_params("arbitrary", "arbitrary"),
        name="ffn",
    )(x, g, sc, sh, w_in, dw, dwb, w_out, g3, gate)


def _kv_kernel(x_ref, g_ref, wk_ref, wvt_ref, k_ref, vt_ref):
    tm = x_ref.shape[1]
    tk = vt_ref.shape[-1]
    hb = (_rms(x_ref[0]) * g_ref[...]).astype(_BF16)
    k = _dot(hb, wk_ref[...])
    vt = lax.dot_general(wvt_ref[...], hb, (((1,), (1,)), ((), ())), preferred_element_type=_F32)
    for hd in range(N_HEADS):
        rows = slice(hd * V_HEAD_DIM, (hd + 1) * V_HEAD_DIM)
        k_ref[0, hd] = k[:, rows].astype(_BF16)
        for c in range(tm // tk):
            vt_ref[0, hd, c, :V_HEAD_DIM, :] = vt[rows, c * tk:(c + 1) * tk].astype(_BF16)
            vt_ref[0, hd, c, V_HEAD_DIM:, :] = jnp.ones((ONES_ROWS, tk), _BF16)


def _kv(x, g, wk, wvt):
    bsz, seq, d = x.shape
    tm, tk = ROW_TILE, ATTN_TILE
    return pl.pallas_call(
        _kv_kernel,
        grid=(bsz, seq // tm),
        in_specs=[
            pl.BlockSpec((1, tm, d), lambda b, i: (b, i, 0)),
            _const_spec((1, d)), _const_spec((d, d)), _const_spec((d, d)),
        ],
        out_specs=[
            pl.BlockSpec((1, N_HEADS, tm, V_HEAD_DIM), lambda b, i: (b, 0, i, 0)),
            pl.BlockSpec((1, N_HEADS, tm // tk, V_ROWS, tk), lambda b, i: (b, 0, i, 0, 0)),
        ],
        out_shape=[
            jax.ShapeDtypeStruct((bsz, N_HEADS, seq, V_HEAD_DIM), _BF16),
            jax.ShapeDtypeStruct((bsz, N_HEADS, seq // tk, V_ROWS, tk), _BF16),
        ],
        compiler_params=_params("parallel", "parallel"),
        name="kv",
    )(x, g, wk, wvt)


def _q_kernel(x_ref, g_ref, sc_ref, sh_ref, wqt_ref, q_ref):
    tm = x_ref.shape[1]
    tq = q_ref.shape[-1] // 2
    h = _rms(x_ref[0]) * g_ref[...] * (1.0 + sc_ref[0]) + sh_ref[0]
    qt = lax.dot_general(wqt_ref[...], h.astype(_BF16), (((1,), (1,)), ((), ())),
                         preferred_element_type=_F32) * (HEAD_DIM ** -0.5 * LOG2E)
    first = lax.broadcasted_iota(jnp.int32, (V_HEAD_DIM, tq), 0) < HEAD_DIM
    for hd in range(N_HEADS):
        for c in range(tm // tq):
            qh = qt[hd * V_HEAD_DIM:(hd + 1) * V_HEAD_DIM, c * tq:(c + 1) * tq]
            q_ref[0, hd, c, :, :tq] = jnp.where(first, qh, 0.0).astype(_BF16)
            q_ref[0, hd, c, :, tq:] = jnp.where(first, 0.0, qh).astype(_BF16)


def _qproj(x, g, sc, sh, wqt):
    bsz, seq, d = x.shape
    tm, tq = ROW_TILE, ATTN_TILE
    vec = pl.BlockSpec((1, 1, d), lambda b, i: (b, 0, 0))
    return pl.pallas_call(
        _q_kernel,
        grid=(bsz, seq // tm),
        in_specs=[pl.BlockSpec((1, tm, d), lambda b, i: (b, i, 0)), _const_spec((1, d)), vec, vec,
                  _const_spec((d, d))],
        out_specs=pl.BlockSpec((1, N_HEADS, tm // tq, V_HEAD_DIM, 2 * tq), lambda b, i: (b, 0, i, 0, 0)),
        out_shape=jax.ShapeDtypeStruct((bsz, N_HEADS, seq // tq, V_HEAD_DIM, 2 * tq), _BF16),
        compiler_params=_params("parallel", "parallel"),
        name="qproj",
    )(x, g, sc, sh, wqt)


def _bias_kernel(rb_ref, o_ref):
    hd = pl.program_id(0)
    t = o_ref.shape[-1]
    key = lax.broadcasted_iota(jnp.int32, (t, t), 0)
    qry = lax.broadcasted_iota(jnp.int32, (t, t), 1)
    segs = _bucket_segments()
    far = rb_ref[N_BUCKETS - 1, hd]
    for which in range(2):
        rel = qry - key + which * t
        val = jnp.full((t, t), 0.0, _F32)
        for k in range(len(segs) - 2, -1, -1):
            val = jnp.where(rel < segs[k + 1][0], (rb_ref[segs[k][1], hd] - far) * LOG2E, val)
        o_ref[0, which] = jnp.where(rel >= 0, val, MASK_VALUE)


def _bias_tiles(rel_bias):
    t = ATTN_TILE
    return pl.pallas_call(
        _bias_kernel,
        grid=(N_HEADS,),
        in_specs=[pl.BlockSpec(memory_space=pltpu.SMEM)],
        out_specs=pl.BlockSpec((1, 2, t, t), lambda h: (h, 0, 0, 0)),
        out_shape=jax.ShapeDtypeStruct((N_HEADS, 2, t, t), _F32),
        compiler_params=_params("arbitrary"),
        name="bias",
    )(rel_bias)


def _attn_kernel(q_ref, k_ref, vt_ref, bias_ref, lam_ref, sg_ref, o_ref, m_sc, acc_sc, s_a, s_b, x_a, x_b,
                 *, lambda_init):
    t = ATTN_TILE
    dv = V_HEAD_DIM
    heads = q_ref.shape[1]
    i = pl.program_id(2)
    m_sc[...] = jnp.full_like(m_sc, MASK_VALUE)
    acc_sc[...] = jnp.zeros_like(acc_sc)

    def scores_into(buf, j, hh):
        s_buf, x_buf = buf
        rows = pl.ds(pl.multiple_of(j * t, t), t)
        s = _dot(k_ref[0, hh, rows, :], q_ref[0, hh, 0])
        s_buf[hh] = s
        x_buf[hh] = jnp.max(s, axis=0, keepdims=True)

    def step(j, which, cur, nxt):
        if nxt is not None:
            scores_into(nxt, j + 1, 0)
        for hh in range(heads):
            if nxt is not None and hh + 1 < heads:
                scores_into(nxt, j + 1, hh + 1)
            s = cur[0][hh]
            if which is None:
                m_cur = cur[1][hh]
            else:
                bias = bias_ref[hh, which]
                s = jnp.concatenate([s[:, :t] + bias, s[:, t:] + bias], axis=1)
                m_cur = jnp.max(s, axis=0, keepdims=True)
            m_prev = m_sc[hh]
            m_new = jnp.maximum(m_prev, m_cur)
            alpha = jnp.exp2(m_prev - m_new)
            p = jnp.exp2(s - m_new)
            acc_sc[hh] = alpha * acc_sc[hh] + _dot(vt_ref[0, hh, j], p.astype(_BF16))
            m_sc[hh] = m_new

    buf_a, buf_b = (s_a, x_a), (s_b, x_b)

    def far_pair(jj, carry):
        ja = 2 * jj
        for hh in range(heads):
            m_prev = m_sc[hh]
            m_new = jnp.maximum(m_prev, jnp.maximum(x_a[hh], x_b[hh]))
            alpha = jnp.exp2(m_prev - m_new)
            pa = jnp.exp2(s_a[hh] - m_new).astype(_BF16)
            scores_into(buf_a, ja + 2, hh)
            pb = jnp.exp2(s_b[hh] - m_new).astype(_BF16)
            scores_into(buf_b, ja + 3, hh)
            vt2 = jnp.concatenate([vt_ref[0, hh, ja], vt_ref[0, hh, ja + 1]], axis=1)
            pv = _dot(vt2, jnp.concatenate([pa, pb], axis=0))
            acc_sc[hh] = alpha * acc_sc[hh] + pv
            m_sc[hh] = m_new
        return carry

    for hh in range(heads):
        scores_into(buf_a, 0, hh)

    @pl.when(i >= 1)
    def _():
        for hh in range(heads):
            scores_into(buf_b, 1, hh)

    lax.fori_loop(0, jnp.maximum((i - 1) // 2, 0), far_pair, 0)

    @pl.when(i % 2 == 0)
    def _():
        @pl.when(i >= 2)
        def _():
            step(i - 2, None, buf_a, None)
            step(i - 1, 1, buf_b, buf_a)

        step(i, 0, buf_a, None)

    @pl.when(i % 2 == 1)
    def _():
        step(i - 1, 1, buf_a, None)
        step(i, 0, buf_b, None)

    lv = lam_ref[...]
    lam = (jnp.exp(jnp.sum(lv[0:1] * lv[1:2], axis=-1, keepdims=True))
           - jnp.exp(jnp.sum(lv[2:3] * lv[3:4], axis=-1, keepdims=True)) + lambda_init)
    for hh in range(heads):
        acc = acc_sc[hh]
        o = acc[:dv] / acc[dv:dv + 1]
        o = o[:, :t] - lam * o[:, t:]
        o = o * lax.rsqrt(jnp.mean(o * o, axis=0, keepdims=True) + EPS) * sg_ref[...] * (1.0 - lambda_init)
        o_ref[0, :, hh * dv:(hh + 1) * dv] = o.T.astype(o_ref.dtype)


def _attention(qt, k, vt, bias, lam, sg, lambda_init):
    bsz, n_heads, seq, dv = k.shape
    t = ATTN_TILE
    nt = seq // t
    hp = ATTN_HEADS_PER_STEP
    return pl.pallas_call(
        functools.partial(_attn_kernel, lambda_init=lambda_init),
        grid=(bsz, n_heads // hp, nt),
        in_specs=[
            pl.BlockSpec((1, hp, 1, dv, 2 * t), lambda b, h, i: (b, h, i, 0, 0)),
            pl.BlockSpec((1, hp, seq, dv), lambda b, h, i: (b, h, 0, 0)),
            pl.BlockSpec((1, hp, nt, V_ROWS, t), lambda b, h, i: (b, h, 0, 0, 0)),
            pl.BlockSpec((hp, 2, t, t), lambda b, h, i: (h, 0, 0, 0)),
            pl.BlockSpec((4, HEAD_DIM), lambda b, h, i: (0, 0)),
            pl.BlockSpec((dv, 1), lambda b, h, i: (0, 0)),
        ],
        out_specs=pl.BlockSpec((1, t, hp * dv), lambda b, h, i: (b, i, h)),
        out_shape=jax.ShapeDtypeStruct((bsz, seq, n_heads * dv), _BF16),
        scratch_shapes=[pltpu.VMEM((hp, 1, 2 * t), _F32), pltpu.VMEM((hp, V_ROWS, 2 * t), _F32),
                        pltpu.VMEM((hp, t, 2 * t), _F32), pltpu.VMEM((hp, t, 2 * t), _F32),
                        pltpu.VMEM((hp, 1, 2 * t), _F32), pltpu.VMEM((hp, 1, 2 * t), _F32)],
        compiler_params=_params("parallel", "parallel", "arbitrary"),
        name="attn",
    )(qt, k, vt, bias, lam, sg)


def _o_kernel(x_ref, a_ref, wo_ref, g_ref, gate_ref, o_ref):
    y = _dot(a_ref[0], wo_ref[...])
    o_ref[0] = x_ref[0] + gate_ref[0] * (_rms(y) * g_ref[...])


def _oproj(x, a, wo, g, gate):
    bsz, seq, d = x.shape
    tm = ROW_TILE
    tile = pl.BlockSpec((1, tm, d), lambda b, i: (b, i, 0))
    return pl.pallas_call(
        _o_kernel,
        grid=(bsz, seq // tm),
        in_specs=[tile, tile, _const_spec((d, d)), _const_spec((1, d)),
                  pl.BlockSpec((1, 1, d), lambda b, i: (b, 0, 0))],
        out_specs=tile,
        out_shape=jax.ShapeDtypeStruct((bsz, seq, d), _F32),
        compiler_params=_params("parallel", "parallel"),
        name="oproj",
    )(x, a, wo, g, gate)


def kernel(x, c, mod_w, mod_b, norm_g, cm_w1, cm_b1, cm_dw, cm_dwb, cm_ln_g, cm_ln_b, cm_w2, cm_b2, kv_norm_g,
           w_k, w_v, w_q, lam, subln_g, w_o, rel_bias, ffn_w_in, ffn_dw, ffn_dwb, ffn_w_out):
    depth = mod_w.shape[0]
    n_conv = cm_w1.shape[0]
    bsz, seq, d = x.shape

    mod = _modulation(c, mod_w, mod_b)

    def row(v):
        return v.reshape(1, -1)

    kk = vt = bias = None
    for l in range(depth):
        sh_m, sc_m, g_m, sh_f, sc_f, g_f = [m.reshape(bsz, 1, d) for m in jnp.split(mod[l], 6, axis=-1)]
        g0, g1, g2, g3 = [row(norm_g[l, k]) for k in range(4)]
        if l < n_conv:
            u = _conv_a(x, g0, sc_m, sh_m, cm_w1[l].astype(_BF16), row(cm_b1[l]))
            x = _conv_b(x, u, cm_dw[l], row(cm_dwb[l]), row(cm_ln_g[l]), row(cm_ln_b[l]),
                        cm_w2[l].astype(_BF16), row(cm_b2[l]), g1, g_m)
        else:
            if l == n_conv:
                kk, vt = _kv(x, row(kv_norm_g), w_k.astype(_BF16), w_v.T.astype(_BF16))
                bias = _bias_tiles(rel_bias)
            j = l - n_conv
            lambda_init = 0.8 - 0.6 * math.exp(-0.3 * l)
            qt = _qproj(x, g0, sc_m, sh_m, w_q[j].T.astype(_BF16))
            a = _attention(qt, kk, vt, bias, lam[j], subln_g[j].reshape(-1, 1), lambda_init)
            x = _oproj(x, a, w_o[j].astype(_BF16), g1, g_m)
        x = _ffn(x, g2, sc_f, sh_f, ffn_w_in[l].astype(_BF16), ffn_dw[l], row(ffn_dwb[l]),
                 ffn_w_out[l].astype(_BF16), g3, g_f)
    return x
```

```python
import functools
import math

import jax
import jax.numpy as jnp
from jax import lax
from jax.experimental import pallas as pl
from jax.experimental.pallas import tpu as pltpu

N_HEADS = 8
HEAD_DIM = 64
V_HEAD_DIM = 128
CONV_WIDTH = 31
FFN_CONV_WIDTH = 3
N_BUCKETS = 32
MAX_DISTANCE = 128
MAX_EXACT = 16
EPS = 1e-6

ROW_TILE = 512
ATTN_TILE = 256
ATTN_HEADS_PER_STEP = 8
ONES_ROWS = 16
V_ROWS = V_HEAD_DIM + ONES_ROWS
LOG2E = math.log2(math.e)
CONV_HALO = 32
CONV_ROWS = 64
LANES = 128
FFN_CHUNK = 256
FFN_HALO = 8
FFN_SUBTILES = 1
MASK_VALUE = -1e30
VMEM_LIMIT = 56 * 1024 * 1024

_BF16 = jnp.bfloat16
_F32 = jnp.float32


def _bucket_of_distance(n):
    if n < MAX_EXACT:
        return n
    large = MAX_EXACT + int(math.log(n / MAX_EXACT) / math.log(MAX_DISTANCE / MAX_EXACT) * (N_BUCKETS - MAX_EXACT))
    return min(large, N_BUCKETS - 1)


def _bucket_segments():
    segs = []
    for n in range(MAX_DISTANCE):
        b = _bucket_of_distance(n)
        if not segs or segs[-1][1] != b:
            segs.append((n, b))
    assert all(_bucket_of_distance(n) == N_BUCKETS - 1 for n in range(segs[-1][0], 4 * MAX_DISTANCE))
    return segs


def _rms(x):
    return x * lax.rsqrt(jnp.mean(x * x, axis=-1, keepdims=True) + EPS)


def _dot(a, b):
    return jnp.dot(a, b, preferred_element_type=_F32)


def _params(*sem, flags=None):
    return pltpu.CompilerParams(dimension_semantics=sem, vmem_limit_bytes=VMEM_LIMIT, flags=flags)


def _const_spec(shape):
    nd = len(shape)
    return pl.BlockSpec(shape, lambda *_: (0,) * nd)


def _mod_kernel(c_ref, w_ref, b_ref, o_ref):
    c = c_ref[...]
    c_act = c * jax.nn.sigmoid(c)
    o_ref[0] = _dot(c_act, w_ref[0]) + b_ref[0]


def _modulation(c, mod_w, mod_b):
    depth, d, n = mod_w.shape
    bsz = c.shape[0]
    tn = 1024
    return pl.pallas_call(
        _mod_kernel,
        grid=(depth, n // tn),
        in_specs=[
            pl.BlockSpec((bsz, d), lambda l, j: (0, 0)),
            pl.BlockSpec((1, d, tn), lambda l, j: (l, 0, j)),
            pl.BlockSpec((1, 1, tn), lambda l, j: (l, 0, j)),
        ],
        out_specs=pl.BlockSpec((1, bsz, tn), lambda l, j: (l, 0, j)),
        out_shape=jax.ShapeDtypeStruct((depth, bsz, n), _F32),
        compiler_params=_params("arbitrary", "arbitrary"),
        name="mod",
    )(c, mod_w, mod_b.reshape(depth, 1, n))


def _conv_a_kernel(x_ref, g_ref, sc_ref, sh_ref, w1_ref, b1_ref, u_ref):
    d = x_ref.shape[-1]
    h = _rms(x_ref[0]) * g_ref[...] * (1.0 + sc_ref[0]) + sh_ref[0]
    hb = h.astype(_BF16)
    a = _dot(hb, w1_ref[:, :d]) + b1_ref[:, :d]
    gt = _dot(hb, w1_ref[:, d:]) + b1_ref[:, d:]
    u_ref[0] = a * jax.nn.sigmoid(gt)


def _conv_a(x, g, sc, sh, w1, b1):
    bsz, seq, d = x.shape
    tm = ROW_TILE
    vec = pl.BlockSpec((1, 1, d), lambda b, i: (b, 0, 0))
    return pl.pallas_call(
        _conv_a_kernel,
        grid=(bsz, seq // tm),
        in_specs=[
            pl.BlockSpec((1, tm, d), lambda b, i: (b, i, 0)),
            _const_spec((1, d)), vec, vec,
            _const_spec((d, 2 * d)), _const_spec((1, 2 * d)),
        ],
        out_specs=pl.BlockSpec((1, tm, d), lambda b, i: (b, i, 0)),
        out_shape=jax.ShapeDtypeStruct((bsz, seq, d), _F32),
        compiler_params=_params("parallel", "parallel"),
        name="conv_a",
    )(x, g, sc, sh, w1, b1)


def _conv_b_kernel(x_ref, ucur_ref, uprev_ref, dw_ref, dwb_ref, lng_ref, lnb_ref, w2_ref, b2_ref,
                   g_ref, gate_ref, o_ref, ubuf, cbuf):
    tm, d = ucur_ref.shape[1], ucur_ref.shape[2]
    i = pl.program_id(1)
    for c in range(d // LANES):
        lanes = slice(c * LANES, (c + 1) * LANES)
        prev = uprev_ref[0, :, lanes]
        ubuf[c, 0:CONV_HALO, :] = jnp.where(i > 0, prev, jnp.zeros_like(prev))
        ubuf[c, CONV_HALO:, :] = ucur_ref[0, :, lanes]

    first_shift = CONV_HALO - (CONV_WIDTH - 1)

    def row_block(r, carry):
        r0 = pl.multiple_of(r * CONV_ROWS, CONV_ROWS)
        for c in range(d // LANES):
            lanes = slice(c * LANES, (c + 1) * LANES)
            acc = jnp.broadcast_to(dwb_ref[:, lanes], (CONV_ROWS, LANES))
            for j in range(CONV_WIDTH):
                acc = acc + ubuf[c, pl.ds(r0 + first_shift + j, CONV_ROWS), :] * dw_ref[j:j + 1, lanes]
            cbuf[pl.ds(r0, CONV_ROWS), lanes] = acc
        return carry

    lax.fori_loop(0, tm // CONV_ROWS, row_block, 0)

    cv = cbuf[...]
    mu = jnp.mean(cv, axis=-1, keepdims=True)
    cc = cv - mu
    var = jnp.mean(cc * cc, axis=-1, keepdims=True)
    z = cc * lax.rsqrt(var + EPS) * lng_ref[...] + lnb_ref[...]
    z = z * jax.nn.sigmoid(z)
    y = _dot(z.astype(_BF16), w2_ref[...]) + b2_ref[...]
    o_ref[0] = x_ref[0] + gate_ref[0] * (_rms(y) * g_ref[...])


def _conv_b(x, u, dw, dwb, lng, lnb, w2, b2, g, gate):
    bsz, seq, d = x.shape
    tm = ROW_TILE
    ratio = tm // CONV_HALO
    vec = pl.BlockSpec((1, 1, d), lambda b, i: (b, 0, 0))
    tile = pl.BlockSpec((1, tm, d), lambda b, i: (b, i, 0))
    return pl.pallas_call(
        _conv_b_kernel,
        grid=(bsz, seq // tm),
        in_specs=[
            tile, tile,
            pl.BlockSpec((1, CONV_HALO, d), lambda b, i: (b, jnp.maximum(i * ratio - 1, 0), 0)),
            _const_spec((CONV_WIDTH, d)), _const_spec((1, d)), _const_spec((1, d)), _const_spec((1, d)),
            _const_spec((d, d)), _const_spec((1, d)), _const_spec((1, d)), vec,
        ],
        out_specs=tile,
        out_shape=jax.ShapeDtypeStruct((bsz, seq, d), _F32),
        scratch_shapes=[pltpu.VMEM((d // LANES, tm + CONV_HALO, LANES), _F32), pltpu.VMEM((tm, d), _F32)],
        compiler_params=_params("parallel", "parallel"),
        name="conv_b",
    )(x, u, u, dw, dwb, lng, lnb, w2, b2, g, gate)


def _ffn_kernel(x_ref, g_ref, sc_ref, sh_ref, win_ref, dw_ref, dwb_ref, wout_ref, g3_ref, gate_ref, o_ref,
                hbuf, ubuf, carry, pbuf):
    tm = ROW_TILE
    f = wout_ref.shape[0]
    i = pl.program_id(1)
    slabs = FFN_CHUNK // LANES

    @pl.when(i == 0)
    def _():
        carry[...] = jnp.zeros_like(carry)

    def sub_tile(rows, slab_set):
        h = _rms(x_ref[0, rows, :]) * g_ref[...] * (1.0 + sc_ref[0]) + sh_ref[0]
        hbuf[rows, :] = h.astype(_BF16)

        def up_project(col0, slab0):
            u = _dot(hbuf[rows, :], win_ref[:, col0:col0 + FFN_CHUNK])
            for k in range(slabs):
                cols = slice(col0 + k * LANES, col0 + (k + 1) * LANES)
                uk = u[:, k * LANES:(k + 1) * LANES]
                ubuf[slab0 + k, 0:FFN_HALO, :] = carry[:, cols]
                ubuf[slab0 + k, FFN_HALO:, :] = uk
                carry[:, cols] = uk[tm - FFN_HALO:, :]

        def conv3(col0, slab):
            cols = slice(col0, col0 + LANES)
            out = dwb_ref[:, cols]
            for j in range(FFN_CONV_WIDTH):
                shift = FFN_CONV_WIDTH - 1 - j
                out = out + ubuf[slab, pl.ds(FFN_HALO - shift, tm), :] * dw_ref[j:j + 1, cols]
            return out

        for idx, c0 in enumerate(range(0, f, FFN_CHUNK)):
            base = slab_set + (idx % 2) * 2 * slabs
            up_project(c0, base)
            up_project(f + c0, base + slabs)
            for k in range(slabs):
                gk = conv3(c0 + k * LANES, base + k)
                vk = conv3(f + c0 + k * LANES, base + slabs + k)
                pbuf[rows, c0 + k * LANES:c0 + (k + 1) * LANES] = (gk * jax.nn.sigmoid(gk) * vk).astype(_BF16)

        y = _dot(pbuf[rows, :], wout_ref[...])
        o_ref[0, rows, :] = x_ref[0, rows, :] + gate_ref[0] * (_rms(y) * g3_ref[...])

    for sub in range(FFN_SUBTILES):
        sub_tile(slice(sub * tm, (sub + 1) * tm), sub * 4 * slabs)


def _ffn(x, g, sc, sh, w_in, dw, dwb, w_out, g3, gate):
    bsz, seq, d = x.shape
    f = w_out.shape[0]
    tm = ROW_TILE * FFN_SUBTILES
    vec = pl.BlockSpec((1, 1, d), lambda b, i: (b, 0, 0))
    tile = pl.BlockSpec((1, tm, d), lambda b, i: (b, i, 0))
    single = dict(pipeline_mode=pl.Buffered(1))
    return pl.pallas_call(
        _ffn_kernel,
        grid=(bsz, seq // tm),
        in_specs=[
            tile, _const_spec((1, d)), vec, vec,
            pl.BlockSpec((d, 2 * f), lambda b, i: (0, 0), **single),
            _const_spec((FFN_CONV_WIDTH, 2 * f)), _const_spec((1, 2 * f)),
            pl.BlockSpec((f, d), lambda b, i: (0, 0), **single),
            _const_spec((1, d)), vec,
        ],
        out_specs=tile,
        out_shape=jax.ShapeDtypeStruct((bsz, seq, d), _F32),
        scratch_shapes=[
            pltpu.VMEM((tm, d), _BF16),
            pltpu.VMEM((FFN_SUBTILES * 4 * (FFN_CHUNK // LANES), ROW_TILE + FFN_HALO, LANES), _F32),
            pltpu.VMEM((FFN_HALO, 2 * f), _F32),
            pltpu.VMEM((tm, f), _BF16),
        ],
        compiler_params=_params("arbitrary", "arbitrary"),
        name="ffn",
    )(x, g, sc, sh, w_in, dw, dwb, w_out, g3, gate)


def _kv_kernel(x_ref, g_ref, wk_ref, wvt_ref, k_ref, vt_ref):
    tm = x_ref.shape[1]
    tk = vt_ref.shape[-1]
    hb = (_rms(x_ref[0]) * g_ref[...]).astype(_BF16)
    k = _dot(hb, wk_ref[...])
    vt = lax.dot_general(wvt_ref[...], hb, (((1,), (1,)), ((), ())), preferred_element_type=_F32)
    for hd in range(N_HEADS):
        rows = slice(hd * V_HEAD_DIM, (hd + 1) * V_HEAD_DIM)
        k_ref[0, hd] = k[:, rows].astype(_BF16)
        for c in range(tm // tk):
            vt_ref[0, hd, c, :V_HEAD_DIM, :] = vt[rows, c * tk:(c + 1) * tk].astype(_BF16)
            vt_ref[0, hd, c, V_HEAD_DIM:, :] = jnp.ones((ONES_ROWS, tk), _BF16)


def _kv(x, g, wk, wvt):
    bsz, seq, d = x.shape
    tm, tk = ROW_TILE, ATTN_TILE
    return pl.pallas_call(
        _kv_kernel,
        grid=(bsz, seq // tm),
        in_specs=[
            pl.BlockSpec((1, tm, d), lambda b, i: (b, i, 0)),
            _const_spec((1, d)), _const_spec((d, d)), _const_spec((d, d)),
        ],
        out_specs=[
            pl.BlockSpec((1, N_HEADS, tm, V_HEAD_DIM), lambda b, i: (b, 0, i, 0)),
            pl.BlockSpec((1, N_HEADS, tm // tk, V_ROWS, tk), lambda b, i: (b, 0, i, 0, 0)),
        ],
        out_shape=[
            jax.ShapeDtypeStruct((bsz, N_HEADS, seq, V_HEAD_DIM), _BF16),
            jax.ShapeDtypeStruct((bsz, N_HEADS, seq // tk, V_ROWS, tk), _BF16),
        ],
        compiler_params=_params("parallel", "parallel"),
        name="kv",
    )(x, g, wk, wvt)


def _q_kernel(x_ref, g_ref, sc_ref, sh_ref, wqt_ref, q_ref):
    tm = x_ref.shape[1]
    tq = q_ref.shape[-1] // 2
    h = _rms(x_ref[0]) * g_ref[...] * (1.0 + sc_ref[0]) + sh_ref[0]
    qt = lax.dot_general(wqt_ref[...], h.astype(_BF16), (((1,), (1,)), ((), ())),
                         preferred_element_type=_F32) * (HEAD_DIM ** -0.5 * LOG2E)
    first = lax.broadcasted_iota(jnp.int32, (V_HEAD_DIM, tq), 0) < HEAD_DIM
    for hd in range(N_HEADS):
        for c in range(tm // tq):
            qh = qt[hd * V_HEAD_DIM:(hd + 1) * V_HEAD_DIM, c * tq:(c + 1) * tq]
            q_ref[0, hd, c, :, :tq] = jnp.where(first, qh, 0.0).astype(_BF16)
            q_ref[0, hd, c, :, tq:] = jnp.where(first, 0.0, qh).astype(_BF16)


def _qproj(x, g, sc, sh, wqt):
    bsz, seq, d = x.shape
    tm, tq = ROW_TILE, ATTN_TILE
    vec = pl.BlockSpec((1, 1, d), lambda b, i: (b, 0, 0))
    return pl.pallas_call(
        _q_kernel,
        grid=(bsz, seq // tm),
        in_specs=[pl.BlockSpec((1, tm, d), lambda b, i: (b, i, 0)), _const_spec((1, d)), vec, vec,
                  _const_spec((d, d))],
        out_specs=pl.BlockSpec((1, N_HEADS, tm // tq, V_HEAD_DIM, 2 * tq), lambda b, i: (b, 0, i, 0, 0)),
        out_shape=jax.ShapeDtypeStruct((bsz, N_HEADS, seq // tq, V_HEAD_DIM, 2 * tq), _BF16),
        compiler_params=_params("parallel", "parallel"),
        name="qproj",
    )(x, g, sc, sh, wqt)


def _bias_kernel(rb_ref, o_ref):
    hd = pl.program_id(0)
    t = o_ref.shape[-1]
    key = lax.broadcasted_iota(jnp.int32, (t, t), 0)
    qry = lax.broadcasted_iota(jnp.int32, (t, t), 1)
    segs = _bucket_segments()
    far = rb_ref[N_BUCKETS - 1, hd]
    for which in range(2):
        rel = qry - key + which * t
        val = jnp.full((t, t), 0.0, _F32)
        for k in range(len(segs) - 2, -1, -1):
            val = jnp.where(rel < segs[k + 1][0], (rb_ref[segs[k][1], hd] - far) * LOG2E, val)
        o_ref[0, which] = jnp.where(rel >= 0, val, MASK_VALUE)


def _bias_tiles(rel_bias):
    t = ATTN_TILE
    return pl.pallas_call(
        _bias_kernel,
        grid=(N_HEADS,),
        in_specs=[pl.BlockSpec(memory_space=pltpu.SMEM)],
        out_specs=pl.BlockSpec((1, 2, t, t), lambda h: (h, 0, 0, 0)),
        out_shape=jax.ShapeDtypeStruct((N_HEADS, 2, t, t), _F32),
        compiler_params=_params("arbitrary"),
        name="bias",
    )(rel_bias)


def _attn_kernel(q_ref, k_ref, vt_ref, bias_ref, lam_ref, sg_ref, o_ref, m_sc, acc_sc, s_a, s_b, x_a, x_b,
                 *, lambda_init):
    t = ATTN_TILE
    dv = V_HEAD_DIM
    heads = q_ref.shape[1]
    i = pl.program_id(2)
    m_sc[...] = jnp.full_like(m_sc, MASK_VALUE)
    acc_sc[...] = jnp.zeros_like(acc_sc)

    def scores_into(buf, j, hh):
        s_buf, x_buf = buf
        rows = pl.ds(pl.multiple_of(j * t, t), t)
        s = _dot(k_ref[0, hh, rows, :], q_ref[0, hh, 0])
        s_buf[hh] = s
        x_buf[hh] = jnp.max(s, axis=0, keepdims=True)

    def step(j, which, cur, nxt):
        if nxt is not None:
            scores_into(nxt, j + 1, 0)
        for hh in range(heads):
            if nxt is not None and hh + 1 < heads:
                scores_into(nxt, j + 1, hh + 1)
            s = cur[0][hh]
            if which is None:
                m_cur = cur[1][hh]
            else:
                bias = bias_ref[hh, which]
                s = jnp.concatenate([s[:, :t] + bias, s[:, t:] + bias], axis=1)
                m_cur = jnp.max(s, axis=0, keepdims=True)
            m_prev = m_sc[hh]
            m_new = jnp.maximum(m_prev, m_cur)
            alpha = jnp.exp2(m_prev - m_new)
            p = jnp.exp2(s - m_new)
            acc_sc[hh] = alpha * acc_sc[hh] + _dot(vt_ref[0, hh, j], p.astype(_BF16))
            m_sc[hh] = m_new

    buf_a, buf_b = (s_a, x_a), (s_b, x_b)

    def far_pair(jj, carry):
        ja = 2 * jj
        for hh in range(heads):
            m_prev = m_sc[hh]
            m_new = jnp.maximum(m_prev, jnp.maximum(x_a[hh], x_b[hh]))
            alpha = jnp.exp2(m_prev - m_new)
            pa = jnp.exp2(s_a[hh] - m_new).astype(_BF16)
            scores_into(buf_a, ja + 2, hh)
            pb = jnp.exp2(s_b[hh] - m_new).astype(_BF16)
            scores_into(buf_b, ja + 3, hh)
            vt2 = jnp.concatenate([vt_ref[0, hh, ja], vt_ref[0, hh, ja + 1]], axis=1)
            pv = _dot(vt2, jnp.concatenate([pa, pb], axis=0))
            acc_sc[hh] = alpha * acc_sc[hh] + pv
            m_sc[hh] = m_new
        return carry

    for hh in range(heads):
        scores_into(buf_a, 0, hh)

    @pl.when(i >= 1)
    def _():
        for hh in range(heads):
            scores_into(buf_b, 1, hh)

    lax.fori_loop(0, jnp.maximum((i - 1) // 2, 0), far_pair, 0)

    @pl.when(i % 2 == 0)
    def _():
        @pl.when(i >= 2)
        def _():
            step(i - 2, None, buf_a, None)
            step(i - 1, 1, buf_b, buf_a)

        step(i, 0, buf_a, None)

    @pl.when(i % 2 == 1)
    def _():
        step(i - 1, 1, buf_a, None)
        step(i, 0, buf_b, None)

    lv = lam_ref[...]
    lam = (jnp.exp(jnp.sum(lv[0:1] * lv[1:2], axis=-1, keepdims=True))
           - jnp.exp(jnp.sum(lv[2:3] * lv[3:4], axis=-1, keepdims=True)) + lambda_init)
    for hh in range(heads):
        acc = acc_sc[hh]
        o = acc[:dv] / acc[dv:dv + 1]
        o = o[:, :t] - lam * o[:, t:]
        o = o * lax.rsqrt(jnp.mean(o * o, axis=0, keepdims=True) + EPS) * sg_ref[...] * (1.0 - lambda_init)
        o_ref[0, :, hh * dv:(hh + 1) * dv] = o.T.astype(o_ref.dtype)


def _attention(qt, k, vt, bias, lam, sg, lambda_init):
    bsz, n_heads, seq, dv = k.shape
    t = ATTN_TILE
    nt = seq // t
    hp = ATTN_HEADS_PER_STEP
    return pl.pallas_call(
        functools.partial(_attn_kernel, lambda_init=lambda_init),
        grid=(bsz, n_heads // hp, nt),
        in_specs=[
            pl.BlockSpec((1, hp, 1, dv, 2 * t), lambda b, h, i: (b, h, i, 0, 0)),
            pl.BlockSpec((1, hp, seq, dv), lambda b, h, i: (b, h, 0, 0), pipeline_mode=pl.Buffered(1)),
            pl.BlockSpec((1, hp, nt, V_ROWS, t), lambda b, h, i: (b, h, 0, 0, 0), pipeline_mode=pl.Buffered(1)),
            pl.BlockSpec((hp, 2, t, t), lambda b, h, i: (h, 0, 0, 0), pipeline_mode=pl.Buffered(1)),
            pl.BlockSpec((4, HEAD_DIM), lambda b, h, i: (0, 0)),
            pl.BlockSpec((dv, 1), lambda b, h, i: (0, 0)),
        ],
        out_specs=pl.BlockSpec((1, t, hp * dv), lambda b, h, i: (b, i, h)),
        out_shape=jax.ShapeDtypeStruct((bsz, seq, n_heads * dv), _BF16),
        scratch_shapes=[pltpu.VMEM((hp, 1, 2 * t), _F32), pltpu.VMEM((hp, V_ROWS, 2 * t), _F32),
                        pltpu.VMEM((hp, t, 2 * t), _F32), pltpu.VMEM((hp, t, 2 * t), _F32),
                        pltpu.VMEM((hp, 1, 2 * t), _F32), pltpu.VMEM((hp, 1, 2 * t), _F32)],
        compiler_params=_params("parallel", "parallel", "arbitrary"),
        name="attn",
    )(qt, k, vt, bias, lam, sg)


def _o_kernel(x_ref, a_ref, wo_ref, g_ref, gate_ref, o_ref):
    y = _dot(a_ref[0], wo_ref[...])
    o_ref[0] = x_ref[0] + gate_ref[0] * (_rms(y) * g_ref[...])


def _oproj(x, a, wo, g, gate):
    bsz, seq, d = x.shape
    tm = ROW_TILE
    tile = pl.BlockSpec((1, tm, d), lambda b, i: (b, i, 0))
    return pl.pallas_call(
        _o_kernel,
        grid=(bsz, seq // tm),
        in_specs=[tile, tile, _const_spec((d, d)), _const_spec((1, d)),
                  pl.BlockSpec((1, 1, d), lambda b, i: (b, 0, 0))],
        out_specs=tile,
        out_shape=jax.ShapeDtypeStruct((bsz, seq, d), _F32),
        compiler_params=_params("parallel", "parallel"),
        name="oproj",
    )(x, a, wo, g, gate)


def kernel(x, c, mod_w, mod_b, norm_g, cm_w1, cm_b1, cm_dw, cm_dwb, cm_ln_g, cm_ln_b, cm_w2, cm_b2, kv_norm_g,
           w_k, w_v, w_q, lam, subln_g, w_o, rel_bias, ffn_w_in, ffn_dw, ffn_dwb, ffn_w_out):
    depth = mod_w.shape[0]
    n_conv = cm_w1.shape[0]
    bsz, seq, d = x.shape

    mod = _modulation(c, mod_w, mod_b)

    def row(v):
        return v.reshape(1, -1)

    kk = vt = bias = None
    for l in range(depth):
        sh_m, sc_m, g_m, sh_f, sc_f, g_f = [m.reshape(bsz, 1, d) for m in jnp.split(mod[l], 6, axis=-1)]
        g0, g1, g2, g3 = [row(norm_g[l, k]) for k in range(4)]
        if l < n_conv:
            u = _conv_a(x, g0, sc_m, sh_m, cm_w1[l].astype(_BF16), row(cm_b1[l]))
            x = _conv_b(x, u, cm_dw[l], row(cm_dwb[l]), row(cm_ln_g[l]), row(cm_ln_b[l]),
                        cm_w2[l].astype(_BF16), row(cm_b2[l]), g1, g_m)
        else:
            if l == n_conv:
                kk, vt = _kv(x, row(kv_norm_g), w_k.astype(_BF16), w_v.T.astype(_BF16))
                bias = _bias_tiles(rel_bias)
            j = l - n_conv
            lambda_init = 0.8 - 0.6 * math.exp(-0.3 * l)
            qt = _qproj(x, g0, sc_m, sh_m, w_q[j].T.astype(_BF16))
            a = _attention(qt, kk, vt, bias, lam[j], subln_g[j].reshape(-1, 1), lambda_init)
            x = _oproj(x, a, w_o[j].astype(_BF16), g1, g_m)
        x = _ffn(x, g2, sc_f, sh_f, ffn_w_in[l].astype(_BF16), ffn_dw[l], row(ffn_dwb[l]),
                 ffn_w_out[l].astype(_BF16), g3, g_f)
    return x
```

```python
import functools
import math

import jax
import jax.numpy as jnp
from jax import lax
from jax.experimental import pallas as pl
from jax.experimental.pallas import tpu as pltpu

N_HEADS = 8
HEAD_DIM = 64
V_HEAD_DIM = 128
CONV_WIDTH = 31
FFN_CONV_WIDTH = 3
N_BUCKETS = 32
MAX_DISTANCE = 128
MAX_EXACT = 16
EPS = 1e-6

ROW_TILE = 512
ATTN_TILE = 256
ATTN_HEADS_PER_STEP = 8
ONES_ROWS = 16
V_ROWS = V_HEAD_DIM + ONES_ROWS
LOG2E = math.log2(math.e)
CONV_HALO = 32
CONV_ROWS = 64
LANES = 128
FFN_CHUNK = 256
FFN_HALO = 8
FFN_SUBTILES = 1
MASK_VALUE = -1e30
VMEM_LIMIT = 56 * 1024 * 1024

_BF16 = jnp.bfloat16
_F32 = jnp.float32


def _bucket_of_distance(n):
    if n < MAX_EXACT:
        return n
    large = MAX_EXACT + int(math.log(n / MAX_EXACT) / math.log(MAX_DISTANCE / MAX_EXACT) * (N_BUCKETS - MAX_EXACT))
    return min(large, N_BUCKETS - 1)


def _bucket_segments():
    segs = []
    for n in range(MAX_DISTANCE):
        b = _bucket_of_distance(n)
        if not segs or segs[-1][1] != b:
            segs.append((n, b))
    assert all(_bucket_of_distance(n) == N_BUCKETS - 1 for n in range(segs[-1][0], 4 * MAX_DISTANCE))
    return segs


def _rms(x):
    return x * lax.rsqrt(jnp.mean(x * x, axis=-1, keepdims=True) + EPS)


def _dot(a, b):
    return jnp.dot(a, b, preferred_element_type=_F32)


def _params(*sem):
    return pltpu.CompilerParams(dimension_semantics=sem, vmem_limit_bytes=VMEM_LIMIT)


def _const_spec(shape):
    nd = len(shape)
    return pl.BlockSpec(shape, lambda *_: (0,) * nd)


def _mod_kernel(c_ref, w_ref, b_ref, o_ref):
    c = c_ref[...]
    c_act = c * jax.nn.sigmoid(c)
    o_ref[0] = _dot(c_act, w_ref[0]) + b_ref[0]


def _modulation(c, mod_w, mod_b):
    depth, d, n = mod_w.shape
    bsz = c.shape[0]
    tn = 1024
    return pl.pallas_call(
        _mod_kernel,
        grid=(depth, n // tn),
        in_specs=[
            pl.BlockSpec((bsz, d), lambda l, j: (0, 0)),
            pl.BlockSpec((1, d, tn), lambda l, j: (l, 0, j)),
            pl.BlockSpec((1, 1, tn), lambda l, j: (l, 0, j)),
        ],
        out_specs=pl.BlockSpec((1, bsz, tn), lambda l, j: (l, 0, j)),
        out_shape=jax.ShapeDtypeStruct((depth, bsz, n), _F32),
        compiler_params=_params("arbitrary", "arbitrary"),
        name="mod",
    )(c, mod_w, mod_b.reshape(depth, 1, n))


def _conv_a_kernel(x_ref, g_ref, sc_ref, sh_ref, w1_ref, b1_ref, u_ref):
    d = x_ref.shape[-1]
    h = _rms(x_ref[0]) * g_ref[...] * (1.0 + sc_ref[0]) + sh_ref[0]
    hb = h.astype(_BF16)
    a = _dot(hb, w1_ref[:, :d]) + b1_ref[:, :d]
    gt = _dot(hb, w1_ref[:, d:]) + b1_ref[:, d:]
    u_ref[0] = a * jax.nn.sigmoid(gt)


def _conv_a(x, g, sc, sh, w1, b1):
    bsz, seq, d = x.shape
    tm = ROW_TILE
    vec = pl.BlockSpec((1, 1, d), lambda b, i: (b, 0, 0))
    return pl.pallas_call(
        _conv_a_kernel,
        grid=(bsz, seq // tm),
        in_specs=[
            pl.BlockSpec((1, tm, d), lambda b, i: (b, i, 0)),
            _const_spec((1, d)), vec, vec,
            _const_spec((d, 2 * d)), _const_spec((1, 2 * d)),
        ],
        out_specs=pl.BlockSpec((1, tm, d), lambda b, i: (b, i, 0)),
        out_shape=jax.ShapeDtypeStruct((bsz, seq, d), _F32),
        compiler_params=_params("parallel", "parallel"),
        name="conv_a",
    )(x, g, sc, sh, w1, b1)


def _conv_b_kernel(x_ref, ucur_ref, uprev_ref, dw_ref, dwb_ref, lng_ref, lnb_ref, w2_ref, b2_ref,
                   g_ref, gate_ref, o_ref, ubuf, cbuf):
    tm, d = ucur_ref.shape[1], ucur_ref.shape[2]
    i = pl.program_id(1)
    for c in range(d // LANES):
        lanes = slice(c * LANES, (c + 1) * LANES)
        prev = uprev_ref[0, :, lanes]
        ubuf[c, 0:CONV_HALO, :] = jnp.where(i > 0, prev, jnp.zeros_like(prev))
        ubuf[c, CONV_HALO:, :] = ucur_ref[0, :, lanes]

    first_shift = CONV_HALO - (CONV_WIDTH - 1)

    def row_block(r, carry):
        r0 = pl.multiple_of(r * CONV_ROWS, CONV_ROWS)
        for c in range(d // LANES):
            lanes = slice(c * LANES, (c + 1) * LANES)
            acc = jnp.broadcast_to(dwb_ref[:, lanes], (CONV_ROWS, LANES))
            for j in range(CONV_WIDTH):
                acc = acc + ubuf[c, pl.ds(r0 + first_shift + j, CONV_ROWS), :] * dw_ref[j:j + 1, lanes]
            cbuf[pl.ds(r0, CONV_ROWS), lanes] = acc
        return carry

    lax.fori_loop(0, tm // CONV_ROWS, row_block, 0)

    cv = cbuf[...]
    mu = jnp.mean(cv, axis=-1, keepdims=True)
    cc = cv - mu
    var = jnp.mean(cc * cc, axis=-1, keepdims=True)
    z = cc * lax.rsqrt(var + EPS) * lng_ref[...] + lnb_ref[...]
    z = z * jax.nn.sigmoid(z)
    y = _dot(z.astype(_BF16), w2_ref[...]) + b2_ref[...]
    o_ref[0] = x_ref[0] + gate_ref[0] * (_rms(y) * g_ref[...])


def _conv_b(x, u, dw, dwb, lng, lnb, w2, b2, g, gate):
    bsz, seq, d = x.shape
    tm = ROW_TILE
    ratio = tm // CONV_HALO
    vec = pl.BlockSpec((1, 1, d), lambda b, i: (b, 0, 0))
    tile = pl.BlockSpec((1, tm, d), lambda b, i: (b, i, 0))
    return pl.pallas_call(
        _conv_b_kernel,
        grid=(bsz, seq // tm),
        in_specs=[
            tile, tile,
            pl.BlockSpec((1, CONV_HALO, d), lambda b, i: (b, jnp.maximum(i * ratio - 1, 0), 0)),
            _const_spec((CONV_WIDTH, d)), _const_spec((1, d)), _const_spec((1, d)), _const_spec((1, d)),
            _const_spec((d, d)), _const_spec((1, d)), _const_spec((1, d)), vec,
        ],
        out_specs=tile,
        out_shape=jax.ShapeDtypeStruct((bsz, seq, d), _F32),
        scratch_shapes=[pltpu.VMEM((d // LANES, tm + CONV_HALO, LANES), _F32), pltpu.VMEM((tm, d), _F32)],
        compiler_params=_params("parallel", "parallel"),
        name="conv_b",
    )(x, u, u, dw, dwb, lng, lnb, w2, b2, g, gate)


def _ffn_kernel(x_ref, g_ref, sc_ref, sh_ref, win_ref, dw_ref, dwb_ref, wout_ref, g3_ref, gate_ref, o_ref,
                hbuf, ubuf, carry, pbuf):
    tm = ROW_TILE
    f = wout_ref.shape[0]
    i = pl.program_id(1)
    slabs = FFN_CHUNK // LANES

    @pl.when(i == 0)
    def _():
        carry[...] = jnp.zeros_like(carry)

    def sub_tile(rows, slab_set):
        h = _rms(x_ref[0, rows, :]) * g_ref[...] * (1.0 + sc_ref[0]) + sh_ref[0]
        hbuf[rows, :] = h.astype(_BF16)

        def up_project(col0, slab0):
            u = _dot(hbuf[rows, :], win_ref[:, col0:col0 + FFN_CHUNK])
            for k in range(slabs):
                cols = slice(col0 + k * LANES, col0 + (k + 1) * LANES)
                uk = u[:, k * LANES:(k + 1) * LANES]
                ubuf[slab0 + k, 0:FFN_HALO, :] = carry[:, cols]
                ubuf[slab0 + k, FFN_HALO:, :] = uk
                carry[:, cols] = uk[tm - FFN_HALO:, :]

        def conv3(col0, slab):
            cols = slice(col0, col0 + LANES)
            out = dwb_ref[:, cols]
            for j in range(FFN_CONV_WIDTH):
                shift = FFN_CONV_WIDTH - 1 - j
                out = out + ubuf[slab, pl.ds(FFN_HALO - shift, tm), :] * dw_ref[j:j + 1, cols]
            return out

        for idx, c0 in enumerate(range(0, f, FFN_CHUNK)):
            base = slab_set + (idx % 2) * 2 * slabs
            up_project(c0, base)
            up_project(f + c0, base + slabs)
            for k in range(slabs):
                gk = conv3(c0 + k * LANES, base + k)
                vk = conv3(f + c0 + k * LANES, base + slabs + k)
                pbuf[rows, c0 + k * LANES:c0 + (k + 1) * LANES] = (gk * jax.nn.sigmoid(gk) * vk).astype(_BF16)

        y = _dot(pbuf[rows, :], wout_ref[...])
        o_ref[0, rows, :] = x_ref[0, rows, :] + gate_ref[0] * (_rms(y) * g3_ref[...])

    for sub in range(FFN_SUBTILES):
        sub_tile(slice(sub * tm, (sub + 1) * tm), sub * 4 * slabs)


def _ffn(x, g, sc, sh, w_in, dw, dwb, w_out, g3, gate):
    bsz, seq, d = x.shape
    f = w_out.shape[0]
    tm = ROW_TILE * FFN_SUBTILES
    vec = pl.BlockSpec((1, 1, d), lambda b, i: (b, 0, 0))
    tile = pl.BlockSpec((1, tm, d), lambda b, i: (b, i, 0))
    single = dict(pipeline_mode=pl.Buffered(1))
    return pl.pallas_call(
        _ffn_kernel,
        grid=(bsz, seq // tm),
        in_specs=[
            tile, _const_spec((1, d)), vec, vec,
            pl.BlockSpec((d, 2 * f), lambda b, i: (0, 0), **single),
            _const_spec((FFN_CONV_WIDTH, 2 * f)), _const_spec((1, 2 * f)),
            pl.BlockSpec((f, d), lambda b, i: (0, 0), **single),
            _const_spec((1, d)), vec,
        ],
        out_specs=tile,
        out_shape=jax.ShapeDtypeStruct((bsz, seq, d), _F32),
        scratch_shapes=[
            pltpu.VMEM((tm, d), _BF16),
            pltpu.VMEM((FFN_SUBTILES * 4 * (FFN_CHUNK // LANES), ROW_TILE + FFN_HALO, LANES), _F32),
            pltpu.VMEM((FFN_HALO, 2 * f), _F32),
            pltpu.VMEM((tm, f), _BF16),
        ],
        compiler_params=_params("arbitrary", "arbitrary"),
        name="ffn",
    )(x, g, sc, sh, w_in, dw, dwb, w_out, g3, gate)


def _kv_kernel(x_ref, g_ref, wk_ref, wvt_ref, k_ref, vt_ref):
    tm = x_ref.shape[1]
    tk = vt_ref.shape[-1]
    hb = (_rms(x_ref[0]) * g_ref[...]).astype(_BF16)
    k = _dot(hb, wk_ref[...])
    vt = lax.dot_general(wvt_ref[...], hb, (((1,), (1,)), ((), ())), preferred_element_type=_F32)
    for hd in range(N_HEADS):
        rows = slice(hd * V_HEAD_DIM, (hd + 1) * V_HEAD_DIM)
        k_ref[0, hd] = k[:, rows].astype(_BF16)
        for c in range(tm // tk):
            vt_ref[0, hd, c, :V_HEAD_DIM, :] = vt[rows, c * tk:(c + 1) * tk].astype(_BF16)
            vt_ref[0, hd, c, V_HEAD_DIM:, :] = jnp.ones((ONES_ROWS, tk), _BF16)


def _kv(x, g, wk, wvt):
    bsz, seq, d = x.shape
    tm, tk = ROW_TILE, ATTN_TILE
    return pl.pallas_call(
        _kv_kernel,
        grid=(bsz, seq // tm),
        in_specs=[
            pl.BlockSpec((1, tm, d), lambda b, i: (b, i, 0)),
            _const_spec((1, d)), _const_spec((d, d)), _const_spec((d, d)),
        ],
        out_specs=[
            pl.BlockSpec((1, N_HEADS, tm, V_HEAD_DIM), lambda b, i: (b, 0, i, 0)),
            pl.BlockSpec((1, N_HEADS, tm // tk, V_ROWS, tk), lambda b, i: (b, 0, i, 0, 0)),
        ],
        out_shape=[
            jax.ShapeDtypeStruct((bsz, N_HEADS, seq, V_HEAD_DIM), _BF16),
            jax.ShapeDtypeStruct((bsz, N_HEADS, seq // tk, V_ROWS, tk), _BF16),
        ],
        compiler_params=_params("parallel", "parallel"),
        name="kv",
    )(x, g, wk, wvt)


def _q_kernel(x_ref, g_ref, sc_ref, sh_ref, wqt_ref, q_ref):
    tm = x_ref.shape[1]
    tq = q_ref.shape[-1] // 2
    h = _rms(x_ref[0]) * g_ref[...] * (1.0 + sc_ref[0]) + sh_ref[0]
    qt = lax.dot_general(wqt_ref[...], h.astype(_BF16), (((1,), (1,)), ((), ())),
                         preferred_element_type=_F32) * (HEAD_DIM ** -0.5 * LOG2E)
    first = lax.broadcasted_iota(jnp.int32, (V_HEAD_DIM, tq), 0) < HEAD_DIM
    for hd in range(N_HEADS):
        for c in range(tm // tq):
            qh = qt[hd * V_HEAD_DIM:(hd + 1) * V_HEAD_DIM, c * tq:(c + 1) * tq]
            q_ref[0, hd, c, :, :tq] = jnp.where(first, qh, 0.0).astype(_BF16)
            q_ref[0, hd, c, :, tq:] = jnp.where(first, 0.0, qh).astype(_BF16)


def _qproj(x, g, sc, sh, wqt):
    bsz, seq, d = x.shape
    tm, tq = ROW_TILE, ATTN_TILE
    vec = pl.BlockSpec((1, 1, d), lambda b, i: (b, 0, 0))
    return pl.pallas_call(
        _q_kernel,
        grid=(bsz, seq // tm),
        in_specs=[pl.BlockSpec((1, tm, d), lambda b, i: (b, i, 0)), _const_spec((1, d)), vec, vec,
                  _const_spec((d, d))],
        out_specs=pl.BlockSpec((1, N_HEADS, tm // tq, V_HEAD_DIM, 2 * tq), lambda b, i: (b, 0, i, 0, 0)),
        out_shape=jax.ShapeDtypeStruct((bsz, N_HEADS, seq // tq, V_HEAD_DIM, 2 * tq), _BF16),
        compiler_params=_params("parallel", "parallel"),
        name="qproj",
    )(x, g, sc, sh, wqt)


def _bias_kernel(rb_ref, o_ref):
    hd = pl.program_id(0)
    t = o_ref.shape[-1]
    key = lax.broadcasted_iota(jnp.int32, (t, t), 0)
    qry = lax.broadcasted_iota(jnp.int32, (t, t), 1)
    segs = _bucket_segments()
    far = rb_ref[N_BUCKETS - 1, hd]
    for which in range(2):
        rel = qry - key + which * t
        val = jnp.full((t, t), 0.0, _F32)
        for k in range(len(segs) - 2, -1, -1):
            val = jnp.where(rel < segs[k + 1][0], (rb_ref[segs[k][1], hd] - far) * LOG2E, val)
        o_ref[0, which] = jnp.where(rel >= 0, val, MASK_VALUE)


def _bias_tiles(rel_bias):
    t = ATTN_TILE
    return pl.pallas_call(
        _bias_kernel,
        grid=(N_HEADS,),
        in_specs=[pl.BlockSpec(memory_space=pltpu.SMEM)],
        out_specs=pl.BlockSpec((1, 2, t, t), lambda h: (h, 0, 0, 0)),
        out_shape=jax.ShapeDtypeStruct((N_HEADS, 2, t, t), _F32),
        compiler_params=_params("arbitrary"),
        name="bias",
    )(rel_bias)


def _attn_kernel(q_ref, k_ref, vt_ref, bias_ref, lam_ref, sg_ref, o_ref, m_sc, acc_sc, s_a, s_b, x_a, x_b,
                 *, lambda_init):
    t = ATTN_TILE
    dv = V_HEAD_DIM
    heads = q_ref.shape[1]
    i = pl.program_id(2)
    m_sc[...] = jnp.full_like(m_sc, MASK_VALUE)
    acc_sc[...] = jnp.zeros_like(acc_sc)

    def scores_into(buf, j, hh):
        s_buf, x_buf = buf
        rows = pl.ds(pl.multiple_of(j * t, t), t)
        s = _dot(k_ref[0, hh, rows, :], q_ref[0, hh, 0])
        s_buf[hh] = s
        x_buf[hh] = jnp.max(s, axis=0, keepdims=True)

    def step(j, which, cur, nxt):
        if nxt is not None:
            scores_into(nxt, j + 1, 0)
        for hh in range(heads):
            if nxt is not None and hh + 1 < heads:
                scores_into(nxt, j + 1, hh + 1)
            s = cur[0][hh]
            if which is None:
                m_cur = cur[1][hh]
            else:
                bias = bias_ref[hh, which]
                s = jnp.concatenate([s[:, :t] + bias, s[:, t:] + bias], axis=1)
                m_cur = jnp.max(s, axis=0, keepdims=True)
            m_prev = m_sc[hh]
            m_new = jnp.maximum(m_prev, m_cur)
            alpha = jnp.exp2(m_prev - m_new)
            p = jnp.exp2(s - m_new)
            acc_sc[hh] = alpha * acc_sc[hh] + _dot(vt_ref[0, hh, j], p.astype(_BF16))
            m_sc[hh] = m_new

    buf_a, buf_b = (s_a, x_a), (s_b, x_b)

    def far_pair(jj, carry):
        ja = 2 * jj
        for hh in range(heads):
            m_prev = m_sc[hh]
            m_new = jnp.maximum(m_prev, jnp.maximum(x_a[hh], x_b[hh]))
            alpha = jnp.exp2(m_prev - m_new)
            pa = jnp.exp2(s_a[hh] - m_new).astype(_BF16)
            scores_into(buf_a, ja + 2, hh)
            pb = jnp.exp2(s_b[hh] - m_new).astype(_BF16)
            scores_into(buf_b, ja + 3, hh)
            vt2 = jnp.concatenate([vt_ref[0, hh, ja], vt_ref[0, hh, ja + 1]], axis=1)
            pv = _dot(vt2, jnp.concatenate([pa, pb], axis=0))
            acc_sc[hh] = alpha * acc_sc[hh] + pv
            m_sc[hh] = m_new
        return carry

    for hh in range(heads):
        scores_into(buf_a, 0, hh)

    @pl.when(i >= 1)
    def _():
        for hh in range(heads):
            scores_into(buf_b, 1, hh)

    lax.fori_loop(0, jnp.maximum((i - 1) // 2, 0), far_pair, 0)

    @pl.when(i % 2 == 0)
    def _():
        @pl.when(i >= 2)
        def _():
            step(i - 2, None, buf_a, None)
            step(i - 1, 1, buf_b, buf_a)

        step(i, 0, buf_a, None)

    @pl.when(i % 2 == 1)
    def _():
        step(i - 1, 1, buf_a, None)
        step(i, 0, buf_b, None)

    lv = lam_ref[...]
    lam = (jnp.exp(jnp.sum(lv[0:1] * lv[1:2], axis=-1, keepdims=True))
           - jnp.exp(jnp.sum(lv[2:3] * lv[3:4], axis=-1, keepdims=True)) + lambda_init)
    for hh in range(heads):
        acc = acc_sc[hh]
        o = acc[:dv] / acc[dv:dv + 1]
        o = o[:, :t] - lam * o[:, t:]
        o = o * lax.rsqrt(jnp.mean(o * o, axis=0, keepdims=True) + EPS) * sg_ref[...] * (1.0 - lambda_init)
        o_ref[0, :, hh * dv:(hh + 1) * dv] = o.T.astype(o_ref.dtype)


def _attention(qt, k, vt, bias, lam, sg, lambda_init):
    bsz, n_heads, seq, dv = k.shape
    t = ATTN_TILE
    nt = seq // t
    hp = ATTN_HEADS_PER_STEP
    single = dict(pipeline_mode=pl.Buffered(1))
    return pl.pallas_call(
        functools.partial(_attn_kernel, lambda_init=lambda_init),
        grid=(bsz, n_heads // hp, nt),
        in_specs=[
            pl.BlockSpec((1, hp, 1, dv, 2 * t), lambda b, h, i: (b, h, i, 0, 0)),
            pl.BlockSpec((1, hp, seq, dv), lambda b, h, i: (b, h, 0, 0), **single),
            pl.BlockSpec((1, hp, nt, V_ROWS, t), lambda b, h, i: (b, h, 0, 0, 0), **single),
            pl.BlockSpec((hp, 2, t, t), lambda b, h, i: (h, 0, 0, 0), **single),
            pl.BlockSpec((4, HEAD_DIM), lambda b, h, i: (0, 0)),
            pl.BlockSpec((dv, 1), lambda b, h, i: (0, 0)),
        ],
        out_specs=pl.BlockSpec((1, t, hp * dv), lambda b, h, i: (b, i, h)),
        out_shape=jax.ShapeDtypeStruct((bsz, seq, n_heads * dv), _BF16),
        scratch_shapes=[pltpu.VMEM((hp, 1, 2 * t), _F32), pltpu.VMEM((hp, V_ROWS, 2 * t), _F32),
                        pltpu.VMEM((hp, t, 2 * t), _F32), pltpu.VMEM((hp, t, 2 * t), _F32),
                        pltpu.VMEM((hp, 1, 2 * t), _F32), pltpu.VMEM((hp, 1, 2 * t), _F32)],
        compiler_params=_params("parallel", "parallel", "arbitrary"),
        name="attn",
    )(qt, k, vt, bias, lam, sg)


def _o_kernel(x_ref, a_ref, wo_ref, g_ref, gate_ref, o_ref):
    y = _dot(a_ref[0], wo_ref[...])
    o_ref[0] = x_ref[0] + gate_ref[0] * (_rms(y) * g_ref[...])


def _oproj(x, a, wo, g, gate):
    bsz, seq, d = x.shape
    tm = ROW_TILE
    tile = pl.BlockSpec((1, tm, d), lambda b, i: (b, i, 0))
    return pl.pallas_call(
        _o_kernel,
        grid=(bsz, seq // tm),
        in_specs=[tile, tile, _const_spec((d, d)), _const_spec((1, d)),
                  pl.BlockSpec((1, 1, d), lambda b, i: (b, 0, 0))],
        out_specs=tile,
        out_shape=jax.ShapeDtypeStruct((bsz, seq, d), _F32),
        compiler_params=_params("parallel", "parallel"),
        name="oproj",
    )(x, a, wo, g, gate)


def kernel(x, c, mod_w, mod_b, norm_g, cm_w1, cm_b1, cm_dw, cm_dwb, cm_ln_g, cm_ln_b, cm_w2, cm_b2, kv_norm_g,
           w_k, w_v, w_q, lam, subln_g, w_o, rel_bias, ffn_w_in, ffn_dw, ffn_dwb, ffn_w_out):
    depth = mod_w.shape[0]
    n_conv = cm_w1.shape[0]
    bsz, seq, d = x.shape

    mod = _modulation(c, mod_w, mod_b)

    def row(v):
        return v.reshape(1, -1)

    kk = vt = bias = None
    for l in range(depth):
        sh_m, sc_m, g_m, sh_f, sc_f, g_f = [m.reshape(bsz, 1, d) for m in jnp.split(mod[l], 6, axis=-1)]
        g0, g1, g2, g3 = [row(norm_g[l, k]) for k in range(4)]
        if l < n_conv:
            u = _conv_a(x, g0, sc_m, sh_m, cm_w1[l].astype(_BF16), row(cm_b1[l]))
            x = _conv_b(x, u, cm_dw[l], row(cm_dwb[l]), row(cm_ln_g[l]), row(cm_ln_b[l]),
                        cm_w2[l].astype(_BF16), row(cm_b2[l]), g1, g_m)
        else:
            if l == n_conv:
                kk, vt = _kv(x, row(kv_norm_g), w_k.astype(_BF16), w_v.T.astype(_BF16))
                bias = _bias_tiles(rel_bias)
            j = l - n_conv
            lambda_init = 0.8 - 0.6 * math.exp(-0.3 * l)
            qt = _qproj(x, g0, sc_m, sh_m, w_q[j].T.astype(_BF16))
            a = _attention(qt, kk, vt, bias, lam[j], subln_g[j].reshape(-1, 1), lambda_init)
            x = _oproj(x, a, w_o[j].astype(_BF16), g1, g_m)
        x = _ffn(x, g2, sc_f, sh_f, ffn_w_in[l].astype(_BF16), ffn_dw[l], row(ffn_dwb[l]),
                 ffn_w_out[l].astype(_BF16), g3, g_f)
    return x
```

```python
import functools
import math

import jax
import jax.numpy as jnp
from jax import lax
from jax.experimental import pallas as pl
from jax.experimental.pallas import tpu as pltpu

N_HEADS = 8
HEAD_DIM = 64
V_HEAD_DIM = 128
CONV_WIDTH = 31
FFN_CONV_WIDTH = 3
N_BUCKETS = 32
MAX_DISTANCE = 128
MAX_EXACT = 16
EPS = 1e-6

ROW_TILE = 512
ATTN_TILE = 256
ATTN_HEADS_PER_STEP = 8
ONES_ROWS = 16
V_ROWS = V_HEAD_DIM + ONES_ROWS
LOG2E = math.log2(math.e)
CONV_HALO = 32
CONV_ROWS = 64
LANES = 128
FFN_CHUNK = 256
FFN_HALO = 8
FFN_SUBTILES = 1
MASK_VALUE = -1e30
VMEM_LIMIT = 56 * 1024 * 1024

_BF16 = jnp.bfloat16
_F32 = jnp.float32


def _bucket_of_distance(n):
    if n < MAX_EXACT:
        return n
    large = MAX_EXACT + int(math.log(n / MAX_EXACT) / math.log(MAX_DISTANCE / MAX_EXACT) * (N_BUCKETS - MAX_EXACT))
    return min(large, N_BUCKETS - 1)


def _bucket_segments():
    segs = []
    for n in range(MAX_DISTANCE):
        b = _bucket_of_distance(n)
        if not segs or segs[-1][1] != b:
            segs.append((n, b))
    assert all(_bucket_of_distance(n) == N_BUCKETS - 1 for n in range(segs[-1][0], 4 * MAX_DISTANCE))
    return segs


def _rms(x):
    return x * lax.rsqrt(jnp.mean(x * x, axis=-1, keepdims=True) + EPS)


def _dot(a, b):
    return jnp.dot(a, b, preferred_element_type=_F32)


def _params(*sem):
    return pltpu.CompilerParams(dimension_semantics=sem, vmem_limit_bytes=VMEM_LIMIT)


def _const_spec(shape):
    nd = len(shape)
    return pl.BlockSpec(shape, lambda *_: (0,) * nd)


def _mod_kernel(c_ref, w_ref, b_ref, o_ref):
    c = c_ref[...]
    c_act = c * jax.nn.sigmoid(c)
    o_ref[0] = _dot(c_act, w_ref[0]) + b_ref[0]


def _modulation(c, mod_w, mod_b):
    depth, d, n = mod_w.shape
    bsz = c.shape[0]
    tn = 1024
    return pl.pallas_call(
        _mod_kernel,
        grid=(depth, n // tn),
        in_specs=[
            pl.BlockSpec((bsz, d), lambda l, j: (0, 0)),
            pl.BlockSpec((1, d, tn), lambda l, j: (l, 0, j)),
            pl.BlockSpec((1, 1, tn), lambda l, j: (l, 0, j)),
        ],
        out_specs=pl.BlockSpec((1, bsz, tn), lambda l, j: (l, 0, j)),
        out_shape=jax.ShapeDtypeStruct((depth, bsz, n), _F32),
        compiler_params=_params("arbitrary", "arbitrary"),
        name="mod",
    )(c, mod_w, mod_b.reshape(depth, 1, n))


def _conv_a_kernel(x_ref, g_ref, sc_ref, sh_ref, w1_ref, b1_ref, u_ref):
    d = x_ref.shape[-1]
    h = _rms(x_ref[0]) * g_ref[...] * (1.0 + sc_ref[0]) + sh_ref[0]
    hb = h.astype(_BF16)
    a = _dot(hb, w1_ref[:, :d]) + b1_ref[:, :d]
    gt = _dot(hb, w1_ref[:, d:]) + b1_ref[:, d:]
    u_ref[0] = a * jax.nn.sigmoid(gt)


def _conv_a(x, g, sc, sh, w1, b1):
    bsz, seq, d = x.shape
    tm = ROW_TILE
    vec = pl.BlockSpec((1, 1, d), lambda b, i: (b, 0, 0))
    return pl.pallas_call(
        _conv_a_kernel,
        grid=(bsz, seq // tm),
        in_specs=[
            pl.BlockSpec((1, tm, d), lambda b, i: (b, i, 0)),
            _const_spec((1, d)), vec, vec,
            _const_spec((d, 2 * d)), _const_spec((1, 2 * d)),
        ],
        out_specs=pl.BlockSpec((1, tm, d), lambda b, i: (b, i, 0)),
        out_shape=jax.ShapeDtypeStruct((bsz, seq, d), _F32),
        compiler_params=_params("parallel", "parallel"),
        name="conv_a",
    )(x, g, sc, sh, w1, b1)


def _conv_b_kernel(x_ref, ucur_ref, uprev_ref, dw_ref, dwb_ref, lng_ref, lnb_ref, w2_ref, b2_ref,
                   g_ref, gate_ref, o_ref, ubuf, cbuf):
    tm, d = ucur_ref.shape[1], ucur_ref.shape[2]
    i = pl.program_id(1)
    for c in range(d // LANES):
        lanes = slice(c * LANES, (c + 1) * LANES)
        prev = uprev_ref[0, :, lanes]
        ubuf[c, 0:CONV_HALO, :] = jnp.where(i > 0, prev, jnp.zeros_like(prev))
        ubuf[c, CONV_HALO:, :] = ucur_ref[0, :, lanes]

    first_shift = CONV_HALO - (CONV_WIDTH - 1)

    def row_block(r, carry):
        r0 = pl.multiple_of(r * CONV_ROWS, CONV_ROWS)
        for c in range(d // LANES):
            lanes = slice(c * LANES, (c + 1) * LANES)
            acc = jnp.broadcast_to(dwb_ref[:, lanes], (CONV_ROWS, LANES))
            for j in range(CONV_WIDTH):
                acc = acc + ubuf[c, pl.ds(r0 + first_shift + j, CONV_ROWS), :] * dw_ref[j:j + 1, lanes]
            cbuf[pl.ds(r0, CONV_ROWS), lanes] = acc
        return carry

    lax.fori_loop(0, tm // CONV_ROWS, row_block, 0)

    cv = cbuf[...]
    mu = jnp.mean(cv, axis=-1, keepdims=True)
    cc = cv - mu
    var = jnp.mean(cc * cc, axis=-1, keepdims=True)
    z = cc * lax.rsqrt(var + EPS) * lng_ref[...] + lnb_ref[...]
    z = z * jax.nn.sigmoid(z)
    y = _dot(z.astype(_BF16), w2_ref[...]) + b2_ref[...]
    o_ref[0] = x_ref[0] + gate_ref[0] * (_rms(y) * g_ref[...])


def _conv_b(x, u, dw, dwb, lng, lnb, w2, b2, g, gate):
    bsz, seq, d = x.shape
    tm = ROW_TILE
    ratio = tm // CONV_HALO
    vec = pl.BlockSpec((1, 1, d), lambda b, i: (b, 0, 0))
    tile = pl.BlockSpec((1, tm, d), lambda b, i: (b, i, 0))
    return pl.pallas_call(
        _conv_b_kernel,
        grid=(bsz, seq // tm),
        in_specs=[
            tile, tile,
            pl.BlockSpec((1, CONV_HALO, d), lambda b, i: (b, jnp.maximum(i * ratio - 1, 0), 0)),
            _const_spec((CONV_WIDTH, d)), _const_spec((1, d)), _const_spec((1, d)), _const_spec((1, d)),
            _const_spec((d, d)), _const_spec((1, d)), _const_spec((1, d)), vec,
        ],
        out_specs=tile,
        out_shape=jax.ShapeDtypeStruct((bsz, seq, d), _F32),
        scratch_shapes=[pltpu.VMEM((d // LANES, tm + CONV_HALO, LANES), _F32), pltpu.VMEM((tm, d), _F32)],
        compiler_params=_params("parallel", "parallel"),
        name="conv_b",
    )(x, u, u, dw, dwb, lng, lnb, w2, b2, g, gate)


def _ffn_kernel(x_ref, g_ref, sc_ref, sh_ref, win_ref, dw_ref, dwb_ref, wout_ref, g3_ref, gate_ref, o_ref,
                hbuf, ubuf, carry, pbuf):
    tm = ROW_TILE
    f = wout_ref.shape[0]
    i = pl.program_id(1)
    slabs = FFN_CHUNK // LANES

    @pl.when(i == 0)
    def _():
        carry[...] = jnp.zeros_like(carry)

    def sub_tile(rows, slab_set):
        h = _rms(x_ref[0, rows, :]) * g_ref[...] * (1.0 + sc_ref[0]) + sh_ref[0]
        hbuf[rows, :] = h.astype(_BF16)

        def up_project(col0, slab0):
            u = _dot(hbuf[rows, :], win_ref[:, col0:col0 + FFN_CHUNK])
            for k in range(slabs):
                cols = slice(col0 + k * LANES, col0 + (k + 1) * LANES)
                uk = u[:, k * LANES:(k + 1) * LANES]
                ubuf[slab0 + k, 0:FFN_HALO, :] = carry[:, cols]
                ubuf[slab0 + k, FFN_HALO:, :] = uk
                carry[:, cols] = uk[tm - FFN_HALO:, :]

        def conv3(col0, slab):
            cols = slice(col0, col0 + LANES)
            out = dwb_ref[:, cols]
            for j in range(FFN_CONV_WIDTH):
                shift = FFN_CONV_WIDTH - 1 - j
                out = out + ubuf[slab, pl.ds(FFN_HALO - shift, tm), :] * dw_ref[j:j + 1, cols]
            return out

        for idx, c0 in enumerate(range(0, f, FFN_CHUNK)):
            base = slab_set + (idx % 2) * 2 * slabs
            up_project(c0, base)
            up_project(f + c0, base + slabs)
            for k in range(slabs):
                gk = conv3(c0 + k * LANES, base + k)
                vk = conv3(f + c0 + k * LANES, base + slabs + k)
                pbuf[rows, c0 + k * LANES:c0 + (k + 1) * LANES] = (gk * jax.nn.sigmoid(gk) * vk).astype(_BF16)

        y = _dot(pbuf[rows, :], wout_ref[...])
        o_ref[0, rows, :] = x_ref[0, rows, :] + gate_ref[0] * (_rms(y) * g3_ref[...])

    for sub in range(FFN_SUBTILES):
        sub_tile(slice(sub * tm, (sub + 1) * tm), sub * 4 * slabs)


def _ffn(x, g, sc, sh, w_in, dw, dwb, w_out, g3, gate):
    bsz, seq, d = x.shape
    f = w_out.shape[0]
    tm = ROW_TILE * FFN_SUBTILES
    vec = pl.BlockSpec((1, 1, d), lambda b, i: (b, 0, 0))
    tile = pl.BlockSpec((1, tm, d), lambda b, i: (b, i, 0))
    single = dict(pipeline_mode=pl.Buffered(1))
    return pl.pallas_call(
        _ffn_kernel,
        grid=(bsz, seq // tm),
        in_specs=[
            tile, _const_spec((1, d)), vec, vec,
            pl.BlockSpec((d, 2 * f), lambda b, i: (0, 0), **single),
            _const_spec((FFN_CONV_WIDTH, 2 * f)), _const_spec((1, 2 * f)),
            pl.BlockSpec((f, d), lambda b, i: (0, 0), **single),
            _const_spec((1, d)), vec,
        ],
        out_specs=tile,
        out_shape=jax.ShapeDtypeStruct((bsz, seq, d), _F32),
        scratch_shapes=[
            pltpu.VMEM((tm, d), _BF16),
            pltpu.VMEM((FFN_SUBTILES * 4 * (FFN_CHUNK // LANES), ROW_TILE + FFN_HALO, LANES), _F32),
            pltpu.VMEM((FFN_HALO, 2 * f), _F32),
            pltpu.VMEM((tm, f), _BF16),
        ],
        compiler_params=_params("arbitrary", "arbitrary"),
        name="ffn",
    )(x, g, sc, sh, w_in, dw, dwb, w_out, g3, gate)


def _kv_kernel(x_ref, g_ref, wk_ref, wvt_ref, k_ref, vt_ref):
    tm = x_ref.shape[1]
    tk = vt_ref.shape[-1]
    hb = (_rms(x_ref[0]) * g_ref[...]).astype(_BF16)
    k = _dot(hb, wk_ref[...])
    vt = lax.dot_general(wvt_ref[...], hb, (((1,), (1,)), ((), ())), preferred_element_type=_F32)
    for hd in range(N_HEADS):
        rows = slice(hd * V_HEAD_DIM, (hd + 1) * V_HEAD_DIM)
        k_ref[0, hd] = k[:, rows].astype(_BF16)
        for c in range(tm // tk):
            vt_ref[0, hd, c, :V_HEAD_DIM, :] = vt[rows, c * tk:(c + 1) * tk].astype(_BF16)
            vt_ref[0, hd, c, V_HEAD_DIM:, :] = jnp.ones((ONES_ROWS, tk), _BF16)


def _kv(x, g, wk, wvt):
    bsz, seq, d = x.shape
    tm, tk = ROW_TILE, ATTN_TILE
    return pl.pallas_call(
        _kv_kernel,
        grid=(bsz, seq // tm),
        in_specs=[
            pl.BlockSpec((1, tm, d), lambda b, i: (b, i, 0)),
            _const_spec((1, d)), _const_spec((d, d)), _const_spec((d, d)),
        ],
        out_specs=[
            pl.BlockSpec((1, N_HEADS, tm, V_HEAD_DIM), lambda b, i: (b, 0, i, 0)),
            pl.BlockSpec((1, N_HEADS, tm // tk, V_ROWS, tk), lambda b, i: (b, 0, i, 0, 0)),
        ],
        out_shape=[
            jax.ShapeDtypeStruct((bsz, N_HEADS, seq, V_HEAD_DIM), _BF16),
            jax.ShapeDtypeStruct((bsz, N_HEADS, seq // tk, V_ROWS, tk), _BF16),
        ],
        compiler_params=_params("parallel", "parallel"),
        name="kv",
    )(x, g, wk, wvt)


def _q_kernel(x_ref, g_ref, sc_ref, sh_ref, wqt_ref, q_ref):
    tm = x_ref.shape[1]
    tq = q_ref.shape[-1] // 2
    h = _rms(x_ref[0]) * g_ref[...] * (1.0 + sc_ref[0]) + sh_ref[0]
    qt = lax.dot_general(wqt_ref[...], h.astype(_BF16), (((1,), (1,)), ((), ())),
                         preferred_element_type=_F32) * (HEAD_DIM ** -0.5 * LOG2E)
    first = lax.broadcasted_iota(jnp.int32, (V_HEAD_DIM, tq), 0) < HEAD_DIM
    for hd in range(N_HEADS):
        for c in range(tm // tq):
            qh = qt[hd * V_HEAD_DIM:(hd + 1) * V_HEAD_DIM, c * tq:(c + 1) * tq]
            q_ref[0, hd, c, :, :tq] = jnp.where(first, qh, 0.0).astype(_BF16)
            q_ref[0, hd, c, :, tq:] = jnp.where(first, 0.0, qh).astype(_BF16)


def _qproj(x, g, sc, sh, wqt):
    bsz, seq, d = x.shape
    tm, tq = ROW_TILE, ATTN_TILE
    vec = pl.BlockSpec((1, 1, d), lambda b, i: (b, 0, 0))
    return pl.pallas_call(
        _q_kernel,
        grid=(bsz, seq // tm),
        in_specs=[pl.BlockSpec((1, tm, d), lambda b, i: (b, i, 0)), _const_spec((1, d)), vec, vec,
                  _const_spec((d, d))],
        out_specs=pl.BlockSpec((1, N_HEADS, tm // tq, V_HEAD_DIM, 2 * tq), lambda b, i: (b, 0, i, 0, 0)),
        out_shape=jax.ShapeDtypeStruct((bsz, N_HEADS, seq // tq, V_HEAD_DIM, 2 * tq), _BF16),
        compiler_params=_params("parallel", "parallel"),
        name="qproj",
    )(x, g, sc, sh, wqt)


def _bias_kernel(rb_ref, o_ref):
    hd = pl.program_id(0)
    t = o_ref.shape[-1]
    key = lax.broadcasted_iota(jnp.int32, (t, t), 0)
    qry = lax.broadcasted_iota(jnp.int32, (t, t), 1)
    segs = _bucket_segments()
    far = rb_ref[N_BUCKETS - 1, hd]
    for which in range(2):
        rel = qry - key + which * t
        val = jnp.full((t, t), 0.0, _F32)
        for k in range(len(segs) - 2, -1, -1):
            val = jnp.where(rel < segs[k + 1][0], (rb_ref[segs[k][1], hd] - far) * LOG2E, val)
        o_ref[0, which] = jnp.where(rel >= 0, val, MASK_VALUE)


def _bias_tiles(rel_bias):
    t = ATTN_TILE
    return pl.pallas_call(
        _bias_kernel,
        grid=(N_HEADS,),
        in_specs=[pl.BlockSpec(memory_space=pltpu.SMEM)],
        out_specs=pl.BlockSpec((1, 2, t, t), lambda h: (h, 0, 0, 0)),
        out_shape=jax.ShapeDtypeStruct((N_HEADS, 2, t, t), _F32),
        compiler_params=_params("arbitrary"),
        name="bias",
    )(rel_bias)


def _attn_kernel(q_ref, qn_ref, k_ref, vt_ref, bias_ref, lam_ref, sg_ref, o_ref, m_sc, acc_sc, s_a, s_b, x_a, x_b,
                 *, lambda_init):
    t = ATTN_TILE
    dv = V_HEAD_DIM
    heads = q_ref.shape[1]
    i = pl.program_id(2)
    m_sc[...] = jnp.full_like(m_sc, MASK_VALUE)
    acc_sc[...] = jnp.zeros_like(acc_sc)

    def scores_into(buf, j, hh, q_src=q_ref):
        s_buf, x_buf = buf
        rows = pl.ds(pl.multiple_of(j * t, t), t)
        s = _dot(k_ref[0, hh, rows, :], q_src[0, hh, 0])
        s_buf[hh] = s
        x_buf[hh] = jnp.max(s, axis=0, keepdims=True)

    def step(j, which, cur, nxt):
        if nxt is not None:
            scores_into(nxt, j + 1, 0)
        for hh in range(heads):
            if nxt is not None and hh + 1 < heads:
                scores_into(nxt, j + 1, hh + 1)
            s = cur[0][hh]
            if which is None:
                m_cur = cur[1][hh]
            else:
                bias = bias_ref[hh, which]
                s = jnp.concatenate([s[:, :t] + bias, s[:, t:] + bias], axis=1)
                m_cur = jnp.max(s, axis=0, keepdims=True)
            m_prev = m_sc[hh]
            m_new = jnp.maximum(m_prev, m_cur)
            alpha = jnp.exp2(m_prev - m_new)
            p = jnp.exp2(s - m_new)
            acc_sc[hh] = alpha * acc_sc[hh] + _dot(vt_ref[0, hh, j], p.astype(_BF16))
            m_sc[hh] = m_new

    buf_a, buf_b = (s_a, x_a), (s_b, x_b)

    def far_pair(jj, carry):
        ja = 2 * jj
        for hh in range(heads):
            m_prev = m_sc[hh]
            m_new = jnp.maximum(m_prev, jnp.maximum(x_a[hh], x_b[hh]))
            alpha = jnp.exp2(m_prev - m_new)
            pa = jnp.exp2(s_a[hh] - m_new).astype(_BF16)
            scores_into(buf_a, ja + 2, hh)
            pb = jnp.exp2(s_b[hh] - m_new).astype(_BF16)
            scores_into(buf_b, ja + 3, hh)
            vt2 = jnp.concatenate([vt_ref[0, hh, ja], vt_ref[0, hh, ja + 1]], axis=1)
            pv = _dot(vt2, jnp.concatenate([pa, pb], axis=0))
            acc_sc[hh] = alpha * acc_sc[hh] + pv
            m_sc[hh] = m_new
        return carry

    @pl.when(i == 0)
    def _():
        for hh in range(heads):
            scores_into(buf_a, 0, hh)

    lax.fori_loop(0, jnp.maximum((i - 1) // 2, 0), far_pair, 0)

    @pl.when(i % 2 == 0)
    def _():
        @pl.when(i >= 2)
        def _():
            step(i - 2, None, buf_a, None)
            step(i - 1, 1, buf_b, buf_a)

        step(i, 0, buf_a, None)

    @pl.when(i % 2 == 1)
    def _():
        step(i - 1, 1, buf_a, None)
        step(i, 0, buf_b, None)

    for hh in range(heads):
        scores_into(buf_a, 0, hh, qn_ref)
        scores_into(buf_b, 1, hh, qn_ref)

    lv = lam_ref[...]
    lam = (jnp.exp(jnp.sum(lv[0:1] * lv[1:2], axis=-1, keepdims=True))
           - jnp.exp(jnp.sum(lv[2:3] * lv[3:4], axis=-1, keepdims=True)) + lambda_init)
    for hh in range(heads):
        acc = acc_sc[hh]
        o = acc[:dv] / acc[dv:dv + 1]
        o = o[:, :t] - lam * o[:, t:]
        o = o * lax.rsqrt(jnp.mean(o * o, axis=0, keepdims=True) + EPS) * sg_ref[...] * (1.0 - lambda_init)
        o_ref[0, :, hh * dv:(hh + 1) * dv] = o.T.astype(o_ref.dtype)


def _attention(qt, k, vt, bias, lam, sg, lambda_init):
    bsz, n_heads, seq, dv = k.shape
    t = ATTN_TILE
    nt = seq // t
    hp = ATTN_HEADS_PER_STEP
    single = dict(pipeline_mode=pl.Buffered(1))
    return pl.pallas_call(
        functools.partial(_attn_kernel, lambda_init=lambda_init),
        grid=(bsz, n_heads // hp, nt),
        in_specs=[
            pl.BlockSpec((1, hp, 1, dv, 2 * t), lambda b, h, i: (b, h, i, 0, 0)),
            pl.BlockSpec((1, hp, 1, dv, 2 * t), lambda b, h, i: (b, h, jnp.minimum(i + 1, nt - 1), 0, 0)),
            pl.BlockSpec((1, hp, seq, dv), lambda b, h, i: (b, h, 0, 0), **single),
            pl.BlockSpec((1, hp, nt, V_ROWS, t), lambda b, h, i: (b, h, 0, 0, 0), **single),
            pl.BlockSpec((hp, 2, t, t), lambda b, h, i: (h, 0, 0, 0), **single),
            pl.BlockSpec((4, HEAD_DIM), lambda b, h, i: (0, 0)),
            pl.BlockSpec((dv, 1), lambda b, h, i: (0, 0)),
        ],
        out_specs=pl.BlockSpec((1, t, hp * dv), lambda b, h, i: (b, i, h)),
        out_shape=jax.ShapeDtypeStruct((bsz, seq, n_heads * dv), _BF16),
        scratch_shapes=[pltpu.VMEM((hp, 1, 2 * t), _F32), pltpu.VMEM((hp, V_ROWS, 2 * t), _F32),
                        pltpu.VMEM((hp, t, 2 * t), _F32), pltpu.VMEM((hp, t, 2 * t), _F32),
                        pltpu.VMEM((hp, 1, 2 * t), _F32), pltpu.VMEM((hp, 1, 2 * t), _F32)],
        compiler_params=_params("parallel", "parallel", "arbitrary"),
        name="attn",
    )(qt, qt, k, vt, bias, lam, sg)


def _o_kernel(x_ref, a_ref, wo_ref, g_ref, gate_ref, o_ref):
    y = _dot(a_ref[0], wo_ref[...])
    o_ref[0] = x_ref[0] + gate_ref[0] * (_rms(y) * g_ref[...])


def _oproj(x, a, wo, g, gate):
    bsz, seq, d = x.shape
    tm = ROW_TILE
    tile = pl.BlockSpec((1, tm, d), lambda b, i: (b, i, 0))
    return pl.pallas_call(
        _o_kernel,
        grid=(bsz, seq // tm),
        in_specs=[tile, tile, _const_spec((d, d)), _const_spec((1, d)),
                  pl.BlockSpec((1, 1, d), lambda b, i: (b, 0, 0))],
        out_specs=tile,
        out_shape=jax.ShapeDtypeStruct((bsz, seq, d), _F32),
        compiler_params=_params("parallel", "parallel"),
        name="oproj",
    )(x, a, wo, g, gate)


def kernel(x, c, mod_w, mod_b, norm_g, cm_w1, cm_b1, cm_dw, cm_dwb, cm_ln_g, cm_ln_b, cm_w2, cm_b2, kv_norm_g,
           w_k, w_v, w_q, lam, subln_g, w_o, rel_bias, ffn_w_in, ffn_dw, ffn_dwb, ffn_w_out):
    depth = mod_w.shape[0]
    n_conv = cm_w1.shape[0]
    bsz, seq, d = x.shape

    mod = _modulation(c, mod_w, mod_b)

    def row(v):
        return v.reshape(1, -1)

    kk = vt = bias = None
    for l in range(depth):
        sh_m, sc_m, g_m, sh_f, sc_f, g_f = [m.reshape(bsz, 1, d) for m in jnp.split(mod[l], 6, axis=-1)]
        g0, g1, g2, g3 = [row(norm_g[l, k]) for k in range(4)]
        if l < n_conv:
            u = _conv_a(x, g0, sc_m, sh_m, cm_w1[l].astype(_BF16), row(cm_b1[l]))
            x = _conv_b(x, u, cm_dw[l], row(cm_dwb[l]), row(cm_ln_g[l]), row(cm_ln_b[l]),
                        cm_w2[l].astype(_BF16), row(cm_b2[l]), g1, g_m)
        else:
            if l == n_conv:
                kk, vt = _kv(x, row(kv_norm_g), w_k.astype(_BF16), w_v.T.astype(_BF16))
                bias = _bias_tiles(rel_bias)
            j = l - n_conv
            lambda_init = 0.8 - 0.6 * math.exp(-0.3 * l)
            qt = _qproj(x, g0, sc_m, sh_m, w_q[j].T.astype(_BF16))
            a = _attention(qt, kk, vt, bias, lam[j], subln_g[j].reshape(-1, 1), lambda_init)
            x = _oproj(x, a, w_o[j].astype(_BF16), g1, g_m)
        x = _ffn(x, g2, sc_f, sh_f, ffn_w_in[l].astype(_BF16), ffn_dw[l], row(ffn_dwb[l]),
                 ffn_w_out[l].astype(_BF16), g3, g_f)
    return x
```

```python
import functools
import math

import jax
import jax.numpy as jnp
from jax import lax
from jax.experimental import pallas as pl
from jax.experimental.pallas import tpu as pltpu

N_HEADS = 8
HEAD_DIM = 64
V_HEAD_DIM = 128
CONV_WIDTH = 31
FFN_CONV_WIDTH = 3
N_BUCKETS = 32
MAX_DISTANCE = 128
MAX_EXACT = 16
EPS = 1e-6

ROW_TILE = 512
ATTN_TILE = 256
ATTN_HEADS_PER_STEP = 8
ONES_ROWS = 16
V_ROWS = V_HEAD_DIM + ONES_ROWS
LOG2E = math.log2(math.e)
CONV_HALO = 32
CONV_ROWS = 64
LANES = 128
FFN_CHUNK = 256
FFN_HALO = 8
MASK_VALUE = -1e30
VMEM_LIMIT = 56 * 1024 * 1024

_BF16 = jnp.bfloat16
_F32 = jnp.float32


def _bucket_of_distance(n):
    if n < MAX_EXACT:
        return n
    large = MAX_EXACT + int(math.log(n / MAX_EXACT) / math.log(MAX_DISTANCE / MAX_EXACT) * (N_BUCKETS - MAX_EXACT))
    return min(large, N_BUCKETS - 1)


def _bucket_segments():
    segs = []
    for n in range(MAX_DISTANCE):
        b = _bucket_of_distance(n)
        if not segs or segs[-1][1] != b:
            segs.append((n, b))
    assert all(_bucket_of_distance(n) == N_BUCKETS - 1 for n in range(segs[-1][0], 4 * MAX_DISTANCE))
    return segs


def _rms(x):
    return x * lax.rsqrt(jnp.mean(x * x, axis=-1, keepdims=True) + EPS)


def _dot(a, b):
    return jnp.dot(a, b, preferred_element_type=_F32)


def _params(*sem):
    return pltpu.CompilerParams(dimension_semantics=sem, vmem_limit_bytes=VMEM_LIMIT)


def _const_spec(shape):
    nd = len(shape)
    return pl.BlockSpec(shape, lambda *_: (0,) * nd)


def _mod_kernel(c_ref, w_ref, b_ref, o_ref):
    c = c_ref[...]
    c_act = c * jax.nn.sigmoid(c)
    o_ref[0] = _dot(c_act, w_ref[0]) + b_ref[0]


def _modulation(c, mod_w, mod_b):
    depth, d, n = mod_w.shape
    bsz = c.shape[0]
    tn = 1024
    return pl.pallas_call(
        _mod_kernel,
        grid=(depth, n // tn),
        in_specs=[
            pl.BlockSpec((bsz, d), lambda l, j: (0, 0)),
            pl.BlockSpec((1, d, tn), lambda l, j: (l, 0, j)),
            pl.BlockSpec((1, 1, tn), lambda l, j: (l, 0, j)),
        ],
        out_specs=pl.BlockSpec((1, bsz, tn), lambda l, j: (l, 0, j)),
        out_shape=jax.ShapeDtypeStruct((depth, bsz, n), _F32),
        compiler_params=_params("arbitrary", "arbitrary"),
        name="mod",
    )(c, mod_w, mod_b.reshape(depth, 1, n))


def _conv_a_kernel(x_ref, g_ref, sc_ref, sh_ref, w1_ref, b1_ref, u_ref):
    d = x_ref.shape[-1]
    h = _rms(x_ref[0]) * g_ref[...] * (1.0 + sc_ref[0]) + sh_ref[0]
    hb = h.astype(_BF16)
    a = _dot(hb, w1_ref[:, :d]) + b1_ref[:, :d]
    gt = _dot(hb, w1_ref[:, d:]) + b1_ref[:, d:]
    u_ref[0] = a * jax.nn.sigmoid(gt)


def _conv_a(x, g, sc, sh, w1, b1):
    bsz, seq, d = x.shape
    tm = ROW_TILE
    vec = pl.BlockSpec((1, 1, d), lambda b, i: (b, 0, 0))
    return pl.pallas_call(
        _conv_a_kernel,
        grid=(bsz, seq // tm),
        in_specs=[
            pl.BlockSpec((1, tm, d), lambda b, i: (b, i, 0)),
            _const_spec((1, d)), vec, vec,
            _const_spec((d, 2 * d)), _const_spec((1, 2 * d)),
        ],
        out_specs=pl.BlockSpec((1, tm, d), lambda b, i: (b, i, 0)),
        out_shape=jax.ShapeDtypeStruct((bsz, seq, d), _F32),
        compiler_params=_params("parallel", "parallel"),
        name="conv_a",
    )(x, g, sc, sh, w1, b1)


def _conv_b_kernel(x_ref, ucur_ref, uprev_ref, dw_ref, dwb_ref, lng_ref, lnb_ref, w2_ref, b2_ref,
                   g_ref, gate_ref, o_ref, ubuf, cbuf):
    tm, d = ucur_ref.shape[1], ucur_ref.shape[2]
    i = pl.program_id(1)
    for c in range(d // LANES):
        lanes = slice(c * LANES, (c + 1) * LANES)
        prev = uprev_ref[0, :, lanes]
        ubuf[c, 0:CONV_HALO, :] = jnp.where(i > 0, prev, jnp.zeros_like(prev))
        ubuf[c, CONV_HALO:, :] = ucur_ref[0, :, lanes]

    first_shift = CONV_HALO - (CONV_WIDTH - 1)

    def row_block(r, carry):
        r0 = pl.multiple_of(r * CONV_ROWS, CONV_ROWS)
        for c in range(d // LANES):
            lanes = slice(c * LANES, (c + 1) * LANES)
            acc = jnp.broadcast_to(dwb_ref[:, lanes], (CONV_ROWS, LANES))
            for j in range(CONV_WIDTH):
                acc = acc + ubuf[c, pl.ds(r0 + first_shift + j, CONV_ROWS), :] * dw_ref[j:j + 1, lanes]
            cbuf[pl.ds(r0, CONV_ROWS), lanes] = acc
        return carry

    lax.fori_loop(0, tm // CONV_ROWS, row_block, 0)

    cv = cbuf[...]
    mu = jnp.mean(cv, axis=-1, keepdims=True)
    cc = cv - mu
    var = jnp.mean(cc * cc, axis=-1, keepdims=True)
    z = cc * lax.rsqrt(var + EPS) * lng_ref[...] + lnb_ref[...]
    z = z * jax.nn.sigmoid(z)
    y = _dot(z.astype(_BF16), w2_ref[...]) + b2_ref[...]
    o_ref[0] = x_ref[0] + gate_ref[0] * (_rms(y) * g_ref[...])


def _conv_b(x, u, dw, dwb, lng, lnb, w2, b2, g, gate):
    bsz, seq, d = x.shape
    tm = ROW_TILE
    ratio = tm // CONV_HALO
    vec = pl.BlockSpec((1, 1, d), lambda b, i: (b, 0, 0))
    tile = pl.BlockSpec((1, tm, d), lambda b, i: (b, i, 0))
    return pl.pallas_call(
        _conv_b_kernel,
        grid=(bsz, seq // tm),
        in_specs=[
            tile, tile,
            pl.BlockSpec((1, CONV_HALO, d), lambda b, i: (b, jnp.maximum(i * ratio - 1, 0), 0)),
            _const_spec((CONV_WIDTH, d)), _const_spec((1, d)), _const_spec((1, d)), _const_spec((1, d)),
            _const_spec((d, d)), _const_spec((1, d)), _const_spec((1, d)), vec,
        ],
        out_specs=tile,
        out_shape=jax.ShapeDtypeStruct((bsz, seq, d), _F32),
        scratch_shapes=[pltpu.VMEM((d // LANES, tm + CONV_HALO, LANES), _F32), pltpu.VMEM((tm, d), _F32)],
        compiler_params=_params("parallel", "parallel"),
        name="conv_b",
    )(x, u, u, dw, dwb, lng, lnb, w2, b2, g, gate)


def _ffn_kernel(*refs, has_oproj, has_q):
    refs = list(refs)
    x_ref = refs.pop(0)
    if has_oproj:
        a_ref, wo_ref, g1_ref, gatem_ref = refs[:4]
        refs = refs[4:]
    g_ref, sc_ref, sh_ref, win_ref, dw_ref, dwb_ref, wout_ref, g3_ref, gate_ref = refs[:9]
    refs = refs[9:]
    if has_q:
        gq_ref, scq_ref, shq_ref, wqt_ref = refs[:4]
        refs = refs[4:]
    o_ref = refs.pop(0)
    if has_q:
        q_ref = refs.pop(0)
    hbuf, ubuf, carry, pbuf = refs[:4]
    tm = x_ref.shape[1]
    f = wout_ref.shape[0]
    i = pl.program_id(1)
    slabs = FFN_CHUNK // LANES

    @pl.when(i == 0)
    def _():
        carry[...] = jnp.zeros_like(carry)

    if has_oproj:
        xmid = refs[4]
        xmid[...] = x_ref[0] + gatem_ref[0] * (_rms(_dot(a_ref[0], wo_ref[...])) * g1_ref[...])
        x_in = xmid
    else:
        x_in = x_ref.at[0]

    h = _rms(x_in[...]) * g_ref[...] * (1.0 + sc_ref[0]) + sh_ref[0]
    hbuf[...] = h.astype(_BF16)

    def up_project(col0, slab0):
        u = _dot(hbuf[...], win_ref[:, col0:col0 + FFN_CHUNK])
        for k in range(slabs):
            cols = slice(col0 + k * LANES, col0 + (k + 1) * LANES)
            uk = u[:, k * LANES:(k + 1) * LANES]
            ubuf[slab0 + k, 0:FFN_HALO, :] = carry[:, cols]
            ubuf[slab0 + k, FFN_HALO:, :] = uk
            carry[:, cols] = uk[tm - FFN_HALO:, :]

    def conv3(col0, slab):
        cols = slice(col0, col0 + LANES)
        out = dwb_ref[:, cols]
        for j in range(FFN_CONV_WIDTH):
            shift = FFN_CONV_WIDTH - 1 - j
            out = out + ubuf[slab, pl.ds(FFN_HALO - shift, tm), :] * dw_ref[j:j + 1, cols]
        return out

    for idx, c0 in enumerate(range(0, f, FFN_CHUNK)):
        base = (idx % 2) * 2 * slabs
        up_project(c0, base)
        up_project(f + c0, base + slabs)
        for k in range(slabs):
            gk = conv3(c0 + k * LANES, base + k)
            vk = conv3(f + c0 + k * LANES, base + slabs + k)
            pbuf[:, c0 + k * LANES:c0 + (k + 1) * LANES] = (gk * jax.nn.sigmoid(gk) * vk).astype(_BF16)

    y = _dot(pbuf[...], wout_ref[...])
    x_out = x_in[...] + gate_ref[0] * (_rms(y) * g3_ref[...])
    o_ref[0] = x_out
    if has_q:
        _project_q(x_out, gq_ref, scq_ref, shq_ref, wqt_ref, q_ref)


def _ffn(x, g, sc, sh, w_in, dw, dwb, w_out, g3, gate, oproj=None, qproj=None):
    bsz, seq, d = x.shape
    f = w_out.shape[0]
    tm = ROW_TILE
    vec = pl.BlockSpec((1, 1, d), lambda b, i: (b, 0, 0))
    tile = pl.BlockSpec((1, tm, d), lambda b, i: (b, i, 0))
    single = dict(pipeline_mode=pl.Buffered(1))
    operands, in_specs = [x], [tile]
    scratch = [
        pltpu.VMEM((tm, d), _BF16),
        pltpu.VMEM((4 * (FFN_CHUNK // LANES), tm + FFN_HALO, LANES), _F32),
        pltpu.VMEM((FFN_HALO, 2 * f), _F32),
        pltpu.VMEM((tm, f), _BF16),
    ]
    if oproj is not None:
        operands += list(oproj)
        in_specs += [tile, pl.BlockSpec((d, d), lambda b, i: (0, 0), **single), _const_spec((1, d)), vec]
        scratch.append(pltpu.VMEM((tm, d), _F32))
    operands += [g, sc, sh, w_in, dw, dwb, w_out, g3, gate]
    in_specs += [
        _const_spec((1, d)), vec, vec,
        pl.BlockSpec((d, 2 * f), lambda b, i: (0, 0), **single),
        _const_spec((FFN_CONV_WIDTH, 2 * f)), _const_spec((1, 2 * f)),
        pl.BlockSpec((f, d), lambda b, i: (0, 0), **single),
        _const_spec((1, d)), vec,
    ]
    out_specs, out_shape = [tile], [jax.ShapeDtypeStruct((bsz, seq, d), _F32)]
    if qproj is not None:
        tq = ATTN_TILE
        operands += list(qproj)
        in_specs += [_const_spec((1, d)), vec, vec, pl.BlockSpec((d, d), lambda b, i: (0, 0), **single)]
        out_specs.append(pl.BlockSpec((1, N_HEADS, tm // tq, V_HEAD_DIM, 2 * tq), lambda b, i: (b, 0, i, 0, 0)))
        out_shape.append(jax.ShapeDtypeStruct((bsz, N_HEADS, seq // tq, V_HEAD_DIM, 2 * tq), _BF16))
    outs = pl.pallas_call(
        functools.partial(_ffn_kernel, has_oproj=oproj is not None, has_q=qproj is not None),
        grid=(bsz, seq // tm),
        in_specs=in_specs,
        out_specs=out_specs,
        out_shape=out_shape,
        scratch_shapes=scratch,
        compiler_params=_params("arbitrary", "arbitrary"),
        name="ffn",
    )(*operands)
    return outs if qproj is not None else outs[0]


def _kv_kernel(x_ref, g_ref, wk_ref, wvt_ref, k_ref, vt_ref):
    tm = x_ref.shape[1]
    tk = vt_ref.shape[-1]
    hb = (_rms(x_ref[0]) * g_ref[...]).astype(_BF16)
    k = _dot(hb, wk_ref[...])
    vt = lax.dot_general(wvt_ref[...], hb, (((1,), (1,)), ((), ())), preferred_element_type=_F32)
    for hd in range(N_HEADS):
        rows = slice(hd * V_HEAD_DIM, (hd + 1) * V_HEAD_DIM)
        k_ref[0, hd] = k[:, rows].astype(_BF16)
        for c in range(tm // tk):
            vt_ref[0, hd, c, :V_HEAD_DIM, :] = vt[rows, c * tk:(c + 1) * tk].astype(_BF16)
            vt_ref[0, hd, c, V_HEAD_DIM:, :] = jnp.ones((ONES_ROWS, tk), _BF16)


def _kv(x, g, wk, wvt):
    bsz, seq, d = x.shape
    tm, tk = ROW_TILE, ATTN_TILE
    return pl.pallas_call(
        _kv_kernel,
        grid=(bsz, seq // tm),
        in_specs=[
            pl.BlockSpec((1, tm, d), lambda b, i: (b, i, 0)),
            _const_spec((1, d)), _const_spec((d, d)), _const_spec((d, d)),
        ],
        out_specs=[
            pl.BlockSpec((1, N_HEADS, tm, V_HEAD_DIM), lambda b, i: (b, 0, i, 0)),
            pl.BlockSpec((1, N_HEADS, tm // tk, V_ROWS, tk), lambda b, i: (b, 0, i, 0, 0)),
        ],
        out_shape=[
            jax.ShapeDtypeStruct((bsz, N_HEADS, seq, V_HEAD_DIM), _BF16),
            jax.ShapeDtypeStruct((bsz, N_HEADS, seq // tk, V_ROWS, tk), _BF16),
        ],
        compiler_params=_params("parallel", "parallel"),
        name="kv",
    )(x, g, wk, wvt)


def _project_q(x, g_ref, sc_ref, sh_ref, wqt_ref, q_ref):
    tm = x.shape[0]
    tq = q_ref.shape[-1] // 2
    h = _rms(x) * g_ref[...] * (1.0 + sc_ref[0]) + sh_ref[0]
    qt = lax.dot_general(wqt_ref[...], h.astype(_BF16), (((1,), (1,)), ((), ())),
                         preferred_element_type=_F32) * (HEAD_DIM ** -0.5 * LOG2E)
    first = lax.broadcasted_iota(jnp.int32, (V_HEAD_DIM, tq), 0) < HEAD_DIM
    for hd in range(N_HEADS):
        for c in range(tm // tq):
            qh = qt[hd * V_HEAD_DIM:(hd + 1) * V_HEAD_DIM, c * tq:(c + 1) * tq]
            q_ref[0, hd, c, :, :tq] = jnp.where(first, qh, 0.0).astype(_BF16)
            q_ref[0, hd, c, :, tq:] = jnp.where(first, 0.0, qh).astype(_BF16)


def _q_kernel(x_ref, g_ref, sc_ref, sh_ref, wqt_ref, q_ref):
    _project_q(x_ref[0], g_ref, sc_ref, sh_ref, wqt_ref, q_ref)


def _qproj(x, g, sc, sh, wqt):
    bsz, seq, d = x.shape
    tm, tq = ROW_TILE, ATTN_TILE
    vec = pl.BlockSpec((1, 1, d), lambda b, i: (b, 0, 0))
    return pl.pallas_call(
        _q_kernel,
        grid=(bsz, seq // tm),
        in_specs=[pl.BlockSpec((1, tm, d), lambda b, i: (b, i, 0)), _const_spec((1, d)), vec, vec,
                  _const_spec((d, d))],
        out_specs=pl.BlockSpec((1, N_HEADS, tm // tq, V_HEAD_DIM, 2 * tq), lambda b, i: (b, 0, i, 0, 0)),
        out_shape=jax.ShapeDtypeStruct((bsz, N_HEADS, seq // tq, V_HEAD_DIM, 2 * tq), _BF16),
        compiler_params=_params("parallel", "parallel"),
        name="qproj",
    )(x, g, sc, sh, wqt)


def _bias_kernel(rb_ref, o_ref):
    hd = pl.program_id(0)
    t = o_ref.shape[-1]
    key = lax.broadcasted_iota(jnp.int32, (t, t), 0)
    qry = lax.broadcasted_iota(jnp.int32, (t, t), 1)
    segs = _bucket_segments()
    far = rb_ref[N_BUCKETS - 1, hd]
    for which in range(2):
        rel = qry - key + which * t
        val = jnp.full((t, t), 0.0, _F32)
        for k in range(len(segs) - 2, -1, -1):
            val = jnp.where(rel < segs[k + 1][0], (rb_ref[segs[k][1], hd] - far) * LOG2E, val)
        o_ref[0, which] = jnp.where(rel >= 0, val, MASK_VALUE)


def _bias_tiles(rel_bias):
    t = ATTN_TILE
    return pl.pallas_call(
        _bias_kernel,
        grid=(N_HEADS,),
        in_specs=[pl.BlockSpec(memory_space=pltpu.SMEM)],
        out_specs=pl.BlockSpec((1, 2, t, t), lambda h: (h, 0, 0, 0)),
        out_shape=jax.ShapeDtypeStruct((N_HEADS, 2, t, t), _F32),
        compiler_params=_params("arbitrary"),
        name="bias",
    )(rel_bias)


def _attn_kernel(q_ref, qn_ref, k_ref, vt_ref, bias_ref, lam_ref, sg_ref, o_ref, m_sc, acc_sc, s_a, s_b, x_a, x_b,
                 *, lambda_init):
    t = ATTN_TILE
    dv = V_HEAD_DIM
    heads = q_ref.shape[1]
    i = pl.program_id(2)
    m_sc[...] = jnp.full_like(m_sc, MASK_VALUE)
    acc_sc[...] = jnp.zeros_like(acc_sc)

    def scores_into(buf, j, hh, q_src=q_ref):
        s_buf, x_buf = buf
        rows = pl.ds(pl.multiple_of(j * t, t), t)
        s = _dot(k_ref[0, hh, rows, :], q_src[0, hh, 0])
        s_buf[hh] = s
        x_buf[hh] = jnp.max(s, axis=0, keepdims=True)

    def step(j, which, cur, nxt):
        if nxt is not None:
            scores_into(nxt, j + 1, 0)
        for hh in range(heads):
            if nxt is not None and hh + 1 < heads:
                scores_into(nxt, j + 1, hh + 1)
            s = cur[0][hh]
            if which is None:
                m_cur = cur[1][hh]
            else:
                bias = bias_ref[hh, which]
                s = jnp.concatenate([s[:, :t] + bias, s[:, t:] + bias], axis=1)
                m_cur = jnp.max(s, axis=0, keepdims=True)
            m_prev = m_sc[hh]
            m_new = jnp.maximum(m_prev, m_cur)
            alpha = jnp.exp2(m_prev - m_new)
            p = jnp.exp2(s - m_new)
            acc_sc[hh] = alpha * acc_sc[hh] + _dot(vt_ref[0, hh, j], p.astype(_BF16))
            m_sc[hh] = m_new

    buf_a, buf_b = (s_a, x_a), (s_b, x_b)

    def far_pair(jj, carry):
        ja = 2 * jj
        for hh in range(heads):
            m_prev = m_sc[hh]
            m_new = jnp.maximum(m_prev, jnp.maximum(x_a[hh], x_b[hh]))
            alpha = jnp.exp2(m_prev - m_new)
            pa = jnp.exp2(s_a[hh] - m_new).astype(_BF16)
            scores_into(buf_a, ja + 2, hh)
            pb = jnp.exp2(s_b[hh] - m_new).astype(_BF16)
            scores_into(buf_b, ja + 3, hh)
            vt2 = jnp.concatenate([vt_ref[0, hh, ja], vt_ref[0, hh, ja + 1]], axis=1)
            pv = _dot(vt2, jnp.concatenate([pa, pb], axis=0))
            acc_sc[hh] = alpha * acc_sc[hh] + pv
            m_sc[hh] = m_new
        return carry

    @pl.when(i == 0)
    def _():
        for hh in range(heads):
            scores_into(buf_a, 0, hh)

    lax.fori_loop(0, jnp.maximum((i - 1) // 2, 0), far_pair, 0)

    @pl.when(i % 2 == 0)
    def _():
        @pl.when(i >= 2)
        def _():
            step(i - 2, None, buf_a, None)
            step(i - 1, 1, buf_b, buf_a)

        step(i, 0, buf_a, None)

    @pl.when(i % 2 == 1)
    def _():
        step(i - 1, 1, buf_a, None)
        step(i, 0, buf_b, None)

    for hh in range(heads):
        scores_into(buf_a, 0, hh, qn_ref)
        scores_into(buf_b, 1, hh, qn_ref)

    lv = lam_ref[...]
    lam = (jnp.exp(jnp.sum(lv[0:1] * lv[1:2], axis=-1, keepdims=True))
           - jnp.exp(jnp.sum(lv[2:3] * lv[3:4], axis=-1, keepdims=True)) + lambda_init)
    for hh in range(heads):
        acc = acc_sc[hh]
        o = acc[:dv] / acc[dv:dv + 1]
        o = o[:, :t] - lam * o[:, t:]
        o = o * lax.rsqrt(jnp.mean(o * o, axis=0, keepdims=True) + EPS) * sg_ref[...] * (1.0 - lambda_init)
        o_ref[0, :, hh * dv:(hh + 1) * dv] = o.T.astype(o_ref.dtype)


def _attention(qt, k, vt, bias, lam, sg, lambda_init):
    bsz, n_heads, seq, dv = k.shape
    t = ATTN_TILE
    nt = seq // t
    hp = ATTN_HEADS_PER_STEP
    single = dict(pipeline_mode=pl.Buffered(1))
    return pl.pallas_call(
        functools.partial(_attn_kernel, lambda_init=lambda_init),
        grid=(bsz, n_heads // hp, nt),
        in_specs=[
            pl.BlockSpec((1, hp, 1, dv, 2 * t), lambda b, h, i: (b, h, i, 0, 0)),
            pl.BlockSpec((1, hp, 1, dv, 2 * t), lambda b, h, i: (b, h, jnp.minimum(i + 1, nt - 1), 0, 0)),
            pl.BlockSpec((1, hp, seq, dv), lambda b, h, i: (b, h, 0, 0), **single),
            pl.BlockSpec((1, hp, nt, V_ROWS, t), lambda b, h, i: (b, h, 0, 0, 0), **single),
            pl.BlockSpec((hp, 2, t, t), lambda b, h, i: (h, 0, 0, 0), **single),
            pl.BlockSpec((4, HEAD_DIM), lambda b, h, i: (0, 0)),
            pl.BlockSpec((dv, 1), lambda b, h, i: (0, 0)),
        ],
        out_specs=pl.BlockSpec((1, t, hp * dv), lambda b, h, i: (b, i, h)),
        out_shape=jax.ShapeDtypeStruct((bsz, seq, n_heads * dv), _BF16),
        scratch_shapes=[pltpu.VMEM((hp, 1, 2 * t), _F32), pltpu.VMEM((hp, V_ROWS, 2 * t), _F32),
                        pltpu.VMEM((hp, t, 2 * t), _F32), pltpu.VMEM((hp, t, 2 * t), _F32),
                        pltpu.VMEM((hp, 1, 2 * t), _F32), pltpu.VMEM((hp, 1, 2 * t), _F32)],
        compiler_params=_params("parallel", "parallel", "arbitrary"),
        name="attn",
    )(qt, qt, k, vt, bias, lam, sg)


def _o_kernel(x_ref, a_ref, wo_ref, g_ref, gate_ref, o_ref):
    y = _dot(a_ref[0], wo_ref[...])
    o_ref[0] = x_ref[0] + gate_ref[0] * (_rms(y) * g_ref[...])


def _oproj(x, a, wo, g, gate):
    bsz, seq, d = x.shape
    tm = ROW_TILE
    tile = pl.BlockSpec((1, tm, d), lambda b, i: (b, i, 0))
    return pl.pallas_call(
        _o_kernel,
        grid=(bsz, seq // tm),
        in_specs=[tile, tile, _const_spec((d, d)), _const_spec((1, d)),
                  pl.BlockSpec((1, 1, d), lambda b, i: (b, 0, 0))],
        out_specs=tile,
        out_shape=jax.ShapeDtypeStruct((bsz, seq, d), _F32),
        compiler_params=_params("parallel", "parallel"),
        name="oproj",
    )(x, a, wo, g, gate)


def kernel(x, c, mod_w, mod_b, norm_g, cm_w1, cm_b1, cm_dw, cm_dwb, cm_ln_g, cm_ln_b, cm_w2, cm_b2, kv_norm_g,
           w_k, w_v, w_q, lam, subln_g, w_o, rel_bias, ffn_w_in, ffn_dw, ffn_dwb, ffn_w_out):
    depth = mod_w.shape[0]
    n_conv = cm_w1.shape[0]
    bsz, seq, d = x.shape

    mod = _modulation(c, mod_w, mod_b)

    def row(v):
        return v.reshape(1, -1)

    def mod_vectors(l):
        return [m.reshape(bsz, 1, d) for m in jnp.split(mod[l], 6, axis=-1)]

    kk = vt = bias = qt = None
    for l in range(depth):
        sh_m, sc_m, g_m, sh_f, sc_f, g_f = mod_vectors(l)
        g0, g1, g2, g3 = [row(norm_g[l, k]) for k in range(4)]
        oproj = None
        if l < n_conv:
            u = _conv_a(x, g0, sc_m, sh_m, cm_w1[l].astype(_BF16), row(cm_b1[l]))
            x = _conv_b(x, u, cm_dw[l], row(cm_dwb[l]), row(cm_ln_g[l]), row(cm_ln_b[l]),
                        cm_w2[l].astype(_BF16), row(cm_b2[l]), g1, g_m)
        else:
            if l == n_conv:
                kk, vt = _kv(x, row(kv_norm_g), w_k.astype(_BF16), w_v.T.astype(_BF16))
                bias = _bias_tiles(rel_bias)
            j = l - n_conv
            lambda_init = 0.8 - 0.6 * math.exp(-0.3 * l)
            if qt is None:
                qt = _qproj(x, g0, sc_m, sh_m, w_q[j].T.astype(_BF16))
            a = _attention(qt, kk, vt, bias, lam[j], subln_g[j].reshape(-1, 1), lambda_init)
            oproj = (a, w_o[j].astype(_BF16), g1, g_m)
        qproj = None
        if n_conv <= l + 1 < depth:
            nsh_m, nsc_m, _, _, _, _ = mod_vectors(l + 1)
            qproj = (row(norm_g[l + 1, 0]), nsc_m, nsh_m, w_q[l + 1 - n_conv].T.astype(_BF16))
        res = _ffn(x, g2, sc_f, sh_f, ffn_w_in[l].astype(_BF16), ffn_dw[l], row(ffn_dwb[l]),
                   ffn_w_out[l].astype(_BF16), g3, g_f, oproj=oproj, qproj=qproj)
        x, qt = res if qproj is not None else (res, None)
    return x
```

```python
import functools
import math

import jax
import jax.numpy as jnp
from jax import lax
from jax.experimental import pallas as pl
from jax.experimental.pallas import tpu as pltpu

N_HEADS = 8
HEAD_DIM = 64
V_HEAD_DIM = 128
CONV_WIDTH = 31
FFN_CONV_WIDTH = 3
N_BUCKETS = 32
MAX_DISTANCE = 128
MAX_EXACT = 16
EPS = 1e-6

ROW_TILE = 512
ATTN_TILE = 256
ATTN_HEADS_PER_STEP = 8
ONES_ROWS = 16
V_ROWS = V_HEAD_DIM + ONES_ROWS
LOG2E = math.log2(math.e)
CONV_HALO = 32
CONV_ROWS = 64
LANES = 128
FFN_CHUNK = 256
FFN_HALO = 8
MASK_VALUE = -1e30
VMEM_LIMIT = 56 * 1024 * 1024

_BF16 = jnp.bfloat16
_F32 = jnp.float32


def _bucket_of_distance(n):
    if n < MAX_EXACT:
        return n
    large = MAX_EXACT + int(math.log(n / MAX_EXACT) / math.log(MAX_DISTANCE / MAX_EXACT) * (N_BUCKETS - MAX_EXACT))
    return min(large, N_BUCKETS - 1)


def _bucket_segments():
    segs = []
    for n in range(MAX_DISTANCE):
        b = _bucket_of_distance(n)
        if not segs or segs[-1][1] != b:
            segs.append((n, b))
    assert all(_bucket_of_distance(n) == N_BUCKETS - 1 for n in range(segs[-1][0], 4 * MAX_DISTANCE))
    return segs


def _rms(x):
    return x * lax.rsqrt(jnp.mean(x * x, axis=-1, keepdims=True) + EPS)


def _dot(a, b):
    return jnp.dot(a, b, preferred_element_type=_F32)


def _params(*sem):
    return pltpu.CompilerParams(dimension_semantics=sem, vmem_limit_bytes=VMEM_LIMIT)


def _const_spec(shape):
    nd = len(shape)
    return pl.BlockSpec(shape, lambda *_: (0,) * nd)


def _mod_kernel(c_ref, w_ref, b_ref, o_ref):
    c = c_ref[...]
    c_act = c * jax.nn.sigmoid(c)
    o_ref[0] = _dot(c_act, w_ref[0]) + b_ref[0]


def _modulation(c, mod_w, mod_b):
    depth, d, n = mod_w.shape
    bsz = c.shape[0]
    tn = 1024
    return pl.pallas_call(
        _mod_kernel,
        grid=(depth, n // tn),
        in_specs=[
            pl.BlockSpec((bsz, d), lambda l, j: (0, 0)),
            pl.BlockSpec((1, d, tn), lambda l, j: (l, 0, j)),
            pl.BlockSpec((1, 1, tn), lambda l, j: (l, 0, j)),
        ],
        out_specs=pl.BlockSpec((1, bsz, tn), lambda l, j: (l, 0, j)),
        out_shape=jax.ShapeDtypeStruct((depth, bsz, n), _F32),
        compiler_params=_params("arbitrary", "arbitrary"),
        name="mod",
    )(c, mod_w, mod_b.reshape(depth, 1, n))


def _conv_a_kernel(x_ref, g_ref, sc_ref, sh_ref, w1_ref, b1_ref, u_ref):
    d = x_ref.shape[-1]
    h = _rms(x_ref[0]) * g_ref[...] * (1.0 + sc_ref[0]) + sh_ref[0]
    hb = h.astype(_BF16)
    a = _dot(hb, w1_ref[:, :d]) + b1_ref[:, :d]
    gt = _dot(hb, w1_ref[:, d:]) + b1_ref[:, d:]
    u_ref[0] = a * jax.nn.sigmoid(gt)


def _conv_a(x, g, sc, sh, w1, b1):
    bsz, seq, d = x.shape
    tm = ROW_TILE
    vec = pl.BlockSpec((1, 1, d), lambda b, i: (b, 0, 0))
    return pl.pallas_call(
        _conv_a_kernel,
        grid=(bsz, seq // tm),
        in_specs=[
            pl.BlockSpec((1, tm, d), lambda b, i: (b, i, 0)),
            _const_spec((1, d)), vec, vec,
            _const_spec((d, 2 * d)), _const_spec((1, 2 * d)),
        ],
        out_specs=pl.BlockSpec((1, tm, d), lambda b, i: (b, i, 0)),
        out_shape=jax.ShapeDtypeStruct((bsz, seq, d), _F32),
        compiler_params=_params("parallel", "parallel"),
        name="conv_a",
    )(x, g, sc, sh, w1, b1)


def _conv_b_kernel(x_ref, ucur_ref, uprev_ref, dw_ref, dwb_ref, lng_ref, lnb_ref, w2_ref, b2_ref,
                   g_ref, gate_ref, o_ref, ubuf, cbuf):
    tm, d = ucur_ref.shape[1], ucur_ref.shape[2]
    i = pl.program_id(1)
    for c in range(d // LANES):
        lanes = slice(c * LANES, (c + 1) * LANES)
        prev = uprev_ref[0, :, lanes]
        ubuf[c, 0:CONV_HALO, :] = jnp.where(i > 0, prev, jnp.zeros_like(prev))
        ubuf[c, CONV_HALO:, :] = ucur_ref[0, :, lanes]

    first_shift = CONV_HALO - (CONV_WIDTH - 1)

    def row_block(r, carry):
        r0 = pl.multiple_of(r * CONV_ROWS, CONV_ROWS)
        for c in range(d // LANES):
            lanes = slice(c * LANES, (c + 1) * LANES)
            acc = jnp.broadcast_to(dwb_ref[:, lanes], (CONV_ROWS, LANES))
            for j in range(CONV_WIDTH):
                acc = acc + ubuf[c, pl.ds(r0 + first_shift + j, CONV_ROWS), :] * dw_ref[j:j + 1, lanes]
            cbuf[pl.ds(r0, CONV_ROWS), lanes] = acc
        return carry

    lax.fori_loop(0, tm // CONV_ROWS, row_block, 0)

    cv = cbuf[...]
    mu = jnp.mean(cv, axis=-1, keepdims=True)
    cc = cv - mu
    var = jnp.mean(cc * cc, axis=-1, keepdims=True)
    z = cc * lax.rsqrt(var + EPS) * lng_ref[...] + lnb_ref[...]
    z = z * jax.nn.sigmoid(z)
    y = _dot(z.astype(_BF16), w2_ref[...]) + b2_ref[...]
    o_ref[0] = x_ref[0] + gate_ref[0] * (_rms(y) * g_ref[...])


def _conv_b(x, u, dw, dwb, lng, lnb, w2, b2, g, gate):
    bsz, seq, d = x.shape
    tm = ROW_TILE
    ratio = tm // CONV_HALO
    vec = pl.BlockSpec((1, 1, d), lambda b, i: (b, 0, 0))
    tile = pl.BlockSpec((1, tm, d), lambda b, i: (b, i, 0))
    return pl.pallas_call(
        _conv_b_kernel,
        grid=(bsz, seq // tm),
        in_specs=[
            tile, tile,
            pl.BlockSpec((1, CONV_HALO, d), lambda b, i: (b, jnp.maximum(i * ratio - 1, 0), 0)),
            _const_spec((CONV_WIDTH, d)), _const_spec((1, d)), _const_spec((1, d)), _const_spec((1, d)),
            _const_spec((d, d)), _const_spec((1, d)), _const_spec((1, d)), vec,
        ],
        out_specs=tile,
        out_shape=jax.ShapeDtypeStruct((bsz, seq, d), _F32),
        scratch_shapes=[pltpu.VMEM((d // LANES, tm + CONV_HALO, LANES), _F32), pltpu.VMEM((tm, d), _F32)],
        compiler_params=_params("parallel", "parallel"),
        name="conv_b",
    )(x, u, u, dw, dwb, lng, lnb, w2, b2, g, gate)


def _ffn_kernel(*refs, has_oproj, has_q):
    refs = list(refs)
    x_ref = refs.pop(0)
    if has_oproj:
        a_ref, wo_ref, g1_ref, gatem_ref = refs[:4]
        refs = refs[4:]
    g_ref, sc_ref, sh_ref, win_ref, dw_ref, dwb_ref, wout_ref, g3_ref, gate_ref = refs[:9]
    refs = refs[9:]
    if has_q:
        gq_ref, scq_ref, shq_ref, wqt_ref = refs[:4]
        refs = refs[4:]
    o_ref = refs.pop(0)
    if has_q:
        q_ref = refs.pop(0)
    hbuf, ubuf, carry, pbuf = refs[:4]
    tm = x_ref.shape[1]
    f = wout_ref.shape[0]
    i = pl.program_id(1)
    slabs = FFN_CHUNK // LANES

    @pl.when(i == 0)
    def _():
        carry[...] = jnp.zeros_like(carry)

    if has_oproj:
        xmid = refs[4]
        xmid[...] = x_ref[0] + gatem_ref[0] * (_rms(_dot(a_ref[0], wo_ref[...])) * g1_ref[...])
        x_in = xmid
    else:
        x_in = x_ref.at[0]

    h = _rms(x_in[...]) * g_ref[...] * (1.0 + sc_ref[0]) + sh_ref[0]
    hbuf[...] = h.astype(_BF16)

    def up_project(col0, slab0):
        u = _dot(hbuf[...], win_ref[:, col0:col0 + FFN_CHUNK])
        for k in range(slabs):
            cols = slice(col0 + k * LANES, col0 + (k + 1) * LANES)
            uk = u[:, k * LANES:(k + 1) * LANES]
            ubuf[slab0 + k, 0:FFN_HALO, :] = carry[:, cols]
            ubuf[slab0 + k, FFN_HALO:, :] = uk
            carry[:, cols] = uk[tm - FFN_HALO:, :]

    def conv3(col0, slab):
        cols = slice(col0, col0 + LANES)
        out = dwb_ref[:, cols]
        for j in range(FFN_CONV_WIDTH):
            shift = FFN_CONV_WIDTH - 1 - j
            out = out + ubuf[slab, pl.ds(FFN_HALO - shift, tm), :] * dw_ref[j:j + 1, cols]
        return out

    for idx, c0 in enumerate(range(0, f, FFN_CHUNK)):
        base = (idx % 2) * 2 * slabs
        up_project(c0, base)
        up_project(f + c0, base + slabs)
        for k in range(slabs):
            gk = conv3(c0 + k * LANES, base + k)
            vk = conv3(f + c0 + k * LANES, base + slabs + k)
            pbuf[:, c0 + k * LANES:c0 + (k + 1) * LANES] = (gk * jax.nn.sigmoid(gk) * vk).astype(_BF16)

    y = _dot(pbuf[...], wout_ref[...])
    x_out = x_in[...] + gate_ref[0] * (_rms(y) * g3_ref[...])
    o_ref[0] = x_out
    if has_q:
        _project_q(x_out, gq_ref, scq_ref, shq_ref, wqt_ref, q_ref)


def _ffn(x, g, sc, sh, w_in, dw, dwb, w_out, g3, gate, oproj=None, qproj=None):
    bsz, seq, d = x.shape
    f = w_out.shape[0]
    tm = ROW_TILE
    vec = pl.BlockSpec((1, 1, d), lambda b, i: (b, 0, 0))
    tile = pl.BlockSpec((1, tm, d), lambda b, i: (b, i, 0))
    single = dict(pipeline_mode=pl.Buffered(1))
    operands, in_specs = [x], [tile]
    scratch = [
        pltpu.VMEM((tm, d), _BF16),
        pltpu.VMEM((4 * (FFN_CHUNK // LANES), tm + FFN_HALO, LANES), _F32),
        pltpu.VMEM((FFN_HALO, 2 * f), _F32),
        pltpu.VMEM((tm, f), _BF16),
    ]
    if oproj is not None:
        operands += list(oproj)
        in_specs += [tile, pl.BlockSpec((d, d), lambda b, i: (0, 0), **single), _const_spec((1, d)), vec]
        scratch.append(pltpu.VMEM((tm, d), _F32))
    operands += [g, sc, sh, w_in, dw, dwb, w_out, g3, gate]
    in_specs += [
        _const_spec((1, d)), vec, vec,
        pl.BlockSpec((d, 2 * f), lambda b, i: (0, 0), **single),
        _const_spec((FFN_CONV_WIDTH, 2 * f)), _const_spec((1, 2 * f)),
        pl.BlockSpec((f, d), lambda b, i: (0, 0), **single),
        _const_spec((1, d)), vec,
    ]
    out_specs, out_shape = [tile], [jax.ShapeDtypeStruct((bsz, seq, d), _F32)]
    if qproj is not None:
        tq = ATTN_TILE
        operands += list(qproj)
        in_specs += [_const_spec((1, d)), vec, vec, pl.BlockSpec((d, d), lambda b, i: (0, 0), **single)]
        out_specs.append(pl.BlockSpec((1, N_HEADS, tm // tq, V_HEAD_DIM, 2 * tq), lambda b, i: (b, 0, i, 0, 0)))
        out_shape.append(jax.ShapeDtypeStruct((bsz, N_HEADS, seq // tq, V_HEAD_DIM, 2 * tq), _BF16))
    outs = pl.pallas_call(
        functools.partial(_ffn_kernel, has_oproj=oproj is not None, has_q=qproj is not None),
        grid=(bsz, seq // tm),
        in_specs=in_specs,
        out_specs=out_specs,
        out_shape=out_shape,
        scratch_shapes=scratch,
        compiler_params=_params("arbitrary", "arbitrary"),
        name="ffn",
    )(*operands)
    return outs if qproj is not None else outs[0]


def _kv_kernel(x_ref, g_ref, wk_ref, wvt_ref, k_ref, vt_ref):
    tm = x_ref.shape[1]
    tk = vt_ref.shape[-1]
    hb = (_rms(x_ref[0]) * g_ref[...]).astype(_BF16)
    k = _dot(hb, wk_ref[...])
    vt = lax.dot_general(wvt_ref[...], hb, (((1,), (1,)), ((), ())), preferred_element_type=_F32)
    for hd in range(N_HEADS):
        rows = slice(hd * V_HEAD_DIM, (hd + 1) * V_HEAD_DIM)
        k_ref[0, hd] = k[:, rows].astype(_BF16)
        for c in range(tm // tk):
            vt_ref[0, hd, c, :V_HEAD_DIM, :] = vt[rows, c * tk:(c + 1) * tk].astype(_BF16)
            vt_ref[0, hd, c, V_HEAD_DIM:, :] = jnp.ones((ONES_ROWS, tk), _BF16)


def _kv(x, g, wk, wvt):
    bsz, seq, d = x.shape
    tm, tk = ROW_TILE, ATTN_TILE
    return pl.pallas_call(
        _kv_kernel,
        grid=(bsz, seq // tm),
        in_specs=[
            pl.BlockSpec((1, tm, d), lambda b, i: (b, i, 0)),
            _const_spec((1, d)), _const_spec((d, d)), _const_spec((d, d)),
        ],
        out_specs=[
            pl.BlockSpec((1, N_HEADS, tm, V_HEAD_DIM), lambda b, i: (b, 0, i, 0)),
            pl.BlockSpec((1, N_HEADS, tm // tk, V_ROWS, tk), lambda b, i: (b, 0, i, 0, 0)),
        ],
        out_shape=[
            jax.ShapeDtypeStruct((bsz, N_HEADS, seq, V_HEAD_DIM), _BF16),
            jax.ShapeDtypeStruct((bsz, N_HEADS, seq // tk, V_ROWS, tk), _BF16),
        ],
        compiler_params=_params("parallel", "parallel"),
        name="kv",
    )(x, g, wk, wvt)


def _project_q(x, g_ref, sc_ref, sh_ref, wqt_ref, q_ref):
    tm = x.shape[0]
    tq = q_ref.shape[-1] // 2
    h = _rms(x) * g_ref[...] * (1.0 + sc_ref[0]) + sh_ref[0]
    qt = lax.dot_general(wqt_ref[...], h.astype(_BF16), (((1,), (1,)), ((), ())),
                         preferred_element_type=_F32) * (HEAD_DIM ** -0.5 * LOG2E)
    first = lax.broadcasted_iota(jnp.int32, (V_HEAD_DIM, tq), 0) < HEAD_DIM
    for hd in range(N_HEADS):
        for c in range(tm // tq):
            qh = qt[hd * V_HEAD_DIM:(hd + 1) * V_HEAD_DIM, c * tq:(c + 1) * tq]
            q_ref[0, hd, c, :, :tq] = jnp.where(first, qh, 0.0).astype(_BF16)
            q_ref[0, hd, c, :, tq:] = jnp.where(first, 0.0, qh).astype(_BF16)


def _bias_kernel(rb_ref, o_ref):
    hd = pl.program_id(0)
    t = o_ref.shape[-1]
    key = lax.broadcasted_iota(jnp.int32, (t, t), 0)
    qry = lax.broadcasted_iota(jnp.int32, (t, t), 1)
    segs = _bucket_segments()
    far = rb_ref[N_BUCKETS - 1, hd]
    for which in range(2):
        rel = qry - key + which * t
        val = jnp.full((t, t), 0.0, _F32)
        for k in range(len(segs) - 2, -1, -1):
            val = jnp.where(rel < segs[k + 1][0], (rb_ref[segs[k][1], hd] - far) * LOG2E, val)
        o_ref[0, which] = jnp.where(rel >= 0, val, MASK_VALUE)


def _bias_tiles(rel_bias):
    t = ATTN_TILE
    return pl.pallas_call(
        _bias_kernel,
        grid=(N_HEADS,),
        in_specs=[pl.BlockSpec(memory_space=pltpu.SMEM)],
        out_specs=pl.BlockSpec((1, 2, t, t), lambda h: (h, 0, 0, 0)),
        out_shape=jax.ShapeDtypeStruct((N_HEADS, 2, t, t), _F32),
        compiler_params=_params("arbitrary"),
        name="bias",
    )(rel_bias)


def _attn_kernel(q_ref, qn_ref, k_ref, vt_ref, bias_ref, lam_ref, sg_ref, o_ref, m_sc, acc_sc, s_a, s_b, x_a, x_b,
                 *, lambda_init):
    t = ATTN_TILE
    dv = V_HEAD_DIM
    heads = q_ref.shape[1]
    i = pl.program_id(2)
    m_sc[...] = jnp.full_like(m_sc, MASK_VALUE)
    acc_sc[...] = jnp.zeros_like(acc_sc)

    def scores_into(buf, j, hh, q_src=q_ref):
        s_buf, x_buf = buf
        rows = pl.ds(pl.multiple_of(j * t, t), t)
        s = _dot(k_ref[0, hh, rows, :], q_src[0, hh, 0])
        s_buf[hh] = s
        x_buf[hh] = jnp.max(s, axis=0, keepdims=True)

    def step(j, which, cur, nxt):
        if nxt is not None:
            scores_into(nxt, j + 1, 0)
        for hh in range(heads):
            if nxt is not None and hh + 1 < heads:
                scores_into(nxt, j + 1, hh + 1)
            s = cur[0][hh]
            if which is None:
                m_cur = cur[1][hh]
            else:
                bias = bias_ref[hh, which]
                s = jnp.concatenate([s[:, :t] + bias, s[:, t:] + bias], axis=1)
                m_cur = jnp.max(s, axis=0, keepdims=True)
            m_prev = m_sc[hh]
            m_new = jnp.maximum(m_prev, m_cur)
            alpha = jnp.exp2(m_prev - m_new)
            p = jnp.exp2(s - m_new)
            acc_sc[hh] = alpha * acc_sc[hh] + _dot(vt_ref[0, hh, j], p.astype(_BF16))
            m_sc[hh] = m_new

    buf_a, buf_b = (s_a, x_a), (s_b, x_b)

    def far_pair(jj, carry):
        ja = 2 * jj
        for hh in range(heads):
            m_prev = m_sc[hh]
            m_new = jnp.maximum(m_prev, jnp.maximum(x_a[hh], x_b[hh]))
            alpha = jnp.exp2(m_prev - m_new)
            pa = jnp.exp2(s_a[hh] - m_new).astype(_BF16)
            scores_into(buf_a, ja + 2, hh)
            pb = jnp.exp2(s_b[hh] - m_new).astype(_BF16)
            scores_into(buf_b, ja + 3, hh)
            vt2 = jnp.concatenate([vt_ref[0, hh, ja], vt_ref[0, hh, ja + 1]], axis=1)
            pv = _dot(vt2, jnp.concatenate([pa, pb], axis=0))
            acc_sc[hh] = alpha * acc_sc[hh] + pv
            m_sc[hh] = m_new
        return carry

    @pl.when(i == 0)
    def _():
        for hh in range(heads):
            scores_into(buf_a, 0, hh)

    lax.fori_loop(0, jnp.maximum((i - 1) // 2, 0), far_pair, 0)

    @pl.when(i % 2 == 0)
    def _():
        @pl.when(i >= 2)
        def _():
            step(i - 2, None, buf_a, None)
            step(i - 1, 1, buf_b, buf_a)

        step(i, 0, buf_a, None)

    @pl.when(i % 2 == 1)
    def _():
        step(i - 1, 1, buf_a, None)
        step(i, 0, buf_b, None)

    for hh in range(heads):
        scores_into(buf_a, 0, hh, qn_ref)
        scores_into(buf_b, 1, hh, qn_ref)

    lv = lam_ref[...]
    lam = (jnp.exp(jnp.sum(lv[0:1] * lv[1:2], axis=-1, keepdims=True))
           - jnp.exp(jnp.sum(lv[2:3] * lv[3:4], axis=-1, keepdims=True)) + lambda_init)
    for hh in range(heads):
        acc = acc_sc[hh]
        o = acc[:dv] / acc[dv:dv + 1]
        o = o[:, :t] - lam * o[:, t:]
        o = o * lax.rsqrt(jnp.mean(o * o, axis=0, keepdims=True) + EPS) * sg_ref[...] * (1.0 - lambda_init)
        o_ref[0, :, hh * dv:(hh + 1) * dv] = o.T.astype(o_ref.dtype)


def _attention(qt, k, vt, bias, lam, sg, lambda_init):
    bsz, n_heads, seq, dv = k.shape
    t = ATTN_TILE
    nt = seq // t
    hp = ATTN_HEADS_PER_STEP
    single = dict(pipeline_mode=pl.Buffered(1))
    return pl.pallas_call(
        functools.partial(_attn_kernel, lambda_init=lambda_init),
        grid=(bsz, n_heads // hp, nt),
        in_specs=[
            pl.BlockSpec((1, hp, 1, dv, 2 * t), lambda b, h, i: (b, h, i, 0, 0)),
            pl.BlockSpec((1, hp, 1, dv, 2 * t), lambda b, h, i: (b, h, jnp.minimum(i + 1, nt - 1), 0, 0)),
            pl.BlockSpec((1, hp, seq, dv), lambda b, h, i: (b, h, 0, 0), **single),
            pl.BlockSpec((1, hp, nt, V_ROWS, t), lambda b, h, i: (b, h, 0, 0, 0), **single),
            pl.BlockSpec((hp, 2, t, t), lambda b, h, i: (h, 0, 0, 0), **single),
            pl.BlockSpec((4, HEAD_DIM), lambda b, h, i: (0, 0)),
            pl.BlockSpec((dv, 1), lambda b, h, i: (0, 0)),
        ],
        out_specs=pl.BlockSpec((1, t, hp * dv), lambda b, h, i: (b, i, h)),
        out_shape=jax.ShapeDtypeStruct((bsz, seq, n_heads * dv), _BF16),
        scratch_shapes=[pltpu.VMEM((hp, 1, 2 * t), _F32), pltpu.VMEM((hp, V_ROWS, 2 * t), _F32),
                        pltpu.VMEM((hp, t, 2 * t), _F32), pltpu.VMEM((hp, t, 2 * t), _F32),
                        pltpu.VMEM((hp, 1, 2 * t), _F32), pltpu.VMEM((hp, 1, 2 * t), _F32)],
        compiler_params=_params("parallel", "parallel", "arbitrary"),
        name="attn",
    )(qt, qt, k, vt, bias, lam, sg)


def kernel(x, c, mod_w, mod_b, norm_g, cm_w1, cm_b1, cm_dw, cm_dwb, cm_ln_g, cm_ln_b, cm_w2, cm_b2, kv_norm_g,
           w_k, w_v, w_q, lam, subln_g, w_o, rel_bias, ffn_w_in, ffn_dw, ffn_dwb, ffn_w_out):
    depth = mod_w.shape[0]
    n_conv = cm_w1.shape[0]
    bsz, seq, d = x.shape
    assert 1 <= n_conv < depth, "the first attention layer's Q is produced by the preceding layer's FFN"

    mod = _modulation(c, mod_w, mod_b)

    def row(v):
        return v.reshape(1, -1)

    def mod_vectors(l):
        return [m.reshape(bsz, 1, d) for m in jnp.split(mod[l], 6, axis=-1)]

    kk = vt = bias = qt = None
    for l in range(depth):
        sh_m, sc_m, g_m, sh_f, sc_f, g_f = mod_vectors(l)
        g0, g1, g2, g3 = [row(norm_g[l, k]) for k in range(4)]
        oproj = None
        if l < n_conv:
            u = _conv_a(x, g0, sc_m, sh_m, cm_w1[l].astype(_BF16), row(cm_b1[l]))
            x = _conv_b(x, u, cm_dw[l], row(cm_dwb[l]), row(cm_ln_g[l]), row(cm_ln_b[l]),
                        cm_w2[l].astype(_BF16), row(cm_b2[l]), g1, g_m)
        else:
            if l == n_conv:
                kk, vt = _kv(x, row(kv_norm_g), w_k.astype(_BF16), w_v.T.astype(_BF16))
                bias = _bias_tiles(rel_bias)
            j = l - n_conv
            lambda_init = 0.8 - 0.6 * math.exp(-0.3 * l)
            a = _attention(qt, kk, vt, bias, lam[j], subln_g[j].reshape(-1, 1), lambda_init)
            oproj = (a, w_o[j].astype(_BF16), g1, g_m)
        qproj = None
        if n_conv <= l + 1 < depth:
            nsh_m, nsc_m, _, _, _, _ = mod_vectors(l + 1)
            qproj = (row(norm_g[l + 1, 0]), nsc_m, nsh_m, w_q[l + 1 - n_conv].T.astype(_BF16))
        res = _ffn(x, g2, sc_f, sh_f, ffn_w_in[l].astype(_BF16), ffn_dw[l], row(ffn_dwb[l]),
                   ffn_w_out[l].astype(_BF16), g3, g_f, oproj=oproj, qproj=qproj)
        x, qt = res if qproj is not None else (res, None)
    return x
```

```python
import functools
import math

import jax
import jax.numpy as jnp
from jax import lax
from jax.experimental import pallas as pl
from jax.experimental.pallas import tpu as pltpu

N_HEADS = 8
HEAD_DIM = 64
V_HEAD_DIM = 128
CONV_WIDTH = 31
FFN_CONV_WIDTH = 3
N_BUCKETS = 32
MAX_DISTANCE = 128
MAX_EXACT = 16
EPS = 1e-6

ROW_TILE = 512
PROJ_TILE = 1024
ATTN_TILE = 256
ATTN_HEADS_PER_STEP = 8
ONES_ROWS = 16
V_ROWS = V_HEAD_DIM + ONES_ROWS
LOG2E = math.log2(math.e)
CONV_HALO = 32
CONV_ROWS = 64
LANES = 128
FFN_CHUNK = 256
FFN_HALO = 8
MASK_VALUE = -1e30
VMEM_LIMIT = 56 * 1024 * 1024

_BF16 = jnp.bfloat16
_F32 = jnp.float32


def _bucket_of_distance(n):
    if n < MAX_EXACT:
        return n
    large = MAX_EXACT + int(math.log(n / MAX_EXACT) / math.log(MAX_DISTANCE / MAX_EXACT) * (N_BUCKETS - MAX_EXACT))
    return min(large, N_BUCKETS - 1)


def _bucket_segments():
    segs = []
    for n in range(MAX_DISTANCE):
        b = _bucket_of_distance(n)
        if not segs or segs[-1][1] != b:
            segs.append((n, b))
    assert all(_bucket_of_distance(n) == N_BUCKETS - 1 for n in range(segs[-1][0], 4 * MAX_DISTANCE))
    return segs


def _rms(x):
    return x * lax.rsqrt(jnp.mean(x * x, axis=-1, keepdims=True) + EPS)


def _dot(a, b):
    return jnp.dot(a, b, preferred_element_type=_F32)


def _params(*sem):
    return pltpu.CompilerParams(dimension_semantics=sem, vmem_limit_bytes=VMEM_LIMIT)


def _const_spec(shape):
    nd = len(shape)
    return pl.BlockSpec(shape, lambda *_: (0,) * nd)


def _mod_kernel(c_ref, w_ref, b_ref, o_ref):
    c = c_ref[...]
    c_act = c * jax.nn.sigmoid(c)
    o_ref[0] = _dot(c_act, w_ref[0]) + b_ref[0]


def _modulation(c, mod_w, mod_b):
    depth, d, n = mod_w.shape
    bsz = c.shape[0]
    tn = 1024
    return pl.pallas_call(
        _mod_kernel,
        grid=(depth, n // tn),
        in_specs=[
            pl.BlockSpec((bsz, d), lambda l, j: (0, 0)),
            pl.BlockSpec((1, d, tn), lambda l, j: (l, 0, j)),
            pl.BlockSpec((1, 1, tn), lambda l, j: (l, 0, j)),
        ],
        out_specs=pl.BlockSpec((1, bsz, tn), lambda l, j: (l, 0, j)),
        out_shape=jax.ShapeDtypeStruct((depth, bsz, n), _F32),
        compiler_params=_params("arbitrary", "arbitrary"),
        name="mod",
    )(c, mod_w, mod_b.reshape(depth, 1, n))


def _conv_a_kernel(x_ref, g_ref, sc_ref, sh_ref, w1_ref, b1_ref, u_ref):
    d = x_ref.shape[-1]
    h = _rms(x_ref[0]) * g_ref[...] * (1.0 + sc_ref[0]) + sh_ref[0]
    hb = h.astype(_BF16)
    a = _dot(hb, w1_ref[:, :d]) + b1_ref[:, :d]
    gt = _dot(hb, w1_ref[:, d:]) + b1_ref[:, d:]
    u_ref[0] = a * jax.nn.sigmoid(gt)


def _conv_a(x, g, sc, sh, w1, b1):
    bsz, seq, d = x.shape
    tm = PROJ_TILE
    vec = pl.BlockSpec((1, 1, d), lambda b, i: (b, 0, 0))
    return pl.pallas_call(
        _conv_a_kernel,
        grid=(bsz, seq // tm),
        in_specs=[
            pl.BlockSpec((1, tm, d), lambda b, i: (b, i, 0)),
            _const_spec((1, d)), vec, vec,
            _const_spec((d, 2 * d)), _const_spec((1, 2 * d)),
        ],
        out_specs=pl.BlockSpec((1, tm, d), lambda b, i: (b, i, 0)),
        out_shape=jax.ShapeDtypeStruct((bsz, seq, d), _F32),
        compiler_params=_params("parallel", "parallel"),
        name="conv_a",
    )(x, g, sc, sh, w1, b1)


def _conv_b_kernel(x_ref, ucur_ref, uprev_ref, dw_ref, dwb_ref, lng_ref, lnb_ref, w2_ref, b2_ref,
                   g_ref, gate_ref, o_ref, ubuf, cbuf):
    tm, d = ucur_ref.shape[1], ucur_ref.shape[2]
    i = pl.program_id(1)
    for c in range(d // LANES):
        lanes = slice(c * LANES, (c + 1) * LANES)
        prev = uprev_ref[0, :, lanes]
        ubuf[c, 0:CONV_HALO, :] = jnp.where(i > 0, prev, jnp.zeros_like(prev))
        ubuf[c, CONV_HALO:, :] = ucur_ref[0, :, lanes]

    first_shift = CONV_HALO - (CONV_WIDTH - 1)

    def row_block(r, carry):
        r0 = pl.multiple_of(r * CONV_ROWS, CONV_ROWS)
        for c in range(d // LANES):
            lanes = slice(c * LANES, (c + 1) * LANES)
            acc = jnp.broadcast_to(dwb_ref[:, lanes], (CONV_ROWS, LANES))
            for j in range(CONV_WIDTH):
                acc = acc + ubuf[c, pl.ds(r0 + first_shift + j, CONV_ROWS), :] * dw_ref[j:j + 1, lanes]
            cbuf[pl.ds(r0, CONV_ROWS), lanes] = acc
        return carry

    lax.fori_loop(0, tm // CONV_ROWS, row_block, 0)

    cv = cbuf[...]
    mu = jnp.mean(cv, axis=-1, keepdims=True)
    cc = cv - mu
    var = jnp.mean(cc * cc, axis=-1, keepdims=True)
    z = cc * lax.rsqrt(var + EPS) * lng_ref[...] + lnb_ref[...]
    z = z * jax.nn.sigmoid(z)
    y = _dot(z.astype(_BF16), w2_ref[...]) + b2_ref[...]
    o_ref[0] = x_ref[0] + gate_ref[0] * (_rms(y) * g_ref[...])


def _conv_b(x, u, dw, dwb, lng, lnb, w2, b2, g, gate):
    bsz, seq, d = x.shape
    tm = PROJ_TILE
    ratio = tm // CONV_HALO
    vec = pl.BlockSpec((1, 1, d), lambda b, i: (b, 0, 0))
    tile = pl.BlockSpec((1, tm, d), lambda b, i: (b, i, 0))
    return pl.pallas_call(
        _conv_b_kernel,
        grid=(bsz, seq // tm),
        in_specs=[
            tile, tile,
            pl.BlockSpec((1, CONV_HALO, d), lambda b, i: (b, jnp.maximum(i * ratio - 1, 0), 0)),
            _const_spec((CONV_WIDTH, d)), _const_spec((1, d)), _const_spec((1, d)), _const_spec((1, d)),
            _const_spec((d, d)), _const_spec((1, d)), _const_spec((1, d)), vec,
        ],
        out_specs=tile,
        out_shape=jax.ShapeDtypeStruct((bsz, seq, d), _F32),
        scratch_shapes=[pltpu.VMEM((d // LANES, tm + CONV_HALO, LANES), _F32), pltpu.VMEM((tm, d), _F32)],
        compiler_params=_params("parallel", "parallel"),
        name="conv_b",
    )(x, u, u, dw, dwb, lng, lnb, w2, b2, g, gate)


def _ffn_kernel(*refs, has_oproj, has_q):
    refs = list(refs)
    x_ref = refs.pop(0)
    if has_oproj:
        a_ref, wo_ref, g1_ref, gatem_ref = refs[:4]
        refs = refs[4:]
    g_ref, sc_ref, sh_ref, win_ref, dw_ref, dwb_ref, wout_ref, g3_ref, gate_ref = refs[:9]
    refs = refs[9:]
    if has_q:
        gq_ref, scq_ref, shq_ref, wqt_ref = refs[:4]
        refs = refs[4:]
    o_ref = refs.pop(0)
    if has_q:
        q_ref = refs.pop(0)
    hbuf, ubuf, carry, pbuf = refs[:4]
    tm = x_ref.shape[1]
    f = wout_ref.shape[0]
    i = pl.program_id(1)
    slabs = FFN_CHUNK // LANES

    @pl.when(i == 0)
    def _():
        carry[...] = jnp.zeros_like(carry)

    if has_oproj:
        xmid = refs[4]
        xmid[...] = x_ref[0] + gatem_ref[0] * (_rms(_dot(a_ref[0], wo_ref[...])) * g1_ref[...])
        x_in = xmid
    else:
        x_in = x_ref.at[0]

    h = _rms(x_in[...]) * g_ref[...] * (1.0 + sc_ref[0]) + sh_ref[0]
    hbuf[...] = h.astype(_BF16)

    def up_project(col0, slab0):
        u = _dot(hbuf[...], win_ref[:, col0:col0 + FFN_CHUNK])
        for k in range(slabs):
            cols = slice(col0 + k * LANES, col0 + (k + 1) * LANES)
            uk = u[:, k * LANES:(k + 1) * LANES]
            ubuf[slab0 + k, 0:FFN_HALO, :] = carry[:, cols]
            ubuf[slab0 + k, FFN_HALO:, :] = uk
            carry[:, cols] = uk[tm - FFN_HALO:, :]

    def conv3(col0, slab):
        cols = slice(col0, col0 + LANES)
        out = dwb_ref[:, cols]
        for j in range(FFN_CONV_WIDTH):
            shift = FFN_CONV_WIDTH - 1 - j
            out = out + ubuf[slab, pl.ds(FFN_HALO - shift, tm), :] * dw_ref[j:j + 1, cols]
        return out

    for idx, c0 in enumerate(range(0, f, FFN_CHUNK)):
        base = (idx % 2) * 2 * slabs
        up_project(c0, base)
        up_project(f + c0, base + slabs)
        for k in range(slabs):
            gk = conv3(c0 + k * LANES, base + k)
            vk = conv3(f + c0 + k * LANES, base + slabs + k)
            pbuf[:, c0 + k * LANES:c0 + (k + 1) * LANES] = (gk * jax.nn.sigmoid(gk) * vk).astype(_BF16)

    y = _dot(pbuf[...], wout_ref[...])
    x_out = x_in[...] + gate_ref[0] * (_rms(y) * g3_ref[...])
    o_ref[0] = x_out
    if has_q:
        _project_q(x_out, gq_ref, scq_ref, shq_ref, wqt_ref, q_ref)


def _ffn(x, g, sc, sh, w_in, dw, dwb, w_out, g3, gate, oproj=None, qproj=None):
    bsz, seq, d = x.shape
    f = w_out.shape[0]
    tm = ROW_TILE
    vec = pl.BlockSpec((1, 1, d), lambda b, i: (b, 0, 0))
    tile = pl.BlockSpec((1, tm, d), lambda b, i: (b, i, 0))
    single = dict(pipeline_mode=pl.Buffered(1))
    operands, in_specs = [x], [tile]
    scratch = [
        pltpu.VMEM((tm, d), _BF16),
        pltpu.VMEM((4 * (FFN_CHUNK // LANES), tm + FFN_HALO, LANES), _F32),
        pltpu.VMEM((FFN_HALO, 2 * f), _F32),
        pltpu.VMEM((tm, f), _BF16),
    ]
    if oproj is not None:
        operands += list(oproj)
        in_specs += [tile, pl.BlockSpec((d, d), lambda b, i: (0, 0), **single), _const_spec((1, d)), vec]
        scratch.append(pltpu.VMEM((tm, d), _F32))
    operands += [g, sc, sh, w_in, dw, dwb, w_out, g3, gate]
    in_specs += [
        _const_spec((1, d)), vec, vec,
        pl.BlockSpec((d, 2 * f), lambda b, i: (0, 0), **single),
        _const_spec((FFN_CONV_WIDTH, 2 * f)), _const_spec((1, 2 * f)),
        pl.BlockSpec((f, d), lambda b, i: (0, 0), **single),
        _const_spec((1, d)), vec,
    ]
    out_specs, out_shape = [tile], [jax.ShapeDtypeStruct((bsz, seq, d), _F32)]
    if qproj is not None:
        tq = ATTN_TILE
        operands += list(qproj)
        in_specs += [_const_spec((1, d)), vec, vec, pl.BlockSpec((d, d), lambda b, i: (0, 0), **single)]
        out_specs.append(pl.BlockSpec((1, N_HEADS, tm // tq, V_HEAD_DIM, 2 * tq), lambda b, i: (b, 0, i, 0, 0)))
        out_shape.append(jax.ShapeDtypeStruct((bsz, N_HEADS, seq // tq, V_HEAD_DIM, 2 * tq), _BF16))
    outs = pl.pallas_call(
        functools.partial(_ffn_kernel, has_oproj=oproj is not None, has_q=qproj is not None),
        grid=(bsz, seq // tm),
        in_specs=in_specs,
        out_specs=out_specs,
        out_shape=out_shape,
        scratch_shapes=scratch,
        compiler_params=_params("arbitrary", "arbitrary"),
        name="ffn",
    )(*operands)
    return outs if qproj is not None else outs[0]


def _kv_kernel(x_ref, g_ref, wk_ref, wvt_ref, k_ref, vt_ref):
    tm = x_ref.shape[1]
    tk = vt_ref.shape[-1]
    hb = (_rms(x_ref[0]) * g_ref[...]).astype(_BF16)
    k = _dot(hb, wk_ref[...])
    vt = lax.dot_general(wvt_ref[...], hb, (((1,), (1,)), ((), ())), preferred_element_type=_F32)
    for hd in range(N_HEADS):
        rows = slice(hd * V_HEAD_DIM, (hd + 1) * V_HEAD_DIM)
        k_ref[0, hd] = k[:, rows].astype(_BF16)
        for c in range(tm // tk):
            vt_ref[0, hd, c, :V_HEAD_DIM, :] = vt[rows, c * tk:(c + 1) * tk].astype(_BF16)
            vt_ref[0, hd, c, V_HEAD_DIM:, :] = jnp.ones((ONES_ROWS, tk), _BF16)


def _kv(x, g, wk, wvt):
    bsz, seq, d = x.shape
    tm, tk = PROJ_TILE, ATTN_TILE
    return pl.pallas_call(
        _kv_kernel,
        grid=(bsz, seq // tm),
        in_specs=[
            pl.BlockSpec((1, tm, d), lambda b, i: (b, i, 0)),
            _const_spec((1, d)), _const_spec((d, d)), _const_spec((d, d)),
        ],
        out_specs=[
            pl.BlockSpec((1, N_HEADS, tm, V_HEAD_DIM), lambda b, i: (b, 0, i, 0)),
            pl.BlockSpec((1, N_HEADS, tm // tk, V_ROWS, tk), lambda b, i: (b, 0, i, 0, 0)),
        ],
        out_shape=[
            jax.ShapeDtypeStruct((bsz, N_HEADS, seq, V_HEAD_DIM), _BF16),
            jax.ShapeDtypeStruct((bsz, N_HEADS, seq // tk, V_ROWS, tk), _BF16),
        ],
        compiler_params=_params("parallel", "parallel"),
        name="kv",
    )(x, g, wk, wvt)


def _project_q(x, g_ref, sc_ref, sh_ref, wqt_ref, q_ref):
    tm = x.shape[0]
    tq = q_ref.shape[-1] // 2
    h = _rms(x) * g_ref[...] * (1.0 + sc_ref[0]) + sh_ref[0]
    qt = lax.dot_general(wqt_ref[...], h.astype(_BF16), (((1,), (1,)), ((), ())),
                         preferred_element_type=_F32) * (HEAD_DIM ** -0.5 * LOG2E)
    first = lax.broadcasted_iota(jnp.int32, (V_HEAD_DIM, tq), 0) < HEAD_DIM
    for hd in range(N_HEADS):
        for c in range(tm // tq):
            qh = qt[hd * V_HEAD_DIM:(hd + 1) * V_HEAD_DIM, c * tq:(c + 1) * tq]
            q_ref[0, hd, c, :, :tq] = jnp.where(first, qh, 0.0).astype(_BF16)
            q_ref[0, hd, c, :, tq:] = jnp.where(first, 0.0, qh).astype(_BF16)


def _bias_kernel(rb_ref, o_ref):
    hd = pl.program_id(0)
    t = o_ref.shape[-1]
    key = lax.broadcasted_iota(jnp.int32, (t, t), 0)
    qry = lax.broadcasted_iota(jnp.int32, (t, t), 1)
    segs = _bucket_segments()
    far = rb_ref[N_BUCKETS - 1, hd]
    for which in range(2):
        rel = qry - key + which * t
        val = jnp.full((t, t), 0.0, _F32)
        for k in range(len(segs) - 2, -1, -1):
            val = jnp.where(rel < segs[k + 1][0], (rb_ref[segs[k][1], hd] - far) * LOG2E, val)
        o_ref[0, which] = jnp.where(rel >= 0, val, MASK_VALUE)


def _bias_tiles(rel_bias):
    t = ATTN_TILE
    return pl.pallas_call(
        _bias_kernel,
        grid=(N_HEADS,),
        in_specs=[pl.BlockSpec(memory_space=pltpu.SMEM)],
        out_specs=pl.BlockSpec((1, 2, t, t), lambda h: (h, 0, 0, 0)),
        out_shape=jax.ShapeDtypeStruct((N_HEADS, 2, t, t), _F32),
        compiler_params=_params("arbitrary"),
        name="bias",
    )(rel_bias)


def _attn_kernel(q_ref, qn_ref, k_ref, vt_ref, bias_ref, lam_ref, sg_ref, o_ref, m_sc, acc_sc, s_a, s_b, x_a, x_b,
                 *, lambda_init):
    t = ATTN_TILE
    dv = V_HEAD_DIM
    heads = q_ref.shape[1]
    i = pl.program_id(2)
    m_sc[...] = jnp.full_like(m_sc, MASK_VALUE)
    acc_sc[...] = jnp.zeros_like(acc_sc)

    def scores_into(buf, j, hh, q_src=q_ref):
        s_buf, x_buf = buf
        rows = pl.ds(pl.multiple_of(j * t, t), t)
        s = _dot(k_ref[0, hh, rows, :], q_src[0, hh, 0])
        s_buf[hh] = s
        x_buf[hh] = jnp.max(s, axis=0, keepdims=True)

    def step(j, which, cur, nxt):
        if nxt is not None:
            scores_into(nxt, j + 1, 0)
        for hh in range(heads):
            if nxt is not None and hh + 1 < heads:
                scores_into(nxt, j + 1, hh + 1)
            s = cur[0][hh]
            if which is None:
                m_cur = cur[1][hh]
            else:
                bias = bias_ref[hh, which]
                s = jnp.concatenate([s[:, :t] + bias, s[:, t:] + bias], axis=1)
                m_cur = jnp.max(s, axis=0, keepdims=True)
            m_prev = m_sc[hh]
            m_new = jnp.maximum(m_prev, m_cur)
            alpha = jnp.exp2(m_prev - m_new)
            p = jnp.exp2(s - m_new)
            acc_sc[hh] = alpha * acc_sc[hh] + _dot(vt_ref[0, hh, j], p.astype(_BF16))
            m_sc[hh] = m_new

    buf_a, buf_b = (s_a, x_a), (s_b, x_b)

    def far_pair(jj, carry):
        ja = 2 * jj
        for hh in range(heads):
            m_prev = m_sc[hh]
            m_new = jnp.maximum(m_prev, jnp.maximum(x_a[hh], x_b[hh]))
            alpha = jnp.exp2(m_prev - m_new)
            pa = jnp.exp2(s_a[hh] - m_new).astype(_BF16)
            scores_into(buf_a, ja + 2, hh)
            pb = jnp.exp2(s_b[hh] - m_new).astype(_BF16)
            scores_into(buf_b, ja + 3, hh)
            vt2 = jnp.concatenate([vt_ref[0, hh, ja], vt_ref[0, hh, ja + 1]], axis=1)
            pv = _dot(vt2, jnp.concatenate([pa, pb], axis=0))
            acc_sc[hh] = alpha * acc_sc[hh] + pv
            m_sc[hh] = m_new
        return carry

    @pl.when(i == 0)
    def _():
        for hh in range(heads):
            scores_into(buf_a, 0, hh)

    lax.fori_loop(0, jnp.maximum((i - 1) // 2, 0), far_pair, 0)

    @pl.when(i % 2 == 0)
    def _():
        @pl.when(i >= 2)
        def _():
            step(i - 2, None, buf_a, None)
            step(i - 1, 1, buf_b, buf_a)

        step(i, 0, buf_a, None)

    @pl.when(i % 2 == 1)
    def _():
        step(i - 1, 1, buf_a, None)
        step(i, 0, buf_b, None)

    for hh in range(heads):
        scores_into(buf_a, 0, hh, qn_ref)
        scores_into(buf_b, 1, hh, qn_ref)

    lv = lam_ref[...]
    lam = (jnp.exp(jnp.sum(lv[0:1] * lv[1:2], axis=-1, keepdims=True))
           - jnp.exp(jnp.sum(lv[2:3] * lv[3:4], axis=-1, keepdims=True)) + lambda_init)
    for hh in range(heads):
        acc = acc_sc[hh]
        o = acc[:dv] / acc[dv:dv + 1]
        o = o[:, :t] - lam * o[:, t:]
        o = o * lax.rsqrt(jnp.mean(o * o, axis=0, keepdims=True) + EPS) * sg_ref[...] * (1.0 - lambda_init)
        o_ref[0, :, hh * dv:(hh + 1) * dv] = o.T.astype(o_ref.dtype)


def _attention(qt, k, vt, bias, lam, sg, lambda_init):
    bsz, n_heads, seq, dv = k.shape
    t = ATTN_TILE
    nt = seq // t
    hp = ATTN_HEADS_PER_STEP
    single = dict(pipeline_mode=pl.Buffered(1))
    return pl.pallas_call(
        functools.partial(_attn_kernel, lambda_init=lambda_init),
        grid=(bsz, n_heads // hp, nt),
        in_specs=[
            pl.BlockSpec((1, hp, 1, dv, 2 * t), lambda b, h, i: (b, h, i, 0, 0)),
            pl.BlockSpec((1, hp, 1, dv, 2 * t), lambda b, h, i: (b, h, jnp.minimum(i + 1, nt - 1), 0, 0)),
            pl.BlockSpec((1, hp, seq, dv), lambda b, h, i: (b, h, 0, 0), **single),
            pl.BlockSpec((1, hp, nt, V_ROWS, t), lambda b, h, i: (b, h, 0, 0, 0), **single),
            pl.BlockSpec((hp, 2, t, t), lambda b, h, i: (h, 0, 0, 0), **single),
            pl.BlockSpec((4, HEAD_DIM), lambda b, h, i: (0, 0)),
            pl.BlockSpec((dv, 1), lambda b, h, i: (0, 0)),
        ],
        out_specs=pl.BlockSpec((1, t, hp * dv), lambda b, h, i: (b, i, h)),
        out_shape=jax.ShapeDtypeStruct((bsz, seq, n_heads * dv), _BF16),
        scratch_shapes=[pltpu.VMEM((hp, 1, 2 * t), _F32), pltpu.VMEM((hp, V_ROWS, 2 * t), _F32),
                        pltpu.VMEM((hp, t, 2 * t), _F32), pltpu.VMEM((hp, t, 2 * t), _F32),
                        pltpu.VMEM((hp, 1, 2 * t), _F32), pltpu.VMEM((hp, 1, 2 * t), _F32)],
        compiler_params=_params("parallel", "parallel", "arbitrary"),
        name="attn",
    )(qt, qt, k, vt, bias, lam, sg)


def kernel(x, c, mod_w, mod_b, norm_g, cm_w1, cm_b1, cm_dw, cm_dwb, cm_ln_g, cm_ln_b, cm_w2, cm_b2, kv_norm_g,
           w_k, w_v, w_q, lam, subln_g, w_o, rel_bias, ffn_w_in, ffn_dw, ffn_dwb, ffn_w_out):
    depth = mod_w.shape[0]
    n_conv = cm_w1.shape[0]
    bsz, seq, d = x.shape
    assert 1 <= n_conv < depth, "the first attention layer's Q is produced by the preceding layer's FFN"

    mod = _modulation(c, mod_w, mod_b)

    def row(v):
        return v.reshape(1, -1)

    def mod_vectors(l):
        return [m.reshape(bsz, 1, d) for m in jnp.split(mod[l], 6, axis=-1)]

    kk = vt = bias = qt = None
    for l in range(depth):
        sh_m, sc_m, g_m, sh_f, sc_f, g_f = mod_vectors(l)
        g0, g1, g2, g3 = [row(norm_g[l, k]) for k in range(4)]
        oproj = None
        if l < n_conv:
            u = _conv_a(x, g0, sc_m, sh_m, cm_w1[l].astype(_BF16), row(cm_b1[l]))
            x = _conv_b(x, u, cm_dw[l], row(cm_dwb[l]), row(cm_ln_g[l]), row(cm_ln_b[l]),
                        cm_w2[l].astype(_BF16), row(cm_b2[l]), g1, g_m)
        else:
            if l == n_conv:
                kk, vt = _kv(x, row(kv_norm_g), w_k.astype(_BF16), w_v.T.astype(_BF16))
                bias = _bias_tiles(rel_bias)
            j = l - n_conv
            lambda_init = 0.8 - 0.6 * math.exp(-0.3 * l)
            a = _attention(qt, kk, vt, bias, lam[j], subln_g[j].reshape(-1, 1), lambda_init)
            oproj = (a, w_o[j].astype(_BF16), g1, g_m)
        qproj = None
        if n_conv <= l + 1 < depth:
            nsh_m, nsc_m, _, _, _, _ = mod_vectors(l + 1)
            qproj = (row(norm_g[l + 1, 0]), nsc_m, nsh_m, w_q[l + 1 - n_conv].T.astype(_BF16))
        res = _ffn(x, g2, sc_f, sh_f, ffn_w_in[l].astype(_BF16), ffn_dw[l], row(ffn_dwb[l]),
                   ffn_w_out[l].astype(_BF16), g3, g_f, oproj=oproj, qproj=qproj)
        x, qt = res if qproj is not None else (res, None)
    return x
```

```python
import functools
import math

import jax
import jax.numpy as jnp
from jax import lax
from jax.experimental import pallas as pl
from jax.experimental.pallas import tpu as pltpu

N_HEADS = 8
HEAD_DIM = 64
V_HEAD_DIM = 128
CONV_WIDTH = 31
FFN_CONV_WIDTH = 3
N_BUCKETS = 32
MAX_DISTANCE = 128
MAX_EXACT = 16
EPS = 1e-6

ROW_TILE = 512
PROJ_TILE = 1024
ATTN_TILE = 256
ATTN_HEADS_PER_STEP = 8
ONES_ROWS = 16
V_ROWS = V_HEAD_DIM + ONES_ROWS
LOG2E = math.log2(math.e)
CONV_HALO = 32
CONV_ROWS = 256
LANES = 128
FFN_CHUNK = 256
FFN_HALO = 8
MASK_VALUE = -1e30
VMEM_LIMIT = 56 * 1024 * 1024

_BF16 = jnp.bfloat16
_F32 = jnp.float32


def _bucket_of_distance(n):
    if n < MAX_EXACT:
        return n
    large = MAX_EXACT + int(math.log(n / MAX_EXACT) / math.log(MAX_DISTANCE / MAX_EXACT) * (N_BUCKETS - MAX_EXACT))
    return min(large, N_BUCKETS - 1)


def _bucket_segments():
    segs = []
    for n in range(MAX_DISTANCE):
        b = _bucket_of_distance(n)
        if not segs or segs[-1][1] != b:
            segs.append((n, b))
    assert all(_bucket_of_distance(n) == N_BUCKETS - 1 for n in range(segs[-1][0], 4 * MAX_DISTANCE))
    return segs


def _rms(x):
    return x * lax.rsqrt(jnp.mean(x * x, axis=-1, keepdims=True) + EPS)


def _sigmoid(x):
    return 0.5 * jnp.tanh(0.5 * x) + 0.5


def _swish(x):
    h = 0.5 * x
    return h + h * jnp.tanh(h)


def _dot(a, b):
    return jnp.dot(a, b, preferred_element_type=_F32)


def _params(*sem):
    return pltpu.CompilerParams(dimension_semantics=sem, vmem_limit_bytes=VMEM_LIMIT)


def _const_spec(shape):
    nd = len(shape)
    return pl.BlockSpec(shape, lambda *_: (0,) * nd)


def _mod_kernel(c_ref, w_ref, b_ref, o_ref):
    o_ref[0] = _dot(_swish(c_ref[...]), w_ref[0]) + b_ref[0]


def _modulation(c, mod_w, mod_b):
    depth, d, n = mod_w.shape
    bsz = c.shape[0]
    tn = 1024
    return pl.pallas_call(
        _mod_kernel,
        grid=(depth, n // tn),
        in_specs=[
            pl.BlockSpec((bsz, d), lambda l, j: (0, 0)),
            pl.BlockSpec((1, d, tn), lambda l, j: (l, 0, j)),
            pl.BlockSpec((1, 1, tn), lambda l, j: (l, 0, j)),
        ],
        out_specs=pl.BlockSpec((1, bsz, tn), lambda l, j: (l, 0, j)),
        out_shape=jax.ShapeDtypeStruct((depth, bsz, n), _F32),
        compiler_params=_params("arbitrary", "arbitrary"),
        name="mod",
    )(c, mod_w, mod_b.reshape(depth, 1, n))


def _conv_a_kernel(x_ref, g_ref, sc_ref, sh_ref, w1_ref, b1_ref, u_ref):
    d = x_ref.shape[-1]
    h = _rms(x_ref[0]) * g_ref[...] * (1.0 + sc_ref[0]) + sh_ref[0]
    hb = h.astype(_BF16)
    a = _dot(hb, w1_ref[:, :d]) + b1_ref[:, :d]
    gt = _dot(hb, w1_ref[:, d:]) + b1_ref[:, d:]
    u_ref[0] = a * _sigmoid(gt)


def _conv_a(x, g, sc, sh, w1, b1):
    bsz, seq, d = x.shape
    tm = PROJ_TILE
    vec = pl.BlockSpec((1, 1, d), lambda b, i: (b, 0, 0))
    return pl.pallas_call(
        _conv_a_kernel,
        grid=(bsz, seq // tm),
        in_specs=[
            pl.BlockSpec((1, tm, d), lambda b, i: (b, i, 0)),
            _const_spec((1, d)), vec, vec,
            _const_spec((d, 2 * d)), _const_spec((1, 2 * d)),
        ],
        out_specs=pl.BlockSpec((1, tm, d), lambda b, i: (b, i, 0)),
        out_shape=jax.ShapeDtypeStruct((bsz, seq, d), _F32),
        compiler_params=_params("parallel", "parallel"),
        name="conv_a",
    )(x, g, sc, sh, w1, b1)


def _conv_b_kernel(x_ref, ucur_ref, uprev_ref, dw_ref, dwb_ref, lng_ref, lnb_ref, w2_ref, b2_ref,
                   g_ref, gate_ref, o_ref, ubuf, cbuf):
    tm, d = ucur_ref.shape[1], ucur_ref.shape[2]
    i = pl.program_id(1)
    for c in range(d // LANES):
        lanes = slice(c * LANES, (c + 1) * LANES)
        prev = uprev_ref[0, :, lanes]
        ubuf[c, 0:CONV_HALO, :] = jnp.where(i > 0, prev, jnp.zeros_like(prev))
        ubuf[c, CONV_HALO:, :] = ucur_ref[0, :, lanes]

    first_shift = CONV_HALO - (CONV_WIDTH - 1)

    def row_block(r, carry):
        r0 = pl.multiple_of(r * CONV_ROWS, CONV_ROWS)
        for c in range(d // LANES):
            lanes = slice(c * LANES, (c + 1) * LANES)
            acc = jnp.broadcast_to(dwb_ref[:, lanes], (CONV_ROWS, LANES))
            for j in range(CONV_WIDTH):
                acc = acc + ubuf[c, pl.ds(r0 + first_shift + j, CONV_ROWS), :] * dw_ref[j:j + 1, lanes]
            cbuf[pl.ds(r0, CONV_ROWS), lanes] = acc
        return carry

    lax.fori_loop(0, tm // CONV_ROWS, row_block, 0)

    cv = cbuf[...]
    mu = jnp.mean(cv, axis=-1, keepdims=True)
    cc = cv - mu
    var = jnp.mean(cc * cc, axis=-1, keepdims=True)
    z = cc * lax.rsqrt(var + EPS) * lng_ref[...] + lnb_ref[...]
    y = _dot(_swish(z).astype(_BF16), w2_ref[...]) + b2_ref[...]
    o_ref[0] = x_ref[0] + gate_ref[0] * (_rms(y) * g_ref[...])


def _conv_b(x, u, dw, dwb, lng, lnb, w2, b2, g, gate):
    bsz, seq, d = x.shape
    tm = PROJ_TILE
    ratio = tm // CONV_HALO
    vec = pl.BlockSpec((1, 1, d), lambda b, i: (b, 0, 0))
    tile = pl.BlockSpec((1, tm, d), lambda b, i: (b, i, 0))
    return pl.pallas_call(
        _conv_b_kernel,
        grid=(bsz, seq // tm),
        in_specs=[
            tile, tile,
            pl.BlockSpec((1, CONV_HALO, d), lambda b, i: (b, jnp.maximum(i * ratio - 1, 0), 0)),
            _const_spec((CONV_WIDTH, d)), _const_spec((1, d)), _const_spec((1, d)), _const_spec((1, d)),
            _const_spec((d, d)), _const_spec((1, d)), _const_spec((1, d)), vec,
        ],
        out_specs=tile,
        out_shape=jax.ShapeDtypeStruct((bsz, seq, d), _F32),
        scratch_shapes=[pltpu.VMEM((d // LANES, tm + CONV_HALO, LANES), _F32), pltpu.VMEM((tm, d), _F32)],
        compiler_params=_params("parallel", "parallel"),
        name="conv_b",
    )(x, u, u, dw, dwb, lng, lnb, w2, b2, g, gate)


def _ffn_kernel(*refs, has_oproj, has_q):
    refs = list(refs)
    x_ref = refs.pop(0)
    if has_oproj:
        a_ref, wo_ref, g1_ref, gatem_ref = refs[:4]
        refs = refs[4:]
    g_ref, sc_ref, sh_ref, win_ref, dw_ref, dwb_ref, wout_ref, g3_ref, gate_ref = refs[:9]
    refs = refs[9:]
    if has_q:
        gq_ref, scq_ref, shq_ref, wqt_ref = refs[:4]
        refs = refs[4:]
    o_ref = refs.pop(0)
    if has_q:
        q_ref = refs.pop(0)
    hbuf, ubuf, carry, pbuf = refs[:4]
    tm = x_ref.shape[1]
    f = wout_ref.shape[0]
    i = pl.program_id(1)
    slabs = FFN_CHUNK // LANES

    @pl.when(i == 0)
    def _():
        carry[...] = jnp.zeros_like(carry)

    if has_oproj:
        xmid = refs[4]
        xmid[...] = x_ref[0] + gatem_ref[0] * (_rms(_dot(a_ref[0], wo_ref[...])) * g1_ref[...])
        x_in = xmid
    else:
        x_in = x_ref.at[0]

    h = _rms(x_in[...]) * g_ref[...] * (1.0 + sc_ref[0]) + sh_ref[0]
    hbuf[...] = h.astype(_BF16)

    def up_project(col0, slab0):
        u = _dot(hbuf[...], win_ref[:, col0:col0 + FFN_CHUNK])
        for k in range(slabs):
            cols = slice(col0 + k * LANES, col0 + (k + 1) * LANES)
            uk = u[:, k * LANES:(k + 1) * LANES]
            ubuf[slab0 + k, 0:FFN_HALO, :] = carry[:, cols]
            ubuf[slab0 + k, FFN_HALO:, :] = uk
            carry[:, cols] = uk[tm - FFN_HALO:, :]

    def conv3(col0, slab):
        cols = slice(col0, col0 + LANES)
        out = dwb_ref[:, cols]
        for j in range(FFN_CONV_WIDTH):
            shift = FFN_CONV_WIDTH - 1 - j
            out = out + ubuf[slab, pl.ds(FFN_HALO - shift, tm), :] * dw_ref[j:j + 1, cols]
        return out

    for idx, c0 in enumerate(range(0, f, FFN_CHUNK)):
        base = (idx % 2) * 2 * slabs
        up_project(c0, base)
        up_project(f + c0, base + slabs)
        for k in range(slabs):
            gk = conv3(c0 + k * LANES, base + k)
            vk = conv3(f + c0 + k * LANES, base + slabs + k)
            pbuf[:, c0 + k * LANES:c0 + (k + 1) * LANES] = (_swish(gk) * vk).astype(_BF16)

    y = _dot(pbuf[...], wout_ref[...])
    x_out = x_in[...] + gate_ref[0] * (_rms(y) * g3_ref[...])
    o_ref[0] = x_out
    if has_q:
        _project_q(x_out, gq_ref, scq_ref, shq_ref, wqt_ref, q_ref)


def _ffn(x, g, sc, sh, w_in, dw, dwb, w_out, g3, gate, oproj=None, qproj=None):
    bsz, seq, d = x.shape
    f = w_out.shape[0]
    tm = ROW_TILE
    vec = pl.BlockSpec((1, 1, d), lambda b, i: (b, 0, 0))
    tile = pl.BlockSpec((1, tm, d), lambda b, i: (b, i, 0))
    single = dict(pipeline_mode=pl.Buffered(1))
    operands, in_specs = [x], [tile]
    scratch = [
        pltpu.VMEM((tm, d), _BF16),
        pltpu.VMEM((4 * (FFN_CHUNK // LANES), tm + FFN_HALO, LANES), _F32),
        pltpu.VMEM((FFN_HALO, 2 * f), _F32),
        pltpu.VMEM((tm, f), _BF16),
    ]
    if oproj is not None:
        operands += list(oproj)
        in_specs += [tile, pl.BlockSpec((d, d), lambda b, i: (0, 0), **single), _const_spec((1, d)), vec]
        scratch.append(pltpu.VMEM((tm, d), _F32))
    operands += [g, sc, sh, w_in, dw, dwb, w_out, g3, gate]
    in_specs += [
        _const_spec((1, d)), vec, vec,
        pl.BlockSpec((d, 2 * f), lambda b, i: (0, 0), **single),
        _const_spec((FFN_CONV_WIDTH, 2 * f)), _const_spec((1, 2 * f)),
        pl.BlockSpec((f, d), lambda b, i: (0, 0), **single),
        _const_spec((1, d)), vec,
    ]
    out_specs, out_shape = [tile], [jax.ShapeDtypeStruct((bsz, seq, d), _F32)]
    if qproj is not None:
        tq = ATTN_TILE
        operands += list(qproj)
        in_specs += [_const_spec((1, d)), vec, vec, pl.BlockSpec((d, d), lambda b, i: (0, 0), **single)]
        out_specs.append(pl.BlockSpec((1, N_HEADS, tm // tq, V_HEAD_DIM, 2 * tq), lambda b, i: (b, 0, i, 0, 0)))
        out_shape.append(jax.ShapeDtypeStruct((bsz, N_HEADS, seq // tq, V_HEAD_DIM, 2 * tq), _BF16))
    outs = pl.pallas_call(
        functools.partial(_ffn_kernel, has_oproj=oproj is not None, has_q=qproj is not None),
        grid=(bsz, seq // tm),
        in_specs=in_specs,
        out_specs=out_specs,
        out_shape=out_shape,
        scratch_shapes=scratch,
        compiler_params=_params("arbitrary", "arbitrary"),
        name="ffn",
    )(*operands)
    return outs if qproj is not None else outs[0]


def _kv_kernel(x_ref, g_ref, wk_ref, wvt_ref, k_ref, vt_ref):
    tm = x_ref.shape[1]
    tk = vt_ref.shape[-1]
    hb = (_rms(x_ref[0]) * g_ref[...]).astype(_BF16)
    k = _dot(hb, wk_ref[...])
    vt = lax.dot_general(wvt_ref[...], hb, (((1,), (1,)), ((), ())), preferred_element_type=_F32)
    for hd in range(N_HEADS):
        rows = slice(hd * V_HEAD_DIM, (hd + 1) * V_HEAD_DIM)
        k_ref[0, hd] = k[:, rows].astype(_BF16)
        for c in range(tm // tk):
            vt_ref[0, hd, c, :V_HEAD_DIM, :] = vt[rows, c * tk:(c + 1) * tk].astype(_BF16)
            vt_ref[0, hd, c, V_HEAD_DIM:, :] = jnp.ones((ONES_ROWS, tk), _BF16)


def _kv(x, g, wk, wvt):
    bsz, seq, d = x.shape
    tm, tk = PROJ_TILE, ATTN_TILE
    return pl.pallas_call(
        _kv_kernel,
        grid=(bsz, seq // tm),
        in_specs=[
            pl.BlockSpec((1, tm, d), lambda b, i: (b, i, 0)),
            _const_spec((1, d)), _const_spec((d, d)), _const_spec((d, d)),
        ],
        out_specs=[
            pl.BlockSpec((1, N_HEADS, tm, V_HEAD_DIM), lambda b, i: (b, 0, i, 0)),
            pl.BlockSpec((1, N_HEADS, tm // tk, V_ROWS, tk), lambda b, i: (b, 0, i, 0, 0)),
        ],
        out_shape=[
            jax.ShapeDtypeStruct((bsz, N_HEADS, seq, V_HEAD_DIM), _BF16),
            jax.ShapeDtypeStruct((bsz, N_HEADS, seq // tk, V_ROWS, tk), _BF16),
        ],
        compiler_params=_params("parallel", "parallel"),
        name="kv",
    )(x, g, wk, wvt)


def _project_q(x, g_ref, sc_ref, sh_ref, wqt_ref, q_ref):
    tm = x.shape[0]
    tq = q_ref.shape[-1] // 2
    h = _rms(x) * g_ref[...] * (1.0 + sc_ref[0]) + sh_ref[0]
    qt = lax.dot_general(wqt_ref[...], h.astype(_BF16), (((1,), (1,)), ((), ())),
                         preferred_element_type=_F32) * (HEAD_DIM ** -0.5 * LOG2E)
    first = lax.broadcasted_iota(jnp.int32, (V_HEAD_DIM, tq), 0) < HEAD_DIM
    for hd in range(N_HEADS):
        for c in range(tm // tq):
            qh = qt[hd * V_HEAD_DIM:(hd + 1) * V_HEAD_DIM, c * tq:(c + 1) * tq]
            q_ref[0, hd, c, :, :tq] = jnp.where(first, qh, 0.0).astype(_BF16)
            q_ref[0, hd, c, :, tq:] = jnp.where(first, 0.0, qh).astype(_BF16)


def _bias_kernel(rb_ref, o_ref):
    hd = pl.program_id(0)
    t = o_ref.shape[-1]
    key = lax.broadcasted_iota(jnp.int32, (t, t), 0)
    qry = lax.broadcasted_iota(jnp.int32, (t, t), 1)
    segs = _bucket_segments()
    far = rb_ref[N_BUCKETS - 1, hd]
    for which in range(2):
        rel = qry - key + which * t
        val = jnp.full((t, t), 0.0, _F32)
        for k in range(len(segs) - 2, -1, -1):
            val = jnp.where(rel < segs[k + 1][0], (rb_ref[segs[k][1], hd] - far) * LOG2E, val)
        o_ref[0, which] = jnp.where(rel >= 0, val, MASK_VALUE)


def _bias_tiles(rel_bias):
    t = ATTN_TILE
    return pl.pallas_call(
        _bias_kernel,
        grid=(N_HEADS,),
        in_specs=[pl.BlockSpec(memory_space=pltpu.SMEM)],
        out_specs=pl.BlockSpec((1, 2, t, t), lambda h: (h, 0, 0, 0)),
        out_shape=jax.ShapeDtypeStruct((N_HEADS, 2, t, t), _F32),
        compiler_params=_params("arbitrary"),
        name="bias",
    )(rel_bias)


def _attn_kernel(q_ref, qn_ref, k_ref, vt_ref, bias_ref, lam_ref, sg_ref, o_ref, m_sc, acc_sc, s_a, s_b, x_a, x_b,
                 *, lambda_init):
    t = ATTN_TILE
    dv = V_HEAD_DIM
    heads = q_ref.shape[1]
    i = pl.program_id(2)
    m_sc[...] = jnp.full_like(m_sc, MASK_VALUE)
    acc_sc[...] = jnp.zeros_like(acc_sc)

    def scores_into(buf, j, hh, q_src=q_ref):
        s_buf, x_buf = buf
        rows = pl.ds(pl.multiple_of(j * t, t), t)
        s = _dot(k_ref[0, hh, rows, :], q_src[0, hh, 0])
        s_buf[hh] = s
        x_buf[hh] = jnp.max(s, axis=0, keepdims=True)

    def step(j, which, cur, nxt):
        if nxt is not None:
            scores_into(nxt, j + 1, 0)
        for hh in range(heads):
            if nxt is not None and hh + 1 < heads:
                scores_into(nxt, j + 1, hh + 1)
            s = cur[0][hh]
            if which is None:
                m_cur = cur[1][hh]
            else:
                bias = bias_ref[hh, which]
                s = jnp.concatenate([s[:, :t] + bias, s[:, t:] + bias], axis=1)
                m_cur = jnp.max(s, axis=0, keepdims=True)
            m_prev = m_sc[hh]
            m_new = jnp.maximum(m_prev, m_cur)
            alpha = jnp.exp2(m_prev - m_new)
            p = jnp.exp2(s - m_new)
            acc_sc[hh] = alpha * acc_sc[hh] + _dot(vt_ref[0, hh, j], p.astype(_BF16))
            m_sc[hh] = m_new

    buf_a, buf_b = (s_a, x_a), (s_b, x_b)

    def far_pair(jj, carry):
        ja = 2 * jj
        for hh in range(heads):
            m_prev = m_sc[hh]
            m_new = jnp.maximum(m_prev, jnp.maximum(x_a[hh], x_b[hh]))
            alpha = jnp.exp2(m_prev - m_new)
            pa = jnp.exp2(s_a[hh] - m_new).astype(_BF16)
            scores_into(buf_a, ja + 2, hh)
            pb = jnp.exp2(s_b[hh] - m_new).astype(_BF16)
            scores_into(buf_b, ja + 3, hh)
            vt2 = jnp.concatenate([vt_ref[0, hh, ja], vt_ref[0, hh, ja + 1]], axis=1)
            pv = _dot(vt2, jnp.concatenate([pa, pb], axis=0))
            acc_sc[hh] = alpha * acc_sc[hh] + pv
            m_sc[hh] = m_new
        return carry

    @pl.when(i == 0)
    def _():
        for hh in range(heads):
            scores_into(buf_a, 0, hh)

    lax.fori_loop(0, jnp.maximum((i - 1) // 2, 0), far_pair, 0)

    @pl.when(i % 2 == 0)
    def _():
        @pl.when(i >= 2)
        def _():
            step(i - 2, None, buf_a, None)
            step(i - 1, 1, buf_b, buf_a)

        step(i, 0, buf_a, None)

    @pl.when(i % 2 == 1)
    def _():
        step(i - 1, 1, buf_a, None)
        step(i, 0, buf_b, None)

    for hh in range(heads):
        scores_into(buf_a, 0, hh, qn_ref)
        scores_into(buf_b, 1, hh, qn_ref)

    lv = lam_ref[...]
    lam = (jnp.exp(jnp.sum(lv[0:1] * lv[1:2], axis=-1, keepdims=True))
           - jnp.exp(jnp.sum(lv[2:3] * lv[3:4], axis=-1, keepdims=True)) + lambda_init)
    for hh in range(heads):
        acc = acc_sc[hh]
        o = acc[:dv] / acc[dv:dv + 1]
        o = o[:, :t] - lam * o[:, t:]
        o = o * lax.rsqrt(jnp.mean(o * o, axis=0, keepdims=True) + EPS) * sg_ref[...] * (1.0 - lambda_init)
        o_ref[0, :, hh * dv:(hh + 1) * dv] = o.T.astype(o_ref.dtype)


def _attention(qt, k, vt, bias, lam, sg, lambda_init):
    bsz, n_heads, seq, dv = k.shape
    t = ATTN_TILE
    nt = seq // t
    hp = ATTN_HEADS_PER_STEP
    single = dict(pipeline_mode=pl.Buffered(1))
    return pl.pallas_call(
        functools.partial(_attn_kernel, lambda_init=lambda_init),
        grid=(bsz, n_heads // hp, nt),
        in_specs=[
            pl.BlockSpec((1, hp, 1, dv, 2 * t), lambda b, h, i: (b, h, i, 0, 0)),
            pl.BlockSpec((1, hp, 1, dv, 2 * t), lambda b, h, i: (b, h, jnp.minimum(i + 1, nt - 1), 0, 0)),
            pl.BlockSpec((1, hp, seq, dv), lambda b, h, i: (b, h, 0, 0), **single),
            pl.BlockSpec((1, hp, nt, V_ROWS, t), lambda b, h, i: (b, h, 0, 0, 0), **single),
            pl.BlockSpec((hp, 2, t, t), lambda b, h, i: (h, 0, 0, 0), **single),
            pl.BlockSpec((4, HEAD_DIM), lambda b, h, i: (0, 0)),
            pl.BlockSpec((dv, 1), lambda b, h, i: (0, 0)),
        ],
        out_specs=pl.BlockSpec((1, t, hp * dv), lambda b, h, i: (b, i, h)),
        out_shape=jax.ShapeDtypeStruct((bsz, seq, n_heads * dv), _BF16),
        scratch_shapes=[pltpu.VMEM((hp, 1, 2 * t), _F32), pltpu.VMEM((hp, V_ROWS, 2 * t), _F32),
                        pltpu.VMEM((hp, t, 2 * t), _F32), pltpu.VMEM((hp, t, 2 * t), _F32),
                        pltpu.VMEM((hp, 1, 2 * t), _F32), pltpu.VMEM((hp, 1, 2 * t), _F32)],
        compiler_params=_params("parallel", "parallel", "arbitrary"),
        name="attn",
    )(qt, qt, k, vt, bias, lam, sg)


def kernel(x, c, mod_w, mod_b, norm_g, cm_w1, cm_b1, cm_dw, cm_dwb, cm_ln_g, cm_ln_b, cm_w2, cm_b2, kv_norm_g,
           w_k, w_v, w_q, lam, subln_g, w_o, rel_bias, ffn_w_in, ffn_dw, ffn_dwb, ffn_w_out):
    depth = mod_w.shape[0]
    n_conv = cm_w1.shape[0]
    bsz, seq, d = x.shape
    assert 1 <= n_conv < depth, "the first attention layer's Q is produced by the preceding layer's FFN"

    mod = _modulation(c, mod_w, mod_b)

    def row(v):
        return v.reshape(1, -1)

    def mod_vectors(l):
        return [m.reshape(bsz, 1, d) for m in jnp.split(mod[l], 6, axis=-1)]

    kk = vt = bias = qt = None
    for l in range(depth):
        sh_m, sc_m, g_m, sh_f, sc_f, g_f = mod_vectors(l)
        g0, g1, g2, g3 = [row(norm_g[l, k]) for k in range(4)]
        oproj = None
        if l < n_conv:
            u = _conv_a(x, g0, sc_m, sh_m, cm_w1[l].astype(_BF16), row(cm_b1[l]))
            x = _conv_b(x, u, cm_dw[l], row(cm_dwb[l]), row(cm_ln_g[l]), row(cm_ln_b[l]),
                        cm_w2[l].astype(_BF16), row(cm_b2[l]), g1, g_m)
        else:
            if l == n_conv:
                kk, vt = _kv(x, row(kv_norm_g), w_k.astype(_BF16), w_v.T.astype(_BF16))
                bias = _bias_tiles(rel_bias)
            j = l - n_conv
            lambda_init = 0.8 - 0.6 * math.exp(-0.3 * l)
            a = _attention(qt, kk, vt, bias, lam[j], subln_g[j].reshape(-1, 1), lambda_init)
            oproj = (a, w_o[j].astype(_BF16), g1, g_m)
        qproj = None
        if n_conv <= l + 1 < depth:
            nsh_m, nsc_m, _, _, _, _ = mod_vectors(l + 1)
            qproj = (row(norm_g[l + 1, 0]), nsc_m, nsh_m, w_q[l + 1 - n_conv].T.astype(_BF16))
        res = _ffn(x, g2, sc_f, sh_f, ffn_w_in[l].astype(_BF16), ffn_dw[l], row(ffn_dwb[l]),
                   ffn_w_out[l].astype(_BF16), g3, g_f, oproj=oproj, qproj=qproj)
        x, qt = res if qproj is not None else (res, None)
    return x
```

```python
import functools
import math

import jax
import jax.numpy as jnp
from jax import lax
from jax.experimental import pallas as pl
from jax.experimental.pallas import tpu as pltpu

N_HEADS = 8
HEAD_DIM = 64
V_HEAD_DIM = 128
CONV_WIDTH = 31
FFN_CONV_WIDTH = 3
N_BUCKETS = 32
MAX_DISTANCE = 128
MAX_EXACT = 16
EPS = 1e-6

ROW_TILE = 512
PROJ_TILE = 1024
ATTN_TILE = 256
ATTN_HEADS_PER_STEP = 8
ONES_ROWS = 16
V_ROWS = V_HEAD_DIM + ONES_ROWS
LOG2E = math.log2(math.e)
CONV_HALO = 32
CONV_ROWS = 256
LANES = 128
FFN_CHUNK = 256
FFN_HALO = 8
MASK_VALUE = -1e30
VMEM_LIMIT = 56 * 1024 * 1024

_BF16 = jnp.bfloat16
_F32 = jnp.float32


def _bucket_of_distance(n):
    if n < MAX_EXACT:
        return n
    large = MAX_EXACT + int(math.log(n / MAX_EXACT) / math.log(MAX_DISTANCE / MAX_EXACT) * (N_BUCKETS - MAX_EXACT))
    return min(large, N_BUCKETS - 1)


def _bucket_segments():
    segs = []
    for n in range(MAX_DISTANCE):
        b = _bucket_of_distance(n)
        if not segs or segs[-1][1] != b:
            segs.append((n, b))
    assert all(_bucket_of_distance(n) == N_BUCKETS - 1 for n in range(segs[-1][0], 4 * MAX_DISTANCE))
    return segs


def _rms(x):
    return x * lax.rsqrt(jnp.mean(x * x, axis=-1, keepdims=True) + EPS)


def _sigmoid(x):
    return 0.5 * jnp.tanh(0.5 * x) + 0.5


def _swish(x):
    h = 0.5 * x
    return h + h * jnp.tanh(h)


def _dot(a, b):
    return jnp.dot(a, b, preferred_element_type=_F32)


def _params(*sem):
    return pltpu.CompilerParams(dimension_semantics=sem, vmem_limit_bytes=VMEM_LIMIT)


def _const_spec(shape):
    nd = len(shape)
    return pl.BlockSpec(shape, lambda *_: (0,) * nd)


def _mod_kernel(c_ref, w_ref, b_ref, o_ref):
    o_ref[0] = _dot(_swish(c_ref[...]), w_ref[0]) + b_ref[0]


def _modulation(c, mod_w, mod_b):
    depth, d, n = mod_w.shape
    bsz = c.shape[0]
    tn = 1024
    return pl.pallas_call(
        _mod_kernel,
        grid=(depth, n // tn),
        in_specs=[
            pl.BlockSpec((bsz, d), lambda l, j: (0, 0)),
            pl.BlockSpec((1, d, tn), lambda l, j: (l, 0, j)),
            pl.BlockSpec((1, 1, tn), lambda l, j: (l, 0, j)),
        ],
        out_specs=pl.BlockSpec((1, bsz, tn), lambda l, j: (l, 0, j)),
        out_shape=jax.ShapeDtypeStruct((depth, bsz, n), _F32),
        compiler_params=_params("arbitrary", "arbitrary"),
        name="mod",
    )(c, mod_w, mod_b.reshape(depth, 1, n))


def _conv_a_kernel(x_ref, g_ref, sc_ref, sh_ref, w1_ref, b1_ref, u_ref):
    d = x_ref.shape[-1]
    h = _rms(x_ref[0]) * g_ref[...] * (1.0 + sc_ref[0]) + sh_ref[0]
    hb = h.astype(_BF16)
    a = _dot(hb, w1_ref[:, :d]) + b1_ref[:, :d]
    gt = _dot(hb, w1_ref[:, d:]) + b1_ref[:, d:]
    u_ref[0] = a * _sigmoid(gt)


def _conv_a(x, g, sc, sh, w1, b1):
    bsz, seq, d = x.shape
    tm = PROJ_TILE
    vec = pl.BlockSpec((1, 1, d), lambda b, i: (b, 0, 0))
    return pl.pallas_call(
        _conv_a_kernel,
        grid=(bsz, seq // tm),
        in_specs=[
            pl.BlockSpec((1, tm, d), lambda b, i: (b, i, 0)),
            _const_spec((1, d)), vec, vec,
            _const_spec((d, 2 * d)), _const_spec((1, 2 * d)),
        ],
        out_specs=pl.BlockSpec((1, tm, d), lambda b, i: (b, i, 0)),
        out_shape=jax.ShapeDtypeStruct((bsz, seq, d), _F32),
        compiler_params=_params("parallel", "parallel"),
        name="conv_a",
    )(x, g, sc, sh, w1, b1)


def _conv_b_kernel(x_ref, ucur_ref, uprev_ref, dw_ref, dwb_ref, lng_ref, lnb_ref, w2_ref, b2_ref,
                   g_ref, gate_ref, o_ref, ubuf, cbuf):
    tm, d = ucur_ref.shape[1], ucur_ref.shape[2]
    i = pl.program_id(1)
    for c in range(d // LANES):
        lanes = slice(c * LANES, (c + 1) * LANES)
        prev = uprev_ref[0, :, lanes]
        ubuf[c, 0:CONV_HALO, :] = jnp.where(i > 0, prev, jnp.zeros_like(prev))
        ubuf[c, CONV_HALO:, :] = ucur_ref[0, :, lanes]

    first_shift = CONV_HALO - (CONV_WIDTH - 1)

    def row_block(r, carry):
        r0 = pl.multiple_of(r * CONV_ROWS, CONV_ROWS)
        for c in range(d // LANES):
            lanes = slice(c * LANES, (c + 1) * LANES)
            acc = jnp.broadcast_to(dwb_ref[:, lanes], (CONV_ROWS, LANES))
            for j in range(CONV_WIDTH):
                acc = acc + ubuf[c, pl.ds(r0 + first_shift + j, CONV_ROWS), :] * dw_ref[j:j + 1, lanes]
            cbuf[pl.ds(r0, CONV_ROWS), lanes] = acc
        return carry

    lax.fori_loop(0, tm // CONV_ROWS, row_block, 0)

    cv = cbuf[...]
    mu = jnp.mean(cv, axis=-1, keepdims=True)
    cc = cv - mu
    var = jnp.mean(cc * cc, axis=-1, keepdims=True)
    z = cc * lax.rsqrt(var + EPS) * lng_ref[...] + lnb_ref[...]
    y = _dot(_swish(z).astype(_BF16), w2_ref[...]) + b2_ref[...]
    o_ref[0] = x_ref[0] + gate_ref[0] * (_rms(y) * g_ref[...])


def _conv_b(x, u, dw, dwb, lng, lnb, w2, b2, g, gate):
    bsz, seq, d = x.shape
    tm = PROJ_TILE
    assert seq % tm == 0 and tm % CONV_ROWS == 0 and d % LANES == 0 and CONV_WIDTH - 1 <= CONV_HALO
    ratio = tm // CONV_HALO
    vec = pl.BlockSpec((1, 1, d), lambda b, i: (b, 0, 0))
    tile = pl.BlockSpec((1, tm, d), lambda b, i: (b, i, 0))
    return pl.pallas_call(
        _conv_b_kernel,
        grid=(bsz, seq // tm),
        in_specs=[
            tile, tile,
            pl.BlockSpec((1, CONV_HALO, d), lambda b, i: (b, jnp.maximum(i * ratio - 1, 0), 0)),
            _const_spec((CONV_WIDTH, d)), _const_spec((1, d)), _const_spec((1, d)), _const_spec((1, d)),
            _const_spec((d, d)), _const_spec((1, d)), _const_spec((1, d)), vec,
        ],
        out_specs=tile,
        out_shape=jax.ShapeDtypeStruct((bsz, seq, d), _F32),
        scratch_shapes=[pltpu.VMEM((d // LANES, tm + CONV_HALO, LANES), _F32), pltpu.VMEM((tm, d), _F32)],
        compiler_params=_params("parallel", "parallel"),
        name="conv_b",
    )(x, u, u, dw, dwb, lng, lnb, w2, b2, g, gate)


def _ffn_kernel(*refs, has_oproj, has_q):
    refs = list(refs)
    x_ref = refs.pop(0)
    if has_oproj:
        a_ref, wo_ref, g1_ref, gatem_ref = refs[:4]
        refs = refs[4:]
    g_ref, sc_ref, sh_ref, win_ref, dw_ref, dwb_ref, wout_ref, g3_ref, gate_ref = refs[:9]
    refs = refs[9:]
    if has_q:
        gq_ref, scq_ref, shq_ref, wqt_ref = refs[:4]
        refs = refs[4:]
    o_ref = refs.pop(0)
    if has_q:
        q_ref = refs.pop(0)
    hbuf, ubuf, carry, pbuf = refs[:4]
    tm = x_ref.shape[1]
    f = wout_ref.shape[0]
    i = pl.program_id(1)
    slabs = FFN_CHUNK // LANES

    @pl.when(i == 0)
    def _():
        carry[...] = jnp.zeros_like(carry)

    if has_oproj:
        xmid = refs[4]
        xmid[...] = x_ref[0] + gatem_ref[0] * (_rms(_dot(a_ref[0], wo_ref[...])) * g1_ref[...])
        x_in = xmid
    else:
        x_in = x_ref.at[0]

    h = _rms(x_in[...]) * g_ref[...] * (1.0 + sc_ref[0]) + sh_ref[0]
    hbuf[...] = h.astype(_BF16)

    def up_project(col0, slab0):
        u = _dot(hbuf[...], win_ref[:, col0:col0 + FFN_CHUNK])
        for k in range(slabs):
            cols = slice(col0 + k * LANES, col0 + (k + 1) * LANES)
            uk = u[:, k * LANES:(k + 1) * LANES]
            ubuf[slab0 + k, 0:FFN_HALO, :] = carry[:, cols]
            ubuf[slab0 + k, FFN_HALO:, :] = uk
            carry[:, cols] = uk[tm - FFN_HALO:, :]

    def conv3(col0, slab):
        cols = slice(col0, col0 + LANES)
        out = dwb_ref[:, cols]
        for j in range(FFN_CONV_WIDTH):
            shift = FFN_CONV_WIDTH - 1 - j
            out = out + ubuf[slab, pl.ds(FFN_HALO - shift, tm), :] * dw_ref[j:j + 1, cols]
        return out

    for idx, c0 in enumerate(range(0, f, FFN_CHUNK)):
        base = (idx % 2) * 2 * slabs
        up_project(c0, base)
        up_project(f + c0, base + slabs)
        for k in range(slabs):
            gk = conv3(c0 + k * LANES, base + k)
            vk = conv3(f + c0 + k * LANES, base + slabs + k)
            pbuf[:, c0 + k * LANES:c0 + (k + 1) * LANES] = (_swish(gk) * vk).astype(_BF16)

    y = _dot(pbuf[...], wout_ref[...])
    x_out = x_in[...] + gate_ref[0] * (_rms(y) * g3_ref[...])
    o_ref[0] = x_out
    if has_q:
        _project_q(x_out, gq_ref, scq_ref, shq_ref, wqt_ref, q_ref)


def _ffn(x, g, sc, sh, w_in, dw, dwb, w_out, g3, gate, oproj=None, qproj=None):
    bsz, seq, d = x.shape
    f = w_out.shape[0]
    tm = ROW_TILE
    vec = pl.BlockSpec((1, 1, d), lambda b, i: (b, 0, 0))
    tile = pl.BlockSpec((1, tm, d), lambda b, i: (b, i, 0))
    single = dict(pipeline_mode=pl.Buffered(1))
    assert seq % tm == 0 and f % FFN_CHUNK == 0 and FFN_CONV_WIDTH - 1 <= FFN_HALO
    operands, in_specs = [x], [tile]
    scratch = [
        pltpu.VMEM((tm, d), _BF16),
        pltpu.VMEM((4 * (FFN_CHUNK // LANES), tm + FFN_HALO, LANES), _F32),
        pltpu.VMEM((FFN_HALO, 2 * f), _F32),
        pltpu.VMEM((tm, f), _BF16),
    ]
    if oproj is not None:
        operands += list(oproj)
        in_specs += [tile, pl.BlockSpec((d, d), lambda b, i: (0, 0), **single), _const_spec((1, d)), vec]
        scratch.append(pltpu.VMEM((tm, d), _F32))
    operands += [g, sc, sh, w_in, dw, dwb, w_out, g3, gate]
    in_specs += [
        _const_spec((1, d)), vec, vec,
        pl.BlockSpec((d, 2 * f), lambda b, i: (0, 0), **single),
        _const_spec((FFN_CONV_WIDTH, 2 * f)), _const_spec((1, 2 * f)),
        pl.BlockSpec((f, d), lambda b, i: (0, 0), **single),
        _const_spec((1, d)), vec,
    ]
    out_specs, out_shape = [tile], [jax.ShapeDtypeStruct((bsz, seq, d), _F32)]
    if qproj is not None:
        tq = ATTN_TILE
        operands += list(qproj)
        in_specs += [_const_spec((1, d)), vec, vec, pl.BlockSpec((d, d), lambda b, i: (0, 0), **single)]
        out_specs.append(pl.BlockSpec((1, N_HEADS, tm // tq, V_HEAD_DIM, 2 * tq), lambda b, i: (b, 0, i, 0, 0)))
        out_shape.append(jax.ShapeDtypeStruct((bsz, N_HEADS, seq // tq, V_HEAD_DIM, 2 * tq), _BF16))
    outs = pl.pallas_call(
        functools.partial(_ffn_kernel, has_oproj=oproj is not None, has_q=qproj is not None),
        grid=(bsz, seq // tm),
        in_specs=in_specs,
        out_specs=out_specs,
        out_shape=out_shape,
        scratch_shapes=scratch,
        compiler_params=_params("arbitrary", "arbitrary"),
        name="ffn",
    )(*operands)
    return outs if qproj is not None else outs[0]


def _kv_kernel(x_ref, g_ref, wk_ref, wvt_ref, k_ref, vt_ref):
    tm = x_ref.shape[1]
    tk = vt_ref.shape[-1]
    hb = (_rms(x_ref[0]) * g_ref[...]).astype(_BF16)
    k = _dot(hb, wk_ref[...])
    vt = lax.dot_general(wvt_ref[...], hb, (((1,), (1,)), ((), ())), preferred_element_type=_F32)
    for hd in range(N_HEADS):
        rows = slice(hd * V_HEAD_DIM, (hd + 1) * V_HEAD_DIM)
        k_ref[0, hd] = k[:, rows].astype(_BF16)
        for c in range(tm // tk):
            vt_ref[0, hd, c, :V_HEAD_DIM, :] = vt[rows, c * tk:(c + 1) * tk].astype(_BF16)
            vt_ref[0, hd, c, V_HEAD_DIM:, :] = jnp.ones((ONES_ROWS, tk), _BF16)


def _kv(x, g, wk, wvt):
    bsz, seq, d = x.shape
    tm, tk = PROJ_TILE, ATTN_TILE
    assert seq % tm == 0 and tm % tk == 0 and d == N_HEADS * V_HEAD_DIM
    return pl.pallas_call(
        _kv_kernel,
        grid=(bsz, seq // tm),
        in_specs=[
            pl.BlockSpec((1, tm, d), lambda b, i: (b, i, 0)),
            _const_spec((1, d)), _const_spec((d, d)), _const_spec((d, d)),
        ],
        out_specs=[
            pl.BlockSpec((1, N_HEADS, tm, V_HEAD_DIM), lambda b, i: (b, 0, i, 0)),
            pl.BlockSpec((1, N_HEADS, tm // tk, V_ROWS, tk), lambda b, i: (b, 0, i, 0, 0)),
        ],
        out_shape=[
            jax.ShapeDtypeStruct((bsz, N_HEADS, seq, V_HEAD_DIM), _BF16),
            jax.ShapeDtypeStruct((bsz, N_HEADS, seq // tk, V_ROWS, tk), _BF16),
        ],
        compiler_params=_params("parallel", "parallel"),
        name="kv",
    )(x, g, wk, wvt)


def _project_q(x, g_ref, sc_ref, sh_ref, wqt_ref, q_ref):
    tm = x.shape[0]
    tq = q_ref.shape[-1] // 2
    h = _rms(x) * g_ref[...] * (1.0 + sc_ref[0]) + sh_ref[0]
    qt = lax.dot_general(wqt_ref[...], h.astype(_BF16), (((1,), (1,)), ((), ())),
                         preferred_element_type=_F32) * (HEAD_DIM ** -0.5 * LOG2E)
    first = lax.broadcasted_iota(jnp.int32, (V_HEAD_DIM, tq), 0) < HEAD_DIM
    for hd in range(N_HEADS):
        for c in range(tm // tq):
            qh = qt[hd * V_HEAD_DIM:(hd + 1) * V_HEAD_DIM, c * tq:(c + 1) * tq]
            q_ref[0, hd, c, :, :tq] = jnp.where(first, qh, 0.0).astype(_BF16)
            q_ref[0, hd, c, :, tq:] = jnp.where(first, 0.0, qh).astype(_BF16)


def _bias_kernel(rb_ref, o_ref):
    hd = pl.program_id(0)
    t = o_ref.shape[-1]
    key = lax.broadcasted_iota(jnp.int32, (t, t), 0)
    qry = lax.broadcasted_iota(jnp.int32, (t, t), 1)
    segs = _bucket_segments()
    far = rb_ref[N_BUCKETS - 1, hd]
    for which in range(2):
        rel = qry - key + which * t
        val = jnp.full((t, t), 0.0, _F32)
        for k in range(len(segs) - 2, -1, -1):
            val = jnp.where(rel < segs[k + 1][0], (rb_ref[segs[k][1], hd] - far) * LOG2E, val)
        o_ref[0, which] = jnp.where(rel >= 0, val, MASK_VALUE)


def _bias_tiles(rel_bias):
    t = ATTN_TILE
    return pl.pallas_call(
        _bias_kernel,
        grid=(N_HEADS,),
        in_specs=[pl.BlockSpec(memory_space=pltpu.SMEM)],
        out_specs=pl.BlockSpec((1, 2, t, t), lambda h: (h, 0, 0, 0)),
        out_shape=jax.ShapeDtypeStruct((N_HEADS, 2, t, t), _F32),
        compiler_params=_params("arbitrary"),
        name="bias",
    )(rel_bias)


def _attn_kernel(q_ref, qn_ref, k_ref, vt_ref, bias_ref, lam_ref, sg_ref, o_ref, m_sc, acc_sc, s_a, s_b, x_a, x_b,
                 *, lambda_init):
    t = ATTN_TILE
    dv = V_HEAD_DIM
    heads = q_ref.shape[1]
    i = pl.program_id(2)
    m_sc[...] = jnp.full_like(m_sc, MASK_VALUE)
    acc_sc[...] = jnp.zeros_like(acc_sc)

    def scores_into(buf, j, hh, q_src=q_ref):
        s_buf, x_buf = buf
        rows = pl.ds(pl.multiple_of(j * t, t), t)
        s = _dot(k_ref[0, hh, rows, :], q_src[0, hh, 0])
        s_buf[hh] = s
        x_buf[hh] = jnp.max(s, axis=0, keepdims=True)

    def step(j, which, cur, nxt):
        if nxt is not None:
            scores_into(nxt, j + 1, 0)
        for hh in range(heads):
            if nxt is not None and hh + 1 < heads:
                scores_into(nxt, j + 1, hh + 1)
            s = cur[0][hh]
            if which is None:
                m_cur = cur[1][hh]
            else:
                bias = bias_ref[hh, which]
                s = jnp.concatenate([s[:, :t] + bias, s[:, t:] + bias], axis=1)
                m_cur = jnp.max(s, axis=0, keepdims=True)
            m_prev = m_sc[hh]
            m_new = jnp.maximum(m_prev, m_cur)
            alpha = jnp.exp2(m_prev - m_new)
            p = jnp.exp2(s - m_new)
            acc_sc[hh] = alpha * acc_sc[hh] + _dot(vt_ref[0, hh, j], p.astype(_BF16))
            m_sc[hh] = m_new

    buf_a, buf_b = (s_a, x_a), (s_b, x_b)

    def far_pair(jj, carry):
        ja = 2 * jj
        for hh in range(heads):
            m_prev = m_sc[hh]
            m_new = jnp.maximum(m_prev, jnp.maximum(x_a[hh], x_b[hh]))
            alpha = jnp.exp2(m_prev - m_new)
            pa = jnp.exp2(s_a[hh] - m_new).astype(_BF16)
            scores_into(buf_a, ja + 2, hh)
            pb = jnp.exp2(s_b[hh] - m_new).astype(_BF16)
            scores_into(buf_b, ja + 3, hh)
            vt2 = jnp.concatenate([vt_ref[0, hh, ja], vt_ref[0, hh, ja + 1]], axis=1)
            pv = _dot(vt2, jnp.concatenate([pa, pb], axis=0))
            acc_sc[hh] = alpha * acc_sc[hh] + pv
            m_sc[hh] = m_new
        return carry

    @pl.when(i == 0)
    def _():
        for hh in range(heads):
            scores_into(buf_a, 0, hh)

    lax.fori_loop(0, jnp.maximum((i - 1) // 2, 0), far_pair, 0)

    @pl.when(i % 2 == 0)
    def _():
        @pl.when(i >= 2)
        def _():
            step(i - 2, None, buf_a, None)
            step(i - 1, 1, buf_b, buf_a)

        step(i, 0, buf_a, None)

    @pl.when(i % 2 == 1)
    def _():
        step(i - 1, 1, buf_a, None)
        step(i, 0, buf_b, None)

    for hh in range(heads):
        scores_into(buf_a, 0, hh, qn_ref)
        scores_into(buf_b, 1, hh, qn_ref)

    lv = lam_ref[...]
    lam = (jnp.exp(jnp.sum(lv[0:1] * lv[1:2], axis=-1, keepdims=True))
           - jnp.exp(jnp.sum(lv[2:3] * lv[3:4], axis=-1, keepdims=True)) + lambda_init)
    for hh in range(heads):
        acc = acc_sc[hh]
        o = acc[:dv] / acc[dv:dv + 1]
        o = o[:, :t] - lam * o[:, t:]
        o = o * lax.rsqrt(jnp.mean(o * o, axis=0, keepdims=True) + EPS) * sg_ref[...] * (1.0 - lambda_init)
        o_ref[0, :, hh * dv:(hh + 1) * dv] = o.T.astype(o_ref.dtype)


def _attention(qt, k, vt, bias, lam, sg, lambda_init):
    bsz, n_heads, seq, dv = k.shape
    t = ATTN_TILE
    nt = seq // t
    hp = ATTN_HEADS_PER_STEP
    assert seq % t == 0 and n_heads % hp == 0 and dv == V_HEAD_DIM == 2 * HEAD_DIM and t >= MAX_DISTANCE
    single = dict(pipeline_mode=pl.Buffered(1))
    return pl.pallas_call(
        functools.partial(_attn_kernel, lambda_init=lambda_init),
        grid=(bsz, n_heads // hp, nt),
        in_specs=[
            pl.BlockSpec((1, hp, 1, dv, 2 * t), lambda b, h, i: (b, h, i, 0, 0)),
            pl.BlockSpec((1, hp, 1, dv, 2 * t), lambda b, h, i: (b, h, jnp.minimum(i + 1, nt - 1), 0, 0)),
            pl.BlockSpec((1, hp, seq, dv), lambda b, h, i: (b, h, 0, 0), **single),
            pl.BlockSpec((1, hp, nt, V_ROWS, t), lambda b, h, i: (b, h, 0, 0, 0), **single),
            pl.BlockSpec((hp, 2, t, t), lambda b, h, i: (h, 0, 0, 0), **single),
            pl.BlockSpec((4, HEAD_DIM), lambda b, h, i: (0, 0)),
            pl.BlockSpec((dv, 1), lambda b, h, i: (0, 0)),
        ],
        out_specs=pl.BlockSpec((1, t, hp * dv), lambda b, h, i: (b, i, h)),
        out_shape=jax.ShapeDtypeStruct((bsz, seq, n_heads * dv), _BF16),
        scratch_shapes=[pltpu.VMEM((hp, 1, 2 * t), _F32), pltpu.VMEM((hp, V_ROWS, 2 * t), _F32),
                        pltpu.VMEM((hp, t, 2 * t), _F32), pltpu.VMEM((hp, t, 2 * t), _F32),
                        pltpu.VMEM((hp, 1, 2 * t), _F32), pltpu.VMEM((hp, 1, 2 * t), _F32)],
        compiler_params=_params("parallel", "parallel", "arbitrary"),
        name="attn",
    )(qt, qt, k, vt, bias, lam, sg)


def kernel(x, c, mod_w, mod_b, norm_g, cm_w1, cm_b1, cm_dw, cm_dwb, cm_ln_g, cm_ln_b, cm_w2, cm_b2, kv_norm_g,
           w_k, w_v, w_q, lam, subln_g, w_o, rel_bias, ffn_w_in, ffn_dw, ffn_dwb, ffn_w_out):
    depth = mod_w.shape[0]
    n_conv = cm_w1.shape[0]
    bsz, seq, d = x.shape
    assert 1 <= n_conv < depth, "the first attention layer's Q is produced by the preceding layer's FFN"

    mod = _modulation(c, mod_w, mod_b)

    def row(v):
        return v.reshape(1, -1)

    def mod_vectors(l):
        return [m.reshape(bsz, 1, d) for m in jnp.split(mod[l], 6, axis=-1)]

    kk = vt = bias = qt = None
    for l in range(depth):
        sh_m, sc_m, g_m, sh_f, sc_f, g_f = mod_vectors(l)
        g0, g1, g2, g3 = [row(norm_g[l, k]) for k in range(4)]
        oproj = None
        if l < n_conv:
            u = _conv_a(x, g0, sc_m, sh_m, cm_w1[l].astype(_BF16), row(cm_b1[l]))
            x = _conv_b(x, u, cm_dw[l], row(cm_dwb[l]), row(cm_ln_g[l]), row(cm_ln_b[l]),
                        cm_w2[l].astype(_BF16), row(cm_b2[l]), g1, g_m)
        else:
            if l == n_conv:
                kk, vt = _kv(x, row(kv_norm_g), w_k.astype(_BF16), w_v.T.astype(_BF16))
                bias = _bias_tiles(rel_bias)
            j = l - n_conv
            lambda_init = 0.8 - 0.6 * math.exp(-0.3 * l)
            a = _attention(qt, kk, vt, bias, lam[j], subln_g[j].reshape(-1, 1), lambda_init)
            oproj = (a, w_o[j].astype(_BF16), g1, g_m)
        qproj = None
        if n_conv <= l + 1 < depth:
            nsh_m, nsc_m, _, _, _, _ = mod_vectors(l + 1)
            qproj = (row(norm_g[l + 1, 0]), nsc_m, nsh_m, w_q[l + 1 - n_conv].T.astype(_BF16))
        res = _ffn(x, g2, sc_f, sh_f, ffn_w_in[l].astype(_BF16), ffn_dw[l], row(ffn_dwb[l]),
                   ffn_w_out[l].astype(_BF16), g3, g_f, oproj=oproj, qproj=qproj)
        x, qt = res if qproj is not None else (res, None)
    return x
```

```python
import functools
import math

import jax
import jax.numpy as jnp
from jax import lax
from jax.experimental import pallas as pl
from jax.experimental.pallas import tpu as pltpu

N_HEADS = 8
HEAD_DIM = 64
V_HEAD_DIM = 128
CONV_WIDTH = 31
FFN_CONV_WIDTH = 3
N_BUCKETS = 32
MAX_DISTANCE = 128
MAX_EXACT = 16
EPS = 1e-6

ROW_TILE = 512
PROJ_TILE = 1024
ATTN_TILE = 256
ATTN_HEADS_PER_STEP = 8
ONES_ROWS = 16
V_ROWS = V_HEAD_DIM + ONES_ROWS
LOG2E = math.log2(math.e)
CONV_HALO = 32
CONV_ROWS = 256
LANES = 128
FFN_CHUNK = 256
FFN_HALO = 8
MASK_VALUE = -1e30
VMEM_LIMIT = 56 * 1024 * 1024

_BF16 = jnp.bfloat16
_F32 = jnp.float32


def _bucket_of_distance(n):
    if n < MAX_EXACT:
        return n
    large = MAX_EXACT + int(math.log(n / MAX_EXACT) / math.log(MAX_DISTANCE / MAX_EXACT) * (N_BUCKETS - MAX_EXACT))
    return min(large, N_BUCKETS - 1)


def _bucket_segments():
    segs = []
    for n in range(MAX_DISTANCE):
        b = _bucket_of_distance(n)
        if not segs or segs[-1][1] != b:
            segs.append((n, b))
    assert all(_bucket_of_distance(n) == N_BUCKETS - 1 for n in range(segs[-1][0], 4 * MAX_DISTANCE))
    return segs


def _rms(x):
    return x * lax.rsqrt(jnp.mean(x * x, axis=-1, keepdims=True) + EPS)


def _sigmoid(x):
    return 0.5 * jnp.tanh(0.5 * x) + 0.5


def _swish(x):
    h = 0.5 * x
    return h + h * jnp.tanh(h)


def _dot(a, b):
    return jnp.dot(a, b, preferred_element_type=_F32)


def _params(*sem):
    return pltpu.CompilerParams(dimension_semantics=sem, vmem_limit_bytes=VMEM_LIMIT)


def _const_spec(shape):
    nd = len(shape)
    return pl.BlockSpec(shape, lambda *_: (0,) * nd)


def _mod_kernel(c_ref, w_ref, b_ref, o_ref):
    o_ref[0] = _dot(_swish(c_ref[...]), w_ref[0]) + b_ref[0]


def _modulation(c, mod_w, mod_b):
    depth, d, n = mod_w.shape
    bsz = c.shape[0]
    tn = 1024
    return pl.pallas_call(
        _mod_kernel,
        grid=(depth, n // tn),
        in_specs=[
            pl.BlockSpec((bsz, d), lambda l, j: (0, 0)),
            pl.BlockSpec((1, d, tn), lambda l, j: (l, 0, j)),
            pl.BlockSpec((1, 1, tn), lambda l, j: (l, 0, j)),
        ],
        out_specs=pl.BlockSpec((1, bsz, tn), lambda l, j: (l, 0, j)),
        out_shape=jax.ShapeDtypeStruct((depth, bsz, n), _F32),
        compiler_params=_params("arbitrary", "arbitrary"),
        name="mod",
    )(c, mod_w, mod_b.reshape(depth, 1, n))


def _conv_a_kernel(x_ref, g_ref, sc_ref, sh_ref, w1_ref, b1_ref, u_ref):
    d = x_ref.shape[-1]
    h = _rms(x_ref[0]) * (g_ref[...] * (1.0 + sc_ref[0])) + sh_ref[0]
    hb = h.astype(_BF16)
    a = _dot(hb, w1_ref[:, :d]) + b1_ref[:, :d]
    gt = _dot(hb, w1_ref[:, d:]) + b1_ref[:, d:]
    u_ref[0] = a * _sigmoid(gt)


def _conv_a(x, g, sc, sh, w1, b1):
    bsz, seq, d = x.shape
    tm = PROJ_TILE
    vec = pl.BlockSpec((1, 1, d), lambda b, i: (b, 0, 0))
    return pl.pallas_call(
        _conv_a_kernel,
        grid=(bsz, seq // tm),
        in_specs=[
            pl.BlockSpec((1, tm, d), lambda b, i: (b, i, 0)),
            _const_spec((1, d)), vec, vec,
            _const_spec((d, 2 * d)), _const_spec((1, 2 * d)),
        ],
        out_specs=pl.BlockSpec((1, tm, d), lambda b, i: (b, i, 0)),
        out_shape=jax.ShapeDtypeStruct((bsz, seq, d), _F32),
        compiler_params=_params("parallel", "parallel"),
        name="conv_a",
    )(x, g, sc, sh, w1, b1)


def _conv_b_kernel(x_ref, ucur_ref, uprev_ref, dw_ref, dwb_ref, lng_ref, lnb_ref, w2_ref, b2_ref,
                   g_ref, gate_ref, o_ref, ubuf, cbuf):
    tm, d = ucur_ref.shape[1], ucur_ref.shape[2]
    i = pl.program_id(1)
    for c in range(d // LANES):
        lanes = slice(c * LANES, (c + 1) * LANES)
        prev = uprev_ref[0, :, lanes]
        ubuf[c, 0:CONV_HALO, :] = jnp.where(i > 0, prev, jnp.zeros_like(prev))
        ubuf[c, CONV_HALO:, :] = ucur_ref[0, :, lanes]

    first_shift = CONV_HALO - (CONV_WIDTH - 1)

    def row_block(r, carry):
        r0 = pl.multiple_of(r * CONV_ROWS, CONV_ROWS)
        for c in range(d // LANES):
            lanes = slice(c * LANES, (c + 1) * LANES)
            acc = jnp.broadcast_to(dwb_ref[:, lanes], (CONV_ROWS, LANES))
            for j in range(CONV_WIDTH):
                acc = acc + ubuf[c, pl.ds(r0 + first_shift + j, CONV_ROWS), :] * dw_ref[j:j + 1, lanes]
            cbuf[pl.ds(r0, CONV_ROWS), lanes] = acc
        return carry

    lax.fori_loop(0, tm // CONV_ROWS, row_block, 0)

    cv = cbuf[...]
    mu = jnp.mean(cv, axis=-1, keepdims=True)
    cc = cv - mu
    var = jnp.mean(cc * cc, axis=-1, keepdims=True)
    z = cc * lax.rsqrt(var + EPS) * lng_ref[...] + lnb_ref[...]
    y = _dot(_swish(z).astype(_BF16), w2_ref[...]) + b2_ref[...]
    o_ref[0] = x_ref[0] + _rms(y) * (gate_ref[0] * g_ref[...])


def _conv_b(x, u, dw, dwb, lng, lnb, w2, b2, g, gate):
    bsz, seq, d = x.shape
    tm = PROJ_TILE
    assert seq % tm == 0 and tm % CONV_ROWS == 0 and d % LANES == 0 and CONV_WIDTH - 1 <= CONV_HALO
    ratio = tm // CONV_HALO
    vec = pl.BlockSpec((1, 1, d), lambda b, i: (b, 0, 0))
    tile = pl.BlockSpec((1, tm, d), lambda b, i: (b, i, 0))
    return pl.pallas_call(
        _conv_b_kernel,
        grid=(bsz, seq // tm),
        in_specs=[
            tile, tile,
            pl.BlockSpec((1, CONV_HALO, d), lambda b, i: (b, jnp.maximum(i * ratio - 1, 0), 0)),
            _const_spec((CONV_WIDTH, d)), _const_spec((1, d)), _const_spec((1, d)), _const_spec((1, d)),
            _const_spec((d, d)), _const_spec((1, d)), _const_spec((1, d)), vec,
        ],
        out_specs=tile,
        out_shape=jax.ShapeDtypeStruct((bsz, seq, d), _F32),
        scratch_shapes=[pltpu.VMEM((d // LANES, tm + CONV_HALO, LANES), _F32), pltpu.VMEM((tm, d), _F32)],
        compiler_params=_params("parallel", "parallel"),
        name="conv_b",
    )(x, u, u, dw, dwb, lng, lnb, w2, b2, g, gate)


def _ffn_kernel(*refs, has_oproj, has_q):
    refs = list(refs)
    x_ref = refs.pop(0)
    if has_oproj:
        a_ref, wo_ref, g1_ref, gatem_ref = refs[:4]
        refs = refs[4:]
    g_ref, sc_ref, sh_ref, win_ref, dw_ref, dwb_ref, wout_ref, g3_ref, gate_ref = refs[:9]
    refs = refs[9:]
    if has_q:
        gq_ref, scq_ref, shq_ref, wqt_ref = refs[:4]
        refs = refs[4:]
    o_ref = refs.pop(0)
    if has_q:
        q_ref = refs.pop(0)
    hbuf, ubuf, carry, pbuf = refs[:4]
    tm = x_ref.shape[1]
    f = wout_ref.shape[0]
    i = pl.program_id(1)
    slabs = FFN_CHUNK // LANES

    @pl.when(i == 0)
    def _():
        carry[...] = jnp.zeros_like(carry)

    if has_oproj:
        xmid = refs[4]
        xmid[...] = x_ref[0] + _rms(_dot(a_ref[0], wo_ref[...])) * (gatem_ref[0] * g1_ref[...])
        x_in = xmid
    else:
        x_in = x_ref.at[0]

    h = _rms(x_in[...]) * (g_ref[...] * (1.0 + sc_ref[0])) + sh_ref[0]
    hbuf[...] = h.astype(_BF16)

    def up_project(col0, slab0):
        u = _dot(hbuf[...], win_ref[:, col0:col0 + FFN_CHUNK])
        for k in range(slabs):
            cols = slice(col0 + k * LANES, col0 + (k + 1) * LANES)
            uk = u[:, k * LANES:(k + 1) * LANES]
            ubuf[slab0 + k, 0:FFN_HALO, :] = carry[:, cols]
            ubuf[slab0 + k, FFN_HALO:, :] = uk
            carry[:, cols] = uk[tm - FFN_HALO:, :]

    def conv3(col0, slab):
        cols = slice(col0, col0 + LANES)
        out = dwb_ref[:, cols]
        for j in range(FFN_CONV_WIDTH):
            shift = FFN_CONV_WIDTH - 1 - j
            out = out + ubuf[slab, pl.ds(FFN_HALO - shift, tm), :] * dw_ref[j:j + 1, cols]
        return out

    for idx, c0 in enumerate(range(0, f, FFN_CHUNK)):
        base = (idx % 2) * 2 * slabs
        up_project(c0, base)
        up_project(f + c0, base + slabs)
        for k in range(slabs):
            gk = conv3(c0 + k * LANES, base + k)
            vk = conv3(f + c0 + k * LANES, base + slabs + k)
            pbuf[:, c0 + k * LANES:c0 + (k + 1) * LANES] = (_swish(gk) * vk).astype(_BF16)

    y = _dot(pbuf[...], wout_ref[...])
    x_out = x_in[...] + _rms(y) * (gate_ref[0] * g3_ref[...])
    o_ref[0] = x_out
    if has_q:
        _project_q(x_out, gq_ref, scq_ref, shq_ref, wqt_ref, q_ref)


def _ffn(x, g, sc, sh, w_in, dw, dwb, w_out, g3, gate, oproj=None, qproj=None):
    bsz, seq, d = x.shape
    f = w_out.shape[0]
    tm = ROW_TILE
    vec = pl.BlockSpec((1, 1, d), lambda b, i: (b, 0, 0))
    tile = pl.BlockSpec((1, tm, d), lambda b, i: (b, i, 0))
    single = dict(pipeline_mode=pl.Buffered(1))
    assert seq % tm == 0 and f % FFN_CHUNK == 0 and FFN_CONV_WIDTH - 1 <= FFN_HALO
    operands, in_specs = [x], [tile]
    scratch = [
        pltpu.VMEM((tm, d), _BF16),
        pltpu.VMEM((4 * (FFN_CHUNK // LANES), tm + FFN_HALO, LANES), _F32),
        pltpu.VMEM((FFN_HALO, 2 * f), _F32),
        pltpu.VMEM((tm, f), _BF16),
    ]
    if oproj is not None:
        operands += list(oproj)
        in_specs += [tile, pl.BlockSpec((d, d), lambda b, i: (0, 0), **single), _const_spec((1, d)), vec]
        scratch.append(pltpu.VMEM((tm, d), _F32))
    operands += [g, sc, sh, w_in, dw, dwb, w_out, g3, gate]
    in_specs += [
        _const_spec((1, d)), vec, vec,
        pl.BlockSpec((d, 2 * f), lambda b, i: (0, 0), **single),
        _const_spec((FFN_CONV_WIDTH, 2 * f)), _const_spec((1, 2 * f)),
        pl.BlockSpec((f, d), lambda b, i: (0, 0), **single),
        _const_spec((1, d)), vec,
    ]
    out_specs, out_shape = [tile], [jax.ShapeDtypeStruct((bsz, seq, d), _F32)]
    if qproj is not None:
        tq = ATTN_TILE
        operands += list(qproj)
        in_specs += [_const_spec((1, d)), vec, vec, pl.BlockSpec((d, d), lambda b, i: (0, 0), **single)]
        out_specs.append(pl.BlockSpec((1, N_HEADS, tm // tq, V_HEAD_DIM, 2 * tq), lambda b, i: (b, 0, i, 0, 0)))
        out_shape.append(jax.ShapeDtypeStruct((bsz, N_HEADS, seq // tq, V_HEAD_DIM, 2 * tq), _BF16))
    outs = pl.pallas_call(
        functools.partial(_ffn_kernel, has_oproj=oproj is not None, has_q=qproj is not None),
        grid=(bsz, seq // tm),
        in_specs=in_specs,
        out_specs=out_specs,
        out_shape=out_shape,
        scratch_shapes=scratch,
        compiler_params=_params("arbitrary", "arbitrary"),
        name="ffn",
    )(*operands)
    return outs if qproj is not None else outs[0]


def _kv_kernel(x_ref, g_ref, wk_ref, wvt_ref, k_ref, vt_ref):
    tm = x_ref.shape[1]
    tk = vt_ref.shape[-1]
    hb = (_rms(x_ref[0]) * g_ref[...]).astype(_BF16)
    k = _dot(hb, wk_ref[...])
    vt = lax.dot_general(wvt_ref[...], hb, (((1,), (1,)), ((), ())), preferred_element_type=_F32)
    for hd in range(N_HEADS):
        rows = slice(hd * V_HEAD_DIM, (hd + 1) * V_HEAD_DIM)
        k_ref[0, hd] = k[:, rows].astype(_BF16)
        for c in range(tm // tk):
            vt_ref[0, hd, c, :V_HEAD_DIM, :] = vt[rows, c * tk:(c + 1) * tk].astype(_BF16)
            vt_ref[0, hd, c, V_HEAD_DIM:, :] = jnp.ones((ONES_ROWS, tk), _BF16)


def _kv(x, g, wk, wvt):
    bsz, seq, d = x.shape
    tm, tk = PROJ_TILE, ATTN_TILE
    assert seq % tm == 0 and tm % tk == 0 and d == N_HEADS * V_HEAD_DIM
    return pl.pallas_call(
        _kv_kernel,
        grid=(bsz, seq // tm),
        in_specs=[
            pl.BlockSpec((1, tm, d), lambda b, i: (b, i, 0)),
            _const_spec((1, d)), _const_spec((d, d)), _const_spec((d, d)),
        ],
        out_specs=[
            pl.BlockSpec((1, N_HEADS, tm, V_HEAD_DIM), lambda b, i: (b, 0, i, 0)),
            pl.BlockSpec((1, N_HEADS, tm // tk, V_ROWS, tk), lambda b, i: (b, 0, i, 0, 0)),
        ],
        out_shape=[
            jax.ShapeDtypeStruct((bsz, N_HEADS, seq, V_HEAD_DIM), _BF16),
            jax.ShapeDtypeStruct((bsz, N_HEADS, seq // tk, V_ROWS, tk), _BF16),
        ],
        compiler_params=_params("parallel", "parallel"),
        name="kv",
    )(x, g, wk, wvt)


def _project_q(x, g_ref, sc_ref, sh_ref, wqt_ref, q_ref):
    tm = x.shape[0]
    tq = q_ref.shape[-1] // 2
    h = _rms(x) * (g_ref[...] * (1.0 + sc_ref[0])) + sh_ref[0]
    qt = lax.dot_general(wqt_ref[...], h.astype(_BF16), (((1,), (1,)), ((), ())),
                         preferred_element_type=_F32) * (HEAD_DIM ** -0.5 * LOG2E)
    first = lax.broadcasted_iota(jnp.int32, (V_HEAD_DIM, tq), 0) < HEAD_DIM
    for hd in range(N_HEADS):
        for c in range(tm // tq):
            qh = qt[hd * V_HEAD_DIM:(hd + 1) * V_HEAD_DIM, c * tq:(c + 1) * tq]
            q_ref[0, hd, c, :, :tq] = jnp.where(first, qh, 0.0).astype(_BF16)
            q_ref[0, hd, c, :, tq:] = jnp.where(first, 0.0, qh).astype(_BF16)


def _bias_kernel(rb_ref, o_ref):
    hd = pl.program_id(0)
    t = o_ref.shape[-1]
    key = lax.broadcasted_iota(jnp.int32, (t, t), 0)
    qry = lax.broadcasted_iota(jnp.int32, (t, t), 1)
    segs = _bucket_segments()
    far = rb_ref[N_BUCKETS - 1, hd]
    for which in range(2):
        rel = qry - key + which * t
        val = jnp.full((t, t), 0.0, _F32)
        for k in range(len(segs) - 2, -1, -1):
            val = jnp.where(rel < segs[k + 1][0], (rb_ref[segs[k][1], hd] - far) * LOG2E, val)
        o_ref[0, which] = jnp.where(rel >= 0, val, MASK_VALUE)


def _bias_tiles(rel_bias):
    t = ATTN_TILE
    return pl.pallas_call(
        _bias_kernel,
        grid=(N_HEADS,),
        in_specs=[pl.BlockSpec(memory_space=pltpu.SMEM)],
        out_specs=pl.BlockSpec((1, 2, t, t), lambda h: (h, 0, 0, 0)),
        out_shape=jax.ShapeDtypeStruct((N_HEADS, 2, t, t), _F32),
        compiler_params=_params("arbitrary"),
        name="bias",
    )(rel_bias)


def _attn_kernel(q_ref, qn_ref, k_ref, vt_ref, bias_ref, lam_ref, sg_ref, o_ref, m_sc, acc_sc, s_a, s_b, x_a, x_b,
                 *, lambda_init):
    t = ATTN_TILE
    dv = V_HEAD_DIM
    heads = q_ref.shape[1]
    i = pl.program_id(2)
    m_sc[...] = jnp.full_like(m_sc, MASK_VALUE)
    acc_sc[...] = jnp.zeros_like(acc_sc)

    def scores_into(buf, j, hh, q_src=q_ref):
        s_buf, x_buf = buf
        rows = pl.ds(pl.multiple_of(j * t, t), t)
        s = _dot(k_ref[0, hh, rows, :], q_src[0, hh, 0])
        s_buf[hh] = s
        x_buf[hh] = jnp.max(s, axis=0, keepdims=True)

    def step(j, which, cur, nxt):
        if nxt is not None:
            scores_into(nxt, j + 1, 0)
        for hh in range(heads):
            if nxt is not None and hh + 1 < heads:
                scores_into(nxt, j + 1, hh + 1)
            s = cur[0][hh]
            if which is None:
                m_cur = cur[1][hh]
            else:
                bias = bias_ref[hh, which]
                s = jnp.concatenate([s[:, :t] + bias, s[:, t:] + bias], axis=1)
                m_cur = jnp.max(s, axis=0, keepdims=True)
            m_prev = m_sc[hh]
            m_new = jnp.maximum(m_prev, m_cur)
            alpha = jnp.exp2(m_prev - m_new)
            p = jnp.exp2(s - m_new)
            acc_sc[hh] = alpha * acc_sc[hh] + _dot(vt_ref[0, hh, j], p.astype(_BF16))
            m_sc[hh] = m_new

    buf_a, buf_b = (s_a, x_a), (s_b, x_b)

    def far_pair(jj, carry):
        ja = 2 * jj
        for hh in range(heads):
            m_prev = m_sc[hh]
            m_new = jnp.maximum(m_prev, jnp.maximum(x_a[hh], x_b[hh]))
            alpha = jnp.exp2(m_prev - m_new)
            pa = jnp.exp2(s_a[hh] - m_new).astype(_BF16)
            scores_into(buf_a, ja + 2, hh)
            pb = jnp.exp2(s_b[hh] - m_new).astype(_BF16)
            scores_into(buf_b, ja + 3, hh)
            vt2 = jnp.concatenate([vt_ref[0, hh, ja], vt_ref[0, hh, ja + 1]], axis=1)
            pv = _dot(vt2, jnp.concatenate([pa, pb], axis=0))
            acc_sc[hh] = alpha * acc_sc[hh] + pv
            m_sc[hh] = m_new
        return carry

    @pl.when(i == 0)
    def _():
        for hh in range(heads):
            scores_into(buf_a, 0, hh)

    lax.fori_loop(0, jnp.maximum((i - 1) // 2, 0), far_pair, 0)

    @pl.when(i % 2 == 0)
    def _():
        @pl.when(i >= 2)
        def _():
            step(i - 2, None, buf_a, None)
            step(i - 1, 1, buf_b, buf_a)

        step(i, 0, buf_a, None)

    @pl.when(i % 2 == 1)
    def _():
        step(i - 1, 1, buf_a, None)
        step(i, 0, buf_b, None)

    for hh in range(heads):
        scores_into(buf_a, 0, hh, qn_ref)
        scores_into(buf_b, 1, hh, qn_ref)

    lv = lam_ref[...]
    lam = (jnp.exp(jnp.sum(lv[0:1] * lv[1:2], axis=-1, keepdims=True))
           - jnp.exp(jnp.sum(lv[2:3] * lv[3:4], axis=-1, keepdims=True)) + lambda_init)
    for hh in range(heads):
        acc = acc_sc[hh]
        o = acc[:dv] / acc[dv:dv + 1]
        o = o[:, :t] - lam * o[:, t:]
        o = o * lax.rsqrt(jnp.mean(o * o, axis=0, keepdims=True) + EPS) * sg_ref[...] * (1.0 - lambda_init)
        o_ref[0, :, hh * dv:(hh + 1) * dv] = o.T.astype(o_ref.dtype)


def _attention(qt, k, vt, bias, lam, sg, lambda_init):
    bsz, n_heads, seq, dv = k.shape
    t = ATTN_TILE
    nt = seq // t
    hp = ATTN_HEADS_PER_STEP
    assert seq % t == 0 and n_heads % hp == 0 and dv == V_HEAD_DIM == 2 * HEAD_DIM and t >= MAX_DISTANCE
    single = dict(pipeline_mode=pl.Buffered(1))
    return pl.pallas_call(
        functools.partial(_attn_kernel, lambda_init=lambda_init),
        grid=(bsz, n_heads // hp, nt),
        in_specs=[
            pl.BlockSpec((1, hp, 1, dv, 2 * t), lambda b, h, i: (b, h, i, 0, 0)),
            pl.BlockSpec((1, hp, 1, dv, 2 * t), lambda b, h, i: (b, h, jnp.minimum(i + 1, nt - 1), 0, 0)),
            pl.BlockSpec((1, hp, seq, dv), lambda b, h, i: (b, h, 0, 0), **single),
            pl.BlockSpec((1, hp, nt, V_ROWS, t), lambda b, h, i: (b, h, 0, 0, 0), **single),
            pl.BlockSpec((hp, 2, t, t), lambda b, h, i: (h, 0, 0, 0), **single),
            pl.BlockSpec((4, HEAD_DIM), lambda b, h, i: (0, 0)),
            pl.BlockSpec((dv, 1), lambda b, h, i: (0, 0)),
        ],
        out_specs=pl.BlockSpec((1, t, hp * dv), lambda b, h, i: (b, i, h)),
        out_shape=jax.ShapeDtypeStruct((bsz, seq, n_heads * dv), _BF16),
        scratch_shapes=[pltpu.VMEM((hp, 1, 2 * t), _F32), pltpu.VMEM((hp, V_ROWS, 2 * t), _F32),
                        pltpu.VMEM((hp, t, 2 * t), _F32), pltpu.VMEM((hp, t, 2 * t), _F32),
                        pltpu.VMEM((hp, 1, 2 * t), _F32), pltpu.VMEM((hp, 1, 2 * t), _F32)],
        compiler_params=_params("parallel", "parallel", "arbitrary"),
        name="attn",
    )(qt, qt, k, vt, bias, lam, sg)


def kernel(x, c, mod_w, mod_b, norm_g, cm_w1, cm_b1, cm_dw, cm_dwb, cm_ln_g, cm_ln_b, cm_w2, cm_b2, kv_norm_g,
           w_k, w_v, w_q, lam, subln_g, w_o, rel_bias, ffn_w_in, ffn_dw, ffn_dwb, ffn_w_out):
    depth = mod_w.shape[0]
    n_conv = cm_w1.shape[0]
    bsz, seq, d = x.shape
    assert 1 <= n_conv < depth, "the first attention layer's Q is produced by the preceding layer's FFN"

    mod = _modulation(c, mod_w, mod_b)

    def row(v):
        return v.reshape(1, -1)

    def mod_vectors(l):
        return [m.reshape(bsz, 1, d) for m in jnp.split(mod[l], 6, axis=-1)]

    kk = vt = bias = qt = None
    for l in range(depth):
        sh_m, sc_m, g_m, sh_f, sc_f, g_f = mod_vectors(l)
        g0, g1, g2, g3 = [row(norm_g[l, k]) for k in range(4)]
        oproj = None
        if l < n_conv:
            u = _conv_a(x, g0, sc_m, sh_m, cm_w1[l].astype(_BF16), row(cm_b1[l]))
            x = _conv_b(x, u, cm_dw[l], row(cm_dwb[l]), row(cm_ln_g[l]), row(cm_ln_b[l]),
                        cm_w2[l].astype(_BF16), row(cm_b2[l]), g1, g_m)
        else:
            if l == n_conv:
                kk, vt = _kv(x, row(kv_norm_g), w_k.astype(_BF16), w_v.T.astype(_BF16))
                bias = _bias_tiles(rel_bias)
            j = l - n_conv
            lambda_init = 0.8 - 0.6 * math.exp(-0.3 * l)
            a = _attention(qt, kk, vt, bias, lam[j], subln_g[j].reshape(-1, 1), lambda_init)
            oproj = (a, w_o[j].astype(_BF16), g1, g_m)
        qproj = None
        if n_conv <= l + 1 < depth:
            nsh_m, nsc_m, _, _, _, _ = mod_vectors(l + 1)
            qproj = (row(norm_g[l + 1, 0]), nsc_m, nsh_m, w_q[l + 1 - n_conv].T.astype(_BF16))
        res = _ffn(x, g2, sc_f, sh_f, ffn_w_in[l].astype(_BF16), ffn_dw[l], row(ffn_dwb[l]),
                   ffn_w_out[l].astype(_BF16), g3, g_f, oproj=oproj, qproj=qproj)
        x, qt = res if qproj is not None else (res, None)
    return x
```

```python
import functools
import math

import jax
import jax.numpy as jnp
from jax import lax
from jax.experimental import pallas as pl
from jax.experimental.pallas import tpu as pltpu

N_HEADS = 8
HEAD_DIM = 64
V_HEAD_DIM = 128
CONV_WIDTH = 31
FFN_CONV_WIDTH = 3
N_BUCKETS = 32
MAX_DISTANCE = 128
MAX_EXACT = 16
EPS = 1e-6

ROW_TILE = 512
PROJ_TILE = 1024
ATTN_TILE = 256
ATTN_HEADS_PER_STEP = 8
ONES_ROWS = 16
V_ROWS = V_HEAD_DIM + ONES_ROWS
LOG2E = math.log2(math.e)
CONV_HALO = 32
CONV_ROWS = 256
LANES = 128
FFN_CHUNK = 256
FFN_HALO = 8
MASK_VALUE = -1e30
VMEM_LIMIT = 56 * 1024 * 1024

_BF16 = jnp.bfloat16
_F32 = jnp.float32


def _bucket_of_distance(n):
    if n < MAX_EXACT:
        return n
    large = MAX_EXACT + int(math.log(n / MAX_EXACT) / math.log(MAX_DISTANCE / MAX_EXACT) * (N_BUCKETS - MAX_EXACT))
    return min(large, N_BUCKETS - 1)


def _bucket_segments():
    segs = []
    for n in range(MAX_DISTANCE):
        b = _bucket_of_distance(n)
        if not segs or segs[-1][1] != b:
            segs.append((n, b))
    assert all(_bucket_of_distance(n) == N_BUCKETS - 1 for n in range(segs[-1][0], 4 * MAX_DISTANCE))
    return segs


def _rms(x):
    return x * lax.rsqrt(jnp.mean(x * x, axis=-1, keepdims=True) + EPS)


def _sigmoid(x):
    return 0.5 * jnp.tanh(0.5 * x) + 0.5


def _swish(x):
    h = 0.5 * x
    return h + h * jnp.tanh(h)


def _dot(a, b):
    return jnp.dot(a, b, preferred_element_type=_F32)


def _params(*sem):
    return pltpu.CompilerParams(dimension_semantics=sem, vmem_limit_bytes=VMEM_LIMIT)


def _const_spec(shape):
    nd = len(shape)
    return pl.BlockSpec(shape, lambda *_: (0,) * nd)


def _mod_kernel(c_ref, w_ref, b_ref, o_ref):
    o_ref[0] = _dot(_swish(c_ref[...]), w_ref[0]) + b_ref[0]


def _modulation(c, mod_w, mod_b):
    depth, d, n = mod_w.shape
    bsz = c.shape[0]
    tn = 1024
    return pl.pallas_call(
        _mod_kernel,
        grid=(depth, n // tn),
        in_specs=[
            pl.BlockSpec((bsz, d), lambda l, j: (0, 0)),
            pl.BlockSpec((1, d, tn), lambda l, j: (l, 0, j)),
            pl.BlockSpec((1, 1, tn), lambda l, j: (l, 0, j)),
        ],
        out_specs=pl.BlockSpec((1, bsz, tn), lambda l, j: (l, 0, j)),
        out_shape=jax.ShapeDtypeStruct((depth, bsz, n), _F32),
        compiler_params=_params("arbitrary", "arbitrary"),
        name="mod",
    )(c, mod_w, mod_b.reshape(depth, 1, n))


def _conv_a_kernel(x_ref, g_ref, sc_ref, sh_ref, w1_ref, b1_ref, u_ref):
    d = x_ref.shape[-1]
    h_half = _rms(x_ref[0]) * (0.5 * g_ref[...] * (1.0 + sc_ref[0])) + 0.5 * sh_ref[0]
    hb = h_half.astype(_BF16)
    a_half = _dot(hb, w1_ref[:, :d]) + 0.5 * b1_ref[:, :d]
    g_half = _dot(hb, w1_ref[:, d:]) + 0.5 * b1_ref[:, d:]
    u_ref[0] = a_half + a_half * jnp.tanh(g_half)


def _conv_a(x, g, sc, sh, w1, b1):
    bsz, seq, d = x.shape
    tm = PROJ_TILE
    vec = pl.BlockSpec((1, 1, d), lambda b, i: (b, 0, 0))
    return pl.pallas_call(
        _conv_a_kernel,
        grid=(bsz, seq // tm),
        in_specs=[
            pl.BlockSpec((1, tm, d), lambda b, i: (b, i, 0)),
            _const_spec((1, d)), vec, vec,
            _const_spec((d, 2 * d)), _const_spec((1, 2 * d)),
        ],
        out_specs=pl.BlockSpec((1, tm, d), lambda b, i: (b, i, 0)),
        out_shape=jax.ShapeDtypeStruct((bsz, seq, d), _F32),
        compiler_params=_params("parallel", "parallel"),
        name="conv_a",
    )(x, g, sc, sh, w1, b1)


def _conv_b_kernel(x_ref, ucur_ref, uprev_ref, dw_ref, dwb_ref, lng_ref, lnb_ref, w2_ref, b2_ref,
                   g_ref, gate_ref, o_ref, ubuf, cbuf):
    tm, d = ucur_ref.shape[1], ucur_ref.shape[2]
    i = pl.program_id(1)
    for c in range(d // LANES):
        lanes = slice(c * LANES, (c + 1) * LANES)
        prev = uprev_ref[0, :, lanes]
        ubuf[c, 0:CONV_HALO, :] = jnp.where(i > 0, prev, jnp.zeros_like(prev))
        ubuf[c, CONV_HALO:, :] = ucur_ref[0, :, lanes]

    first_shift = CONV_HALO - (CONV_WIDTH - 1)

    def row_block(r, carry):
        r0 = pl.multiple_of(r * CONV_ROWS, CONV_ROWS)
        for c in range(d // LANES):
            lanes = slice(c * LANES, (c + 1) * LANES)
            acc = jnp.broadcast_to(dwb_ref[:, lanes], (CONV_ROWS, LANES))
            for j in range(CONV_WIDTH):
                acc = acc + ubuf[c, pl.ds(r0 + first_shift + j, CONV_ROWS), :] * dw_ref[j:j + 1, lanes]
            cbuf[pl.ds(r0, CONV_ROWS), lanes] = acc
        return carry

    lax.fori_loop(0, tm // CONV_ROWS, row_block, 0)

    cv = cbuf[...]
    mu = jnp.mean(cv, axis=-1, keepdims=True)
    cc = cv - mu
    var = jnp.mean(cc * cc, axis=-1, keepdims=True)
    hz = cc * lax.rsqrt(var + EPS) * (0.5 * lng_ref[...]) + 0.5 * lnb_ref[...]
    y = _dot((hz + hz * jnp.tanh(hz)).astype(_BF16), w2_ref[...]) + b2_ref[...]
    o_ref[0] = x_ref[0] + _rms(y) * (gate_ref[0] * g_ref[...])


def _conv_b(x, u, dw, dwb, lng, lnb, w2, b2, g, gate):
    bsz, seq, d = x.shape
    tm = PROJ_TILE
    assert seq % tm == 0 and tm % CONV_ROWS == 0 and d % LANES == 0 and CONV_WIDTH - 1 <= CONV_HALO
    ratio = tm // CONV_HALO
    vec = pl.BlockSpec((1, 1, d), lambda b, i: (b, 0, 0))
    tile = pl.BlockSpec((1, tm, d), lambda b, i: (b, i, 0))
    return pl.pallas_call(
        _conv_b_kernel,
        grid=(bsz, seq // tm),
        in_specs=[
            tile, tile,
            pl.BlockSpec((1, CONV_HALO, d), lambda b, i: (b, jnp.maximum(i * ratio - 1, 0), 0)),
            _const_spec((CONV_WIDTH, d)), _const_spec((1, d)), _const_spec((1, d)), _const_spec((1, d)),
            _const_spec((d, d)), _const_spec((1, d)), _const_spec((1, d)), vec,
        ],
        out_specs=tile,
        out_shape=jax.ShapeDtypeStruct((bsz, seq, d), _F32),
        scratch_shapes=[pltpu.VMEM((d // LANES, tm + CONV_HALO, LANES), _F32), pltpu.VMEM((tm, d), _F32)],
        compiler_params=_params("parallel", "parallel"),
        name="conv_b",
    )(x, u, u, dw, dwb, lng, lnb, w2, b2, g, gate)


def _ffn_kernel(*refs, has_oproj, has_q):
    refs = list(refs)
    x_ref = refs.pop(0)
    if has_oproj:
        a_ref, wo_ref, g1_ref, gatem_ref = refs[:4]
        refs = refs[4:]
    g_ref, sc_ref, sh_ref, win_ref, dw_ref, dwb_ref, wout_ref, g3_ref, gate_ref = refs[:9]
    refs = refs[9:]
    if has_q:
        gq_ref, scq_ref, shq_ref, wqt_ref = refs[:4]
        refs = refs[4:]
    o_ref = refs.pop(0)
    if has_q:
        q_ref = refs.pop(0)
    hbuf, ubuf, carry, pbuf = refs[:4]
    tm = x_ref.shape[1]
    f = wout_ref.shape[0]
    i = pl.program_id(1)
    slabs = FFN_CHUNK // LANES

    @pl.when(i == 0)
    def _():
        carry[...] = jnp.zeros_like(carry)

    if has_oproj:
        xmid = refs[4]
        xmid[...] = x_ref[0] + _rms(_dot(a_ref[0], wo_ref[...])) * (gatem_ref[0] * g1_ref[...])
        x_in = xmid
    else:
        x_in = x_ref.at[0]

    h = _rms(x_in[...]) * (g_ref[...] * (1.0 + sc_ref[0])) + sh_ref[0]
    hbuf[...] = h.astype(_BF16)

    def up_project(col0, slab0):
        u = _dot(hbuf[...], win_ref[:, col0:col0 + FFN_CHUNK])
        for k in range(slabs):
            cols = slice(col0 + k * LANES, col0 + (k + 1) * LANES)
            uk = u[:, k * LANES:(k + 1) * LANES]
            ubuf[slab0 + k, 0:FFN_HALO, :] = carry[:, cols]
            ubuf[slab0 + k, FFN_HALO:, :] = uk
            carry[:, cols] = uk[tm - FFN_HALO:, :]

    def conv3(col0, slab):
        cols = slice(col0, col0 + LANES)
        out = dwb_ref[:, cols]
        for j in range(FFN_CONV_WIDTH):
            shift = FFN_CONV_WIDTH - 1 - j
            out = out + ubuf[slab, pl.ds(FFN_HALO - shift, tm), :] * dw_ref[j:j + 1, cols]
        return out

    for idx, c0 in enumerate(range(0, f, FFN_CHUNK)):
        base = (idx % 2) * 2 * slabs
        up_project(c0, base)
        up_project(f + c0, base + slabs)
        for k in range(slabs):
            gk = conv3(c0 + k * LANES, base + k)
            vk = conv3(f + c0 + k * LANES, base + slabs + k)
            pbuf[:, c0 + k * LANES:c0 + (k + 1) * LANES] = (_swish(gk) * vk).astype(_BF16)

    y = _dot(pbuf[...], wout_ref[...])
    x_out = x_in[...] + _rms(y) * (gate_ref[0] * g3_ref[...])
    o_ref[0] = x_out
    if has_q:
        _project_q(x_out, gq_ref, scq_ref, shq_ref, wqt_ref, q_ref)


def _ffn(x, g, sc, sh, w_in, dw, dwb, w_out, g3, gate, oproj=None, qproj=None):
    bsz, seq, d = x.shape
    f = w_out.shape[0]
    tm = ROW_TILE
    vec = pl.BlockSpec((1, 1, d), lambda b, i: (b, 0, 0))
    tile = pl.BlockSpec((1, tm, d), lambda b, i: (b, i, 0))
    single = dict(pipeline_mode=pl.Buffered(1))
    assert seq % tm == 0 and f % FFN_CHUNK == 0 and FFN_CONV_WIDTH - 1 <= FFN_HALO
    operands, in_specs = [x], [tile]
    scratch = [
        pltpu.VMEM((tm, d), _BF16),
        pltpu.VMEM((4 * (FFN_CHUNK // LANES), tm + FFN_HALO, LANES), _F32),
        pltpu.VMEM((FFN_HALO, 2 * f), _F32),
        pltpu.VMEM((tm, f), _BF16),
    ]
    if oproj is not None:
        operands += list(oproj)
        in_specs += [tile, pl.BlockSpec((d, d), lambda b, i: (0, 0), **single), _const_spec((1, d)), vec]
        scratch.append(pltpu.VMEM((tm, d), _F32))
    operands += [g, sc, sh, w_in, dw, dwb, w_out, g3, gate]
    in_specs += [
        _const_spec((1, d)), vec, vec,
        pl.BlockSpec((d, 2 * f), lambda b, i: (0, 0), **single),
        _const_spec((FFN_CONV_WIDTH, 2 * f)), _const_spec((1, 2 * f)),
        pl.BlockSpec((f, d), lambda b, i: (0, 0), **single),
        _const_spec((1, d)), vec,
    ]
    out_specs, out_shape = [tile], [jax.ShapeDtypeStruct((bsz, seq, d), _F32)]
    if qproj is not None:
        tq = ATTN_TILE
        operands += list(qproj)
        in_specs += [_const_spec((1, d)), vec, vec, pl.BlockSpec((d, d), lambda b, i: (0, 0), **single)]
        out_specs.append(pl.BlockSpec((1, N_HEADS, tm // tq, V_HEAD_DIM, 2 * tq), lambda b, i: (b, 0, i, 0, 0)))
        out_shape.append(jax.ShapeDtypeStruct((bsz, N_HEADS, seq // tq, V_HEAD_DIM, 2 * tq), _BF16))
    outs = pl.pallas_call(
        functools.partial(_ffn_kernel, has_oproj=oproj is not None, has_q=qproj is not None),
        grid=(bsz, seq // tm),
        in_specs=in_specs,
        out_specs=out_specs,
        out_shape=out_shape,
        scratch_shapes=scratch,
        compiler_params=_params("arbitrary", "arbitrary"),
        name="ffn",
    )(*operands)
    return outs if qproj is not None else outs[0]


def _kv_kernel(x_ref, g_ref, wk_ref, wvt_ref, k_ref, vt_ref):
    tm = x_ref.shape[1]
    tk = vt_ref.shape[-1]
    hb = (_rms(x_ref[0]) * g_ref[...]).astype(_BF16)
    k = _dot(hb, wk_ref[...])
    vt = lax.dot_general(wvt_ref[...], hb, (((1,), (1,)), ((), ())), preferred_element_type=_F32)
    for hd in range(N_HEADS):
        rows = slice(hd * V_HEAD_DIM, (hd + 1) * V_HEAD_DIM)
        k_ref[0, hd] = k[:, rows].astype(_BF16)
        for c in range(tm // tk):
            vt_ref[0, hd, c, :V_HEAD_DIM, :] = vt[rows, c * tk:(c + 1) * tk].astype(_BF16)
            vt_ref[0, hd, c, V_HEAD_DIM:, :] = jnp.ones((ONES_ROWS, tk), _BF16)


def _kv(x, g, wk, wvt):
    bsz, seq, d = x.shape
    tm, tk = PROJ_TILE, ATTN_TILE
    assert seq % tm == 0 and tm % tk == 0 and d == N_HEADS * V_HEAD_DIM
    return pl.pallas_call(
        _kv_kernel,
        grid=(bsz, seq // tm),
        in_specs=[
            pl.BlockSpec((1, tm, d), lambda b, i: (b, i, 0)),
            _const_spec((1, d)), _const_spec((d, d)), _const_spec((d, d)),
        ],
        out_specs=[
            pl.BlockSpec((1, N_HEADS, tm, V_HEAD_DIM), lambda b, i: (b, 0, i, 0)),
            pl.BlockSpec((1, N_HEADS, tm // tk, V_ROWS, tk), lambda b, i: (b, 0, i, 0, 0)),
        ],
        out_shape=[
            jax.ShapeDtypeStruct((bsz, N_HEADS, seq, V_HEAD_DIM), _BF16),
            jax.ShapeDtypeStruct((bsz, N_HEADS, seq // tk, V_ROWS, tk), _BF16),
        ],
        compiler_params=_params("parallel", "parallel"),
        name="kv",
    )(x, g, wk, wvt)


def _project_q(x, g_ref, sc_ref, sh_ref, wqt_ref, q_ref):
    tm = x.shape[0]
    tq = q_ref.shape[-1] // 2
    h = _rms(x) * (g_ref[...] * (1.0 + sc_ref[0])) + sh_ref[0]
    qt = lax.dot_general(wqt_ref[...], h.astype(_BF16), (((1,), (1,)), ((), ())),
                         preferred_element_type=_F32) * (HEAD_DIM ** -0.5 * LOG2E)
    first = lax.broadcasted_iota(jnp.int32, (V_HEAD_DIM, tq), 0) < HEAD_DIM
    for hd in range(N_HEADS):
        for c in range(tm // tq):
            qh = qt[hd * V_HEAD_DIM:(hd + 1) * V_HEAD_DIM, c * tq:(c + 1) * tq]
            q_ref[0, hd, c, :, :tq] = jnp.where(first, qh, 0.0).astype(_BF16)
            q_ref[0, hd, c, :, tq:] = jnp.where(first, 0.0, qh).astype(_BF16)


def _bias_kernel(rb_ref, o_ref):
    hd = pl.program_id(0)
    t = o_ref.shape[-1]
    key = lax.broadcasted_iota(jnp.int32, (t, t), 0)
    qry = lax.broadcasted_iota(jnp.int32, (t, t), 1)
    segs = _bucket_segments()
    far = rb_ref[N_BUCKETS - 1, hd]
    for which in range(2):
        rel = qry - key + which * t
        val = jnp.full((t, t), 0.0, _F32)
        for k in range(len(segs) - 2, -1, -1):
            val = jnp.where(rel < segs[k + 1][0], (rb_ref[segs[k][1], hd] - far) * LOG2E, val)
        o_ref[0, which] = jnp.where(rel >= 0, val, MASK_VALUE)


def _bias_tiles(rel_bias):
    t = ATTN_TILE
    return pl.pallas_call(
        _bias_kernel,
        grid=(N_HEADS,),
        in_specs=[pl.BlockSpec(memory_space=pltpu.SMEM)],
        out_specs=pl.BlockSpec((1, 2, t, t), lambda h: (h, 0, 0, 0)),
        out_shape=jax.ShapeDtypeStruct((N_HEADS, 2, t, t), _F32),
        compiler_params=_params("arbitrary"),
        name="bias",
    )(rel_bias)


def _attn_kernel(q_ref, qn_ref, k_ref, vt_ref, bias_ref, lam_ref, sg_ref, o_ref, m_sc, acc_sc, s_a, s_b, x_a, x_b,
                 *, lambda_init):
    t = ATTN_TILE
    dv = V_HEAD_DIM
    heads = q_ref.shape[1]
    i = pl.program_id(2)
    m_sc[...] = jnp.full_like(m_sc, MASK_VALUE)
    acc_sc[...] = jnp.zeros_like(acc_sc)

    def scores_into(buf, j, hh, q_src=q_ref):
        s_buf, x_buf = buf
        rows = pl.ds(pl.multiple_of(j * t, t), t)
        s = _dot(k_ref[0, hh, rows, :], q_src[0, hh, 0])
        s_buf[hh] = s
        x_buf[hh] = jnp.max(s, axis=0, keepdims=True)

    def step(j, which, cur, nxt):
        if nxt is not None:
            scores_into(nxt, j + 1, 0)
        for hh in range(heads):
            if nxt is not None and hh + 1 < heads:
                scores_into(nxt, j + 1, hh + 1)
            s = cur[0][hh]
            if which is None:
                m_cur = cur[1][hh]
            else:
                bias = bias_ref[hh, which]
                s = jnp.concatenate([s[:, :t] + bias, s[:, t:] + bias], axis=1)
                m_cur = jnp.max(s, axis=0, keepdims=True)
            m_prev = m_sc[hh]
            m_new = jnp.maximum(m_prev, m_cur)
            alpha = jnp.exp2(m_prev - m_new)
            p = jnp.exp2(s - m_new)
            acc_sc[hh] = alpha * acc_sc[hh] + _dot(vt_ref[0, hh, j], p.astype(_BF16))
            m_sc[hh] = m_new

    buf_a, buf_b = (s_a, x_a), (s_b, x_b)

    def far_pair(jj, carry):
        ja = 2 * jj
        for hh in range(heads):
            m_prev = m_sc[hh]
            m_new = jnp.maximum(m_prev, jnp.maximum(x_a[hh], x_b[hh]))
            alpha = jnp.exp2(m_prev - m_new)
            pa = jnp.exp2(s_a[hh] - m_new).astype(_BF16)
            scores_into(buf_a, ja + 2, hh)
            pb = jnp.exp2(s_b[hh] - m_new).astype(_BF16)
            scores_into(buf_b, ja + 3, hh)
            vt2 = jnp.concatenate([vt_ref[0, hh, ja], vt_ref[0, hh, ja + 1]], axis=1)
            pv = _dot(vt2, jnp.concatenate([pa, pb], axis=0))
            acc_sc[hh] = alpha * acc_sc[hh] + pv
            m_sc[hh] = m_new
        return carry

    @pl.when(i == 0)
    def _():
        for hh in range(heads):
            scores_into(buf_a, 0, hh)

    lax.fori_loop(0, jnp.maximum((i - 1) // 2, 0), far_pair, 0)

    @pl.when(i % 2 == 0)
    def _():
        @pl.when(i >= 2)
        def _():
            step(i - 2, None, buf_a, None)
            step(i - 1, 1, buf_b, buf_a)

        step(i, 0, buf_a, None)

    @pl.when(i % 2 == 1)
    def _():
        step(i - 1, 1, buf_a, None)
        step(i, 0, buf_b, None)

    for hh in range(heads):
        scores_into(buf_a, 0, hh, qn_ref)
        scores_into(buf_b, 1, hh, qn_ref)

    lv = lam_ref[...]
    lam = (jnp.exp(jnp.sum(lv[0:1] * lv[1:2], axis=-1, keepdims=True))
           - jnp.exp(jnp.sum(lv[2:3] * lv[3:4], axis=-1, keepdims=True)) + lambda_init)
    for hh in range(heads):
        acc = acc_sc[hh]
        o = acc[:dv] / acc[dv:dv + 1]
        o = o[:, :t] - lam * o[:, t:]
        o = o * lax.rsqrt(jnp.mean(o * o, axis=0, keepdims=True) + EPS) * sg_ref[...] * (1.0 - lambda_init)
        o_ref[0, :, hh * dv:(hh + 1) * dv] = o.T.astype(o_ref.dtype)


def _attention(qt, k, vt, bias, lam, sg, lambda_init):
    bsz, n_heads, seq, dv = k.shape
    t = ATTN_TILE
    nt = seq // t
    hp = ATTN_HEADS_PER_STEP
    assert seq % t == 0 and n_heads % hp == 0 and dv == V_HEAD_DIM == 2 * HEAD_DIM and t >= MAX_DISTANCE
    single = dict(pipeline_mode=pl.Buffered(1))
    return pl.pallas_call(
        functools.partial(_attn_kernel, lambda_init=lambda_init),
        grid=(bsz, n_heads // hp, nt),
        in_specs=[
            pl.BlockSpec((1, hp, 1, dv, 2 * t), lambda b, h, i: (b, h, i, 0, 0)),
            pl.BlockSpec((1, hp, 1, dv, 2 * t), lambda b, h, i: (b, h, jnp.minimum(i + 1, nt - 1), 0, 0)),
            pl.BlockSpec((1, hp, seq, dv), lambda b, h, i: (b, h, 0, 0), **single),
            pl.BlockSpec((1, hp, nt, V_ROWS, t), lambda b, h, i: (b, h, 0, 0, 0), **single),
            pl.BlockSpec((hp, 2, t, t), lambda b, h, i: (h, 0, 0, 0), **single),
            pl.BlockSpec((4, HEAD_DIM), lambda b, h, i: (0, 0)),
            pl.BlockSpec((dv, 1), lambda b, h, i: (0, 0)),
        ],
        out_specs=pl.BlockSpec((1, t, hp * dv), lambda b, h, i: (b, i, h)),
        out_shape=jax.ShapeDtypeStruct((bsz, seq, n_heads * dv), _BF16),
        scratch_shapes=[pltpu.VMEM((hp, 1, 2 * t), _F32), pltpu.VMEM((hp, V_ROWS, 2 * t), _F32),
                        pltpu.VMEM((hp, t, 2 * t), _F32), pltpu.VMEM((hp, t, 2 * t), _F32),
                        pltpu.VMEM((hp, 1, 2 * t), _F32), pltpu.VMEM((hp, 1, 2 * t), _F32)],
        compiler_params=_params("parallel", "parallel", "arbitrary"),
        name="attn",
    )(qt, qt, k, vt, bias, lam, sg)


def kernel(x, c, mod_w, mod_b, norm_g, cm_w1, cm_b1, cm_dw, cm_dwb, cm_ln_g, cm_ln_b, cm_w2, cm_b2, kv_norm_g,
           w_k, w_v, w_q, lam, subln_g, w_o, rel_bias, ffn_w_in, ffn_dw, ffn_dwb, ffn_w_out):
    depth = mod_w.shape[0]
    n_conv = cm_w1.shape[0]
    bsz, seq, d = x.shape
    assert 1 <= n_conv < depth, "the first attention layer's Q is produced by the preceding layer's FFN"

    mod = _modulation(c, mod_w, mod_b)

    def row(v):
        return v.reshape(1, -1)

    def mod_vectors(l):
        return [m.reshape(bsz, 1, d) for m in jnp.split(mod[l], 6, axis=-1)]

    kk = vt = bias = qt = None
    for l in range(depth):
        sh_m, sc_m, g_m, sh_f, sc_f, g_f = mod_vectors(l)
        g0, g1, g2, g3 = [row(norm_g[l, k]) for k in range(4)]
        oproj = None
        if l < n_conv:
            u = _conv_a(x, g0, sc_m, sh_m, cm_w1[l].astype(_BF16), row(cm_b1[l]))
            x = _conv_b(x, u, cm_dw[l], row(cm_dwb[l]), row(cm_ln_g[l]), row(cm_ln_b[l]),
                        cm_w2[l].astype(_BF16), row(cm_b2[l]), g1, g_m)
        else:
            if l == n_conv:
                kk, vt = _kv(x, row(kv_norm_g), w_k.astype(_BF16), w_v.T.astype(_BF16))
                bias = _bias_tiles(rel_bias)
            j = l - n_conv
            lambda_init = 0.8 - 0.6 * math.exp(-0.3 * l)
            a = _attention(qt, kk, vt, bias, lam[j], subln_g[j].reshape(-1, 1), lambda_init)
            oproj = (a, w_o[j].astype(_BF16), g1, g_m)
        qproj = None
        if n_conv <= l + 1 < depth:
            nsh_m, nsc_m, _, _, _, _ = mod_vectors(l + 1)
            qproj = (row(norm_g[l + 1, 0]), nsc_m, nsh_m, w_q[l + 1 - n_conv].T.astype(_BF16))
        res = _ffn(x, g2, sc_f, sh_f, ffn_w_in[l].astype(_BF16), ffn_dw[l], row(ffn_dwb[l]),
                   ffn_w_out[l].astype(_BF16), g3, g_f, oproj=oproj, qproj=qproj)
        x, qt = res if qproj is not None else (res, None)
    return x
```

```python
import functools
import math

import jax
import jax.numpy as jnp
from jax import lax
from jax.experimental import pallas as pl
from jax.experimental.pallas import tpu as pltpu

N_HEADS = 8
HEAD_DIM = 64
V_HEAD_DIM = 128
CONV_WIDTH = 31
FFN_CONV_WIDTH = 3
N_BUCKETS = 32
MAX_DISTANCE = 128
MAX_EXACT = 16
EPS = 1e-6

ROW_TILE = 512
PROJ_TILE = 1024
ATTN_TILE = 256
ATTN_HEADS_PER_STEP = 8
ONES_ROWS = 16
V_ROWS = V_HEAD_DIM + ONES_ROWS
LOG2E = math.log2(math.e)
CONV_HALO = 32
CONV_ROWS = 256
LANES = 128
FFN_CHUNK = 256
FFN_HALO = 8
FFN_SLAB_SETS = 11
MASK_VALUE = -1e30
VMEM_LIMIT = 56 * 1024 * 1024

_BF16 = jnp.bfloat16
_F32 = jnp.float32


def _bucket_of_distance(n):
    if n < MAX_EXACT:
        return n
    large = MAX_EXACT + int(math.log(n / MAX_EXACT) / math.log(MAX_DISTANCE / MAX_EXACT) * (N_BUCKETS - MAX_EXACT))
    return min(large, N_BUCKETS - 1)


def _bucket_segments():
    segs = []
    for n in range(MAX_DISTANCE):
        b = _bucket_of_distance(n)
        if not segs or segs[-1][1] != b:
            segs.append((n, b))
    assert all(_bucket_of_distance(n) == N_BUCKETS - 1 for n in range(segs[-1][0], 4 * MAX_DISTANCE))
    return segs


def _rms(x):
    return x * lax.rsqrt(jnp.mean(x * x, axis=-1, keepdims=True) + EPS)


def _sigmoid(x):
    return 0.5 * jnp.tanh(0.5 * x) + 0.5


def _swish(x):
    h = 0.5 * x
    return h + h * jnp.tanh(h)


def _dot(a, b):
    return jnp.dot(a, b, preferred_element_type=_F32)


def _params(*sem):
    return pltpu.CompilerParams(dimension_semantics=sem, vmem_limit_bytes=VMEM_LIMIT)


def _const_spec(shape):
    nd = len(shape)
    return pl.BlockSpec(shape, lambda *_: (0,) * nd)


def _mod_kernel(c_ref, w_ref, b_ref, o_ref):
    o_ref[0] = _dot(_swish(c_ref[...]), w_ref[0]) + b_ref[0]


def _modulation(c, mod_w, mod_b):
    depth, d, n = mod_w.shape
    bsz = c.shape[0]
    tn = 1024
    return pl.pallas_call(
        _mod_kernel,
        grid=(depth, n // tn),
        in_specs=[
            pl.BlockSpec((bsz, d), lambda l, j: (0, 0)),
            pl.BlockSpec((1, d, tn), lambda l, j: (l, 0, j)),
            pl.BlockSpec((1, 1, tn), lambda l, j: (l, 0, j)),
        ],
        out_specs=pl.BlockSpec((1, bsz, tn), lambda l, j: (l, 0, j)),
        out_shape=jax.ShapeDtypeStruct((depth, bsz, n), _F32),
        compiler_params=_params("arbitrary", "arbitrary"),
        name="mod",
    )(c, mod_w, mod_b.reshape(depth, 1, n))


def _conv_a_kernel(x_ref, g_ref, sc_ref, sh_ref, w1_ref, b1_ref, u_ref):
    d = x_ref.shape[-1]
    h_half = _rms(x_ref[0]) * (0.5 * g_ref[...] * (1.0 + sc_ref[0])) + 0.5 * sh_ref[0]
    hb = h_half.astype(_BF16)
    a_half = _dot(hb, w1_ref[:, :d]) + 0.5 * b1_ref[:, :d]
    g_half = _dot(hb, w1_ref[:, d:]) + 0.5 * b1_ref[:, d:]
    u_ref[0] = a_half + a_half * jnp.tanh(g_half)


def _conv_a(x, g, sc, sh, w1, b1):
    bsz, seq, d = x.shape
    tm = PROJ_TILE
    vec = pl.BlockSpec((1, 1, d), lambda b, i: (b, 0, 0))
    return pl.pallas_call(
        _conv_a_kernel,
        grid=(bsz, seq // tm),
        in_specs=[
            pl.BlockSpec((1, tm, d), lambda b, i: (b, i, 0)),
            _const_spec((1, d)), vec, vec,
            _const_spec((d, 2 * d)), _const_spec((1, 2 * d)),
        ],
        out_specs=pl.BlockSpec((1, tm, d), lambda b, i: (b, i, 0)),
        out_shape=jax.ShapeDtypeStruct((bsz, seq, d), _F32),
        compiler_params=_params("parallel", "parallel"),
        name="conv_a",
    )(x, g, sc, sh, w1, b1)


def _conv_b_kernel(x_ref, ucur_ref, uprev_ref, dw_ref, dwb_ref, lng_ref, lnb_ref, w2_ref, b2_ref,
                   g_ref, gate_ref, o_ref, ubuf, cbuf):
    tm, d = ucur_ref.shape[1], ucur_ref.shape[2]
    i = pl.program_id(1)
    for c in range(d // LANES):
        lanes = slice(c * LANES, (c + 1) * LANES)
        prev = uprev_ref[0, :, lanes]
        ubuf[c, 0:CONV_HALO, :] = jnp.where(i > 0, prev, jnp.zeros_like(prev))
        ubuf[c, CONV_HALO:, :] = ucur_ref[0, :, lanes]

    first_shift = CONV_HALO - (CONV_WIDTH - 1)

    def row_block(r, carry):
        r0 = pl.multiple_of(r * CONV_ROWS, CONV_ROWS)
        for c in range(d // LANES):
            lanes = slice(c * LANES, (c + 1) * LANES)
            acc = jnp.broadcast_to(dwb_ref[:, lanes], (CONV_ROWS, LANES))
            for j in range(CONV_WIDTH):
                acc = acc + ubuf[c, pl.ds(r0 + first_shift + j, CONV_ROWS), :] * dw_ref[j:j + 1, lanes]
            cbuf[pl.ds(r0, CONV_ROWS), lanes] = acc
        return carry

    lax.fori_loop(0, tm // CONV_ROWS, row_block, 0)

    cv = cbuf[...]
    mu = jnp.mean(cv, axis=-1, keepdims=True)
    cc = cv - mu
    var = jnp.mean(cc * cc, axis=-1, keepdims=True)
    hz = cc * lax.rsqrt(var + EPS) * (0.5 * lng_ref[...]) + 0.5 * lnb_ref[...]
    y = _dot((hz + hz * jnp.tanh(hz)).astype(_BF16), w2_ref[...]) + b2_ref[...]
    o_ref[0] = x_ref[0] + _rms(y) * (gate_ref[0] * g_ref[...])


def _conv_b(x, u, dw, dwb, lng, lnb, w2, b2, g, gate):
    bsz, seq, d = x.shape
    tm = PROJ_TILE
    assert seq % tm == 0 and tm % CONV_ROWS == 0 and d % LANES == 0 and CONV_WIDTH - 1 <= CONV_HALO
    ratio = tm // CONV_HALO
    vec = pl.BlockSpec((1, 1, d), lambda b, i: (b, 0, 0))
    tile = pl.BlockSpec((1, tm, d), lambda b, i: (b, i, 0))
    return pl.pallas_call(
        _conv_b_kernel,
        grid=(bsz, seq // tm),
        in_specs=[
            tile, tile,
            pl.BlockSpec((1, CONV_HALO, d), lambda b, i: (b, jnp.maximum(i * ratio - 1, 0), 0)),
            _const_spec((CONV_WIDTH, d)), _const_spec((1, d)), _const_spec((1, d)), _const_spec((1, d)),
            _const_spec((d, d)), _const_spec((1, d)), _const_spec((1, d)), vec,
        ],
        out_specs=tile,
        out_shape=jax.ShapeDtypeStruct((bsz, seq, d), _F32),
        scratch_shapes=[pltpu.VMEM((d // LANES, tm + CONV_HALO, LANES), _F32), pltpu.VMEM((tm, d), _F32)],
        compiler_params=_params("parallel", "parallel"),
        name="conv_b",
    )(x, u, u, dw, dwb, lng, lnb, w2, b2, g, gate)


def _ffn_kernel(*refs, has_oproj, has_q):
    refs = list(refs)
    x_ref = refs.pop(0)
    if has_oproj:
        a_ref, wo_ref, g1_ref, gatem_ref = refs[:4]
        refs = refs[4:]
    g_ref, sc_ref, sh_ref, win_ref, dw_ref, dwb_ref, wout_ref, g3_ref, gate_ref = refs[:9]
    refs = refs[9:]
    if has_q:
        gq_ref, scq_ref, shq_ref, wqt_ref = refs[:4]
        refs = refs[4:]
    o_ref = refs.pop(0)
    if has_q:
        q_ref = refs.pop(0)
    hbuf, ubuf, carry, pbuf = refs[:4]
    tm = x_ref.shape[1]
    f = wout_ref.shape[0]
    i = pl.program_id(1)
    slabs = FFN_CHUNK // LANES

    @pl.when(i == 0)
    def _():
        carry[...] = jnp.zeros_like(carry)

    if has_oproj:
        xmid = refs[4]
        xmid[...] = x_ref[0] + _rms(_dot(a_ref[0], wo_ref[...])) * (gatem_ref[0] * g1_ref[...])
        x_in = xmid
    else:
        x_in = x_ref.at[0]

    h = _rms(x_in[...]) * (g_ref[...] * (1.0 + sc_ref[0])) + sh_ref[0]
    hbuf[...] = h.astype(_BF16)

    def up_project(col0, slab0):
        u = _dot(hbuf[...], win_ref[:, col0:col0 + FFN_CHUNK])
        for k in range(slabs):
            cols = slice(col0 + k * LANES, col0 + (k + 1) * LANES)
            uk = u[:, k * LANES:(k + 1) * LANES]
            ubuf[slab0 + k, 0:FFN_HALO, :] = carry[:, cols]
            ubuf[slab0 + k, FFN_HALO:, :] = uk
            carry[:, cols] = uk[tm - FFN_HALO:, :]

    def conv3(col0, slab):
        cols = slice(col0, col0 + LANES)
        out = dwb_ref[:, cols]
        for j in range(FFN_CONV_WIDTH):
            shift = FFN_CONV_WIDTH - 1 - j
            out = out + ubuf[slab, pl.ds(FFN_HALO - shift, tm), :] * dw_ref[j:j + 1, cols]
        return out

    for idx, c0 in enumerate(range(0, f, FFN_CHUNK)):
        base = (idx % FFN_SLAB_SETS) * 2 * slabs
        up_project(c0, base)
        up_project(f + c0, base + slabs)
        for k in range(slabs):
            gk = conv3(c0 + k * LANES, base + k)
            vk = conv3(f + c0 + k * LANES, base + slabs + k)
            pbuf[:, c0 + k * LANES:c0 + (k + 1) * LANES] = (_swish(gk) * vk).astype(_BF16)

    y = _dot(pbuf[...], wout_ref[...])
    x_out = x_in[...] + _rms(y) * (gate_ref[0] * g3_ref[...])
    o_ref[0] = x_out
    if has_q:
        _project_q(x_out, gq_ref, scq_ref, shq_ref, wqt_ref, q_ref)


def _ffn(x, g, sc, sh, w_in, dw, dwb, w_out, g3, gate, oproj=None, qproj=None):
    bsz, seq, d = x.shape
    f = w_out.shape[0]
    tm = ROW_TILE
    vec = pl.BlockSpec((1, 1, d), lambda b, i: (b, 0, 0))
    tile = pl.BlockSpec((1, tm, d), lambda b, i: (b, i, 0))
    single = dict(pipeline_mode=pl.Buffered(1))
    assert seq % tm == 0 and f % FFN_CHUNK == 0 and FFN_CONV_WIDTH - 1 <= FFN_HALO
    operands, in_specs = [x], [tile]
    scratch = [
        pltpu.VMEM((tm, d), _BF16),
        pltpu.VMEM((FFN_SLAB_SETS * 2 * (FFN_CHUNK // LANES), tm + FFN_HALO, LANES), _F32),
        pltpu.VMEM((FFN_HALO, 2 * f), _F32),
        pltpu.VMEM((tm, f), _BF16),
    ]
    if oproj is not None:
        operands += list(oproj)
        in_specs += [tile, pl.BlockSpec((d, d), lambda b, i: (0, 0), **single), _const_spec((1, d)), vec]
        scratch.append(pltpu.VMEM((tm, d), _F32))
    operands += [g, sc, sh, w_in, dw, dwb, w_out, g3, gate]
    in_specs += [
        _const_spec((1, d)), vec, vec,
        pl.BlockSpec((d, 2 * f), lambda b, i: (0, 0), **single),
        _const_spec((FFN_CONV_WIDTH, 2 * f)), _const_spec((1, 2 * f)),
        pl.BlockSpec((f, d), lambda b, i: (0, 0), **single),
        _const_spec((1, d)), vec,
    ]
    out_specs, out_shape = [tile], [jax.ShapeDtypeStruct((bsz, seq, d), _F32)]
    if qproj is not None:
        tq = ATTN_TILE
        operands += list(qproj)
        in_specs += [_const_spec((1, d)), vec, vec, pl.BlockSpec((d, d), lambda b, i: (0, 0), **single)]
        out_specs.append(pl.BlockSpec((1, N_HEADS, tm // tq, V_HEAD_DIM, 2 * tq), lambda b, i: (b, 0, i, 0, 0)))
        out_shape.append(jax.ShapeDtypeStruct((bsz, N_HEADS, seq // tq, V_HEAD_DIM, 2 * tq), _BF16))
    outs = pl.pallas_call(
        functools.partial(_ffn_kernel, has_oproj=oproj is not None, has_q=qproj is not None),
        grid=(bsz, seq // tm),
        in_specs=in_specs,
        out_specs=out_specs,
        out_shape=out_shape,
        scratch_shapes=scratch,
        compiler_params=_params("arbitrary", "arbitrary"),
        name="ffn",
    )(*operands)
    return outs if qproj is not None else outs[0]


def _kv_kernel(x_ref, g_ref, wk_ref, wvt_ref, k_ref, vt_ref):
    tm = x_ref.shape[1]
    tk = vt_ref.shape[-1]
    hb = (_rms(x_ref[0]) * g_ref[...]).astype(_BF16)
    k = _dot(hb, wk_ref[...])
    vt = lax.dot_general(wvt_ref[...], hb, (((1,), (1,)), ((), ())), preferred_element_type=_F32)
    for hd in range(N_HEADS):
        rows = slice(hd * V_HEAD_DIM, (hd + 1) * V_HEAD_DIM)
        k_ref[0, hd] = k[:, rows].astype(_BF16)
        for c in range(tm // tk):
            vt_ref[0, hd, c, :V_HEAD_DIM, :] = vt[rows, c * tk:(c + 1) * tk].astype(_BF16)
            vt_ref[0, hd, c, V_HEAD_DIM:, :] = jnp.ones((ONES_ROWS, tk), _BF16)


def _kv(x, g, wk, wvt):
    bsz, seq, d = x.shape
    tm, tk = PROJ_TILE, ATTN_TILE
    assert seq % tm == 0 and tm % tk == 0 and d == N_HEADS * V_HEAD_DIM
    return pl.pallas_call(
        _kv_kernel,
        grid=(bsz, seq // tm),
        in_specs=[
            pl.BlockSpec((1, tm, d), lambda b, i: (b, i, 0)),
            _const_spec((1, d)), _const_spec((d, d)), _const_spec((d, d)),
        ],
        out_specs=[
            pl.BlockSpec((1, N_HEADS, tm, V_HEAD_DIM), lambda b, i: (b, 0, i, 0)),
            pl.BlockSpec((1, N_HEADS, tm // tk, V_ROWS, tk), lambda b, i: (b, 0, i, 0, 0)),
        ],
        out_shape=[
            jax.ShapeDtypeStruct((bsz, N_HEADS, seq, V_HEAD_DIM), _BF16),
            jax.ShapeDtypeStruct((bsz, N_HEADS, seq // tk, V_ROWS, tk), _BF16),
        ],
        compiler_params=_params("parallel", "parallel"),
        name="kv",
    )(x, g, wk, wvt)


def _project_q(x, g_ref, sc_ref, sh_ref, wqt_ref, q_ref):
    tm = x.shape[0]
    tq = q_ref.shape[-1] // 2
    h = _rms(x) * (g_ref[...] * (1.0 + sc_ref[0])) + sh_ref[0]
    qt = lax.dot_general(wqt_ref[...], h.astype(_BF16), (((1,), (1,)), ((), ())),
                         preferred_element_type=_F32) * (HEAD_DIM ** -0.5 * LOG2E)
    first = lax.broadcasted_iota(jnp.int32, (V_HEAD_DIM, tq), 0) < HEAD_DIM
    for hd in range(N_HEADS):
        for c in range(tm // tq):
            qh = qt[hd * V_HEAD_DIM:(hd + 1) * V_HEAD_DIM, c * tq:(c + 1) * tq]
            q_ref[0, hd, c, :, :tq] = jnp.where(first, qh, 0.0).astype(_BF16)
            q_ref[0, hd, c, :, tq:] = jnp.where(first, 0.0, qh).astype(_BF16)


def _bias_kernel(rb_ref, o_ref):
    hd = pl.program_id(0)
    t = o_ref.shape[-1]
    key = lax.broadcasted_iota(jnp.int32, (t, t), 0)
    qry = lax.broadcasted_iota(jnp.int32, (t, t), 1)
    segs = _bucket_segments()
    far = rb_ref[N_BUCKETS - 1, hd]
    for which in range(2):
        rel = qry - key + which * t
        val = jnp.full((t, t), 0.0, _F32)
        for k in range(len(segs) - 2, -1, -1):
            val = jnp.where(rel < segs[k + 1][0], (rb_ref[segs[k][1], hd] - far) * LOG2E, val)
        o_ref[0, which] = jnp.where(rel >= 0, val, MASK_VALUE)


def _bias_tiles(rel_bias):
    t = ATTN_TILE
    return pl.pallas_call(
        _bias_kernel,
        grid=(N_HEADS,),
        in_specs=[pl.BlockSpec(memory_space=pltpu.SMEM)],
        out_specs=pl.BlockSpec((1, 2, t, t), lambda h: (h, 0, 0, 0)),
        out_shape=jax.ShapeDtypeStruct((N_HEADS, 2, t, t), _F32),
        compiler_params=_params("arbitrary"),
        name="bias",
    )(rel_bias)


def _attn_kernel(q_ref, qn_ref, k_ref, vt_ref, bias_ref, lam_ref, sg_ref, o_ref, m_sc, acc_sc, s_a, s_b, x_a, x_b,
                 *, lambda_init):
    t = ATTN_TILE
    dv = V_HEAD_DIM
    heads = q_ref.shape[1]
    i = pl.program_id(2)
    m_sc[...] = jnp.full_like(m_sc, MASK_VALUE)
    acc_sc[...] = jnp.zeros_like(acc_sc)

    def scores_into(buf, j, hh, q_src=q_ref):
        s_buf, x_buf = buf
        rows = pl.ds(pl.multiple_of(j * t, t), t)
        s = _dot(k_ref[0, hh, rows, :], q_src[0, hh, 0])
        s_buf[hh] = s
        x_buf[hh] = jnp.max(s, axis=0, keepdims=True)

    def step(j, which, cur, nxt):
        if nxt is not None:
            scores_into(nxt, j + 1, 0)
        for hh in range(heads):
            if nxt is not None and hh + 1 < heads:
                scores_into(nxt, j + 1, hh + 1)
            s = cur[0][hh]
            if which is None:
                m_cur = cur[1][hh]
            else:
                bias = bias_ref[hh, which]
                s = jnp.concatenate([s[:, :t] + bias, s[:, t:] + bias], axis=1)
                m_cur = jnp.max(s, axis=0, keepdims=True)
            m_prev = m_sc[hh]
            m_new = jnp.maximum(m_prev, m_cur)
            alpha = jnp.exp2(m_prev - m_new)
            p = jnp.exp2(s - m_new)
            acc_sc[hh] = alpha * acc_sc[hh] + _dot(vt_ref[0, hh, j], p.astype(_BF16))
            m_sc[hh] = m_new

    buf_a, buf_b = (s_a, x_a), (s_b, x_b)

    def far_pair(jj, carry):
        ja = 2 * jj
        for hh in range(heads):
            m_prev = m_sc[hh]
            m_new = jnp.maximum(m_prev, jnp.maximum(x_a[hh], x_b[hh]))
            alpha = jnp.exp2(m_prev - m_new)
            pa = jnp.exp2(s_a[hh] - m_new).astype(_BF16)
            scores_into(buf_a, ja + 2, hh)
            pb = jnp.exp2(s_b[hh] - m_new).astype(_BF16)
            scores_into(buf_b, ja + 3, hh)
            vt2 = jnp.concatenate([vt_ref[0, hh, ja], vt_ref[0, hh, ja + 1]], axis=1)
            pv = _dot(vt2, jnp.concatenate([pa, pb], axis=0))
            acc_sc[hh] = alpha * acc_sc[hh] + pv
            m_sc[hh] = m_new
        return carry

    @pl.when(i == 0)
    def _():
        for hh in range(heads):
            scores_into(buf_a, 0, hh)

    lax.fori_loop(0, jnp.maximum((i - 1) // 2, 0), far_pair, 0)

    @pl.when(i % 2 == 0)
    def _():
        @pl.when(i >= 2)
        def _():
            step(i - 2, None, buf_a, None)
            step(i - 1, 1, buf_b, buf_a)

        step(i, 0, buf_a, None)

    @pl.when(i % 2 == 1)
    def _():
        step(i - 1, 1, buf_a, None)
        step(i, 0, buf_b, None)

    for hh in range(heads):
        scores_into(buf_a, 0, hh, qn_ref)
        scores_into(buf_b, 1, hh, qn_ref)

    lv = lam_ref[...]
    lam = (jnp.exp(jnp.sum(lv[0:1] * lv[1:2], axis=-1, keepdims=True))
           - jnp.exp(jnp.sum(lv[2:3] * lv[3:4], axis=-1, keepdims=True)) + lambda_init)
    for hh in range(heads):
        acc = acc_sc[hh]
        o = acc[:dv] / acc[dv:dv + 1]
        o = o[:, :t] - lam * o[:, t:]
        o = o * lax.rsqrt(jnp.mean(o * o, axis=0, keepdims=True) + EPS) * sg_ref[...] * (1.0 - lambda_init)
        o_ref[0, :, hh * dv:(hh + 1) * dv] = o.T.astype(o_ref.dtype)


def _attention(qt, k, vt, bias, lam, sg, lambda_init):
    bsz, n_heads, seq, dv = k.shape
    t = ATTN_TILE
    nt = seq // t
    hp = ATTN_HEADS_PER_STEP
    assert seq % t == 0 and n_heads % hp == 0 and dv == V_HEAD_DIM == 2 * HEAD_DIM and t >= MAX_DISTANCE
    single = dict(pipeline_mode=pl.Buffered(1))
    return pl.pallas_call(
        functools.partial(_attn_kernel, lambda_init=lambda_init),
        grid=(bsz, n_heads // hp, nt),
        in_specs=[
            pl.BlockSpec((1, hp, 1, dv, 2 * t), lambda b, h, i: (b, h, i, 0, 0)),
            pl.BlockSpec((1, hp, 1, dv, 2 * t), lambda b, h, i: (b, h, jnp.minimum(i + 1, nt - 1), 0, 0)),
            pl.BlockSpec((1, hp, seq, dv), lambda b, h, i: (b, h, 0, 0), **single),
            pl.BlockSpec((1, hp, nt, V_ROWS, t), lambda b, h, i: (b, h, 0, 0, 0), **single),
            pl.BlockSpec((hp, 2, t, t), lambda b, h, i: (h, 0, 0, 0), **single),
            pl.BlockSpec((4, HEAD_DIM), lambda b, h, i: (0, 0)),
            pl.BlockSpec((dv, 1), lambda b, h, i: (0, 0)),
        ],
        out_specs=pl.BlockSpec((1, t, hp * dv), lambda b, h, i: (b, i, h)),
        out_shape=jax.ShapeDtypeStruct((bsz, seq, n_heads * dv), _BF16),
        scratch_shapes=[pltpu.VMEM((hp, 1, 2 * t), _F32), pltpu.VMEM((hp, V_ROWS, 2 * t), _F32),
                        pltpu.VMEM((hp, t, 2 * t), _F32), pltpu.VMEM((hp, t, 2 * t), _F32),
                        pltpu.VMEM((hp, 1, 2 * t), _F32), pltpu.VMEM((hp, 1, 2 * t), _F32)],
        compiler_params=_params("parallel", "parallel", "arbitrary"),
        name="attn",
    )(qt, qt, k, vt, bias, lam, sg)


def kernel(x, c, mod_w, mod_b, norm_g, cm_w1, cm_b1, cm_dw, cm_dwb, cm_ln_g, cm_ln_b, cm_w2, cm_b2, kv_norm_g,
           w_k, w_v, w_q, lam, subln_g, w_o, rel_bias, ffn_w_in, ffn_dw, ffn_dwb, ffn_w_out):
    depth = mod_w.shape[0]
    n_conv = cm_w1.shape[0]
    bsz, seq, d = x.shape
    assert 1 <= n_conv < depth, "the first attention layer's Q is produced by the preceding layer's FFN"

    mod = _modulation(c, mod_w, mod_b)

    def row(v):
        return v.reshape(1, -1)

    def mod_vectors(l):
        return [m.reshape(bsz, 1, d) for m in jnp.split(mod[l], 6, axis=-1)]

    kk = vt = bias = qt = None
    for l in range(depth):
        sh_m, sc_m, g_m, sh_f, sc_f, g_f = mod_vectors(l)
        g0, g1, g2, g3 = [row(norm_g[l, k]) for k in range(4)]
        oproj = None
        if l < n_conv:
            u = _conv_a(x, g0, sc_m, sh_m, cm_w1[l].astype(_BF16), row(cm_b1[l]))
            x = _conv_b(x, u, cm_dw[l], row(cm_dwb[l]), row(cm_ln_g[l]), row(cm_ln_b[l]),
                        cm_w2[l].astype(_BF16), row(cm_b2[l]), g1, g_m)
        else:
            if l == n_conv:
                kk, vt = _kv(x, row(kv_norm_g), w_k.astype(_BF16), w_v.T.astype(_BF16))
                bias = _bias_tiles(rel_bias)
            j = l - n_conv
            lambda_init = 0.8 - 0.6 * math.exp(-0.3 * l)
            a = _attention(qt, kk, vt, bias, lam[j], subln_g[j].reshape(-1, 1), lambda_init)
            oproj = (a, w_o[j].astype(_BF16), g1, g_m)
        qproj = None
        if n_conv <= l + 1 < depth:
            nsh_m, nsc_m, _, _, _, _ = mod_vectors(l + 1)
            qproj = (row(norm_g[l + 1, 0]), nsc_m, nsh_m, w_q[l + 1 - n_conv].T.astype(_BF16))
        res = _ffn(x, g2, sc_f, sh_f, ffn_w_in[l].astype(_BF16), ffn_dw[l], row(ffn_dwb[l]),
                   ffn_w_out[l].astype(_BF16), g3, g_f, oproj=oproj, qproj=qproj)
        x, qt = res if qproj is not None else (res, None)
    return x
```

```python
import functools
import math

import jax
import jax.numpy as jnp
from jax import lax
from jax.experimental import pallas as pl
from jax.experimental.pallas import tpu as pltpu

N_HEADS = 8
HEAD_DIM = 64
V_HEAD_DIM = 128
CONV_WIDTH = 31
FFN_CONV_WIDTH = 3
N_BUCKETS = 32
MAX_DISTANCE = 128
MAX_EXACT = 16
EPS = 1e-6

ROW_TILE = 512
PROJ_TILE = 1024
ATTN_TILE = 256
ATTN_HEADS_PER_STEP = 8
ONES_ROWS = 16
V_ROWS = V_HEAD_DIM + ONES_ROWS
LOG2E = math.log2(math.e)
CONV_HALO = 32
CONV_ROWS = 256
LANES = 128
FFN_CHUNK = 256
FFN_HALO = 8
MASK_VALUE = -1e30
VMEM_LIMIT = 56 * 1024 * 1024

_BF16 = jnp.bfloat16
_F32 = jnp.float32


def _bucket_of_distance(n):
    if n < MAX_EXACT:
        return n
    large = MAX_EXACT + int(math.log(n / MAX_EXACT) / math.log(MAX_DISTANCE / MAX_EXACT) * (N_BUCKETS - MAX_EXACT))
    return min(large, N_BUCKETS - 1)


def _bucket_segments():
    segs = []
    for n in range(MAX_DISTANCE):
        b = _bucket_of_distance(n)
        if not segs or segs[-1][1] != b:
            segs.append((n, b))
    assert all(_bucket_of_distance(n) == N_BUCKETS - 1 for n in range(segs[-1][0], 4 * MAX_DISTANCE))
    return segs


def _rms(x):
    return x * lax.rsqrt(jnp.mean(x * x, axis=-1, keepdims=True) + EPS)


def _sigmoid(x):
    return 0.5 * jnp.tanh(0.5 * x) + 0.5


def _swish(x):
    h = 0.5 * x
    return h + h * jnp.tanh(h)


def _dot(a, b):
    return jnp.dot(a, b, preferred_element_type=_F32)


def _params(*sem):
    return pltpu.CompilerParams(dimension_semantics=sem, vmem_limit_bytes=VMEM_LIMIT)


def _const_spec(shape):
    nd = len(shape)
    return pl.BlockSpec(shape, lambda *_: (0,) * nd)


def _mod_kernel(c_ref, w_ref, b_ref, o_ref):
    o_ref[0] = _dot(_swish(c_ref[...]), w_ref[0]) + b_ref[0]


def _modulation(c, mod_w, mod_b):
    depth, d, n = mod_w.shape
    bsz = c.shape[0]
    tn = 1024
    return pl.pallas_call(
        _mod_kernel,
        grid=(depth, n // tn),
        in_specs=[
            pl.BlockSpec((bsz, d), lambda l, j: (0, 0)),
            pl.BlockSpec((1, d, tn), lambda l, j: (l, 0, j)),
            pl.BlockSpec((1, 1, tn), lambda l, j: (l, 0, j)),
        ],
        out_specs=pl.BlockSpec((1, bsz, tn), lambda l, j: (l, 0, j)),
        out_shape=jax.ShapeDtypeStruct((depth, bsz, n), _F32),
        compiler_params=_params("arbitrary", "arbitrary"),
        name="mod",
    )(c, mod_w, mod_b.reshape(depth, 1, n))


def _conv_a_kernel(x_ref, g_ref, sc_ref, sh_ref, w1_ref, b1_ref, u_ref):
    d = x_ref.shape[-1]
    h_half = _rms(x_ref[0]) * (0.5 * g_ref[...] * (1.0 + sc_ref[0])) + 0.5 * sh_ref[0]
    hb = h_half.astype(_BF16)
    a_half = _dot(hb, w1_ref[:, :d]) + 0.5 * b1_ref[:, :d]
    g_half = _dot(hb, w1_ref[:, d:]) + 0.5 * b1_ref[:, d:]
    u_ref[0] = a_half + a_half * jnp.tanh(g_half)


def _conv_a(x, g, sc, sh, w1, b1):
    bsz, seq, d = x.shape
    tm = PROJ_TILE
    vec = pl.BlockSpec((1, 1, d), lambda b, i: (b, 0, 0))
    return pl.pallas_call(
        _conv_a_kernel,
        grid=(bsz, seq // tm),
        in_specs=[
            pl.BlockSpec((1, tm, d), lambda b, i: (b, i, 0)),
            _const_spec((1, d)), vec, vec,
            _const_spec((d, 2 * d)), _const_spec((1, 2 * d)),
        ],
        out_specs=pl.BlockSpec((1, tm, d), lambda b, i: (b, i, 0)),
        out_shape=jax.ShapeDtypeStruct((bsz, seq, d), _F32),
        compiler_params=_params("parallel", "parallel"),
        name="conv_a",
    )(x, g, sc, sh, w1, b1)


def _conv_b_kernel(x_ref, ucur_ref, uprev_ref, dw_ref, dwb_ref, lng_ref, lnb_ref, w2_ref, b2_ref,
                   g_ref, gate_ref, o_ref, ubuf, cbuf):
    tm, d = ucur_ref.shape[1], ucur_ref.shape[2]
    i = pl.program_id(1)
    for c in range(d // LANES):
        lanes = slice(c * LANES, (c + 1) * LANES)
        prev = uprev_ref[0, :, lanes]
        ubuf[c, 0:CONV_HALO, :] = jnp.where(i > 0, prev, jnp.zeros_like(prev))
        ubuf[c, CONV_HALO:, :] = ucur_ref[0, :, lanes]

    first_shift = CONV_HALO - (CONV_WIDTH - 1)

    def row_block(r, carry):
        r0 = pl.multiple_of(r * CONV_ROWS, CONV_ROWS)
        for c in range(d // LANES):
            lanes = slice(c * LANES, (c + 1) * LANES)
            acc = jnp.broadcast_to(dwb_ref[:, lanes], (CONV_ROWS, LANES))
            for j in range(CONV_WIDTH):
                acc = acc + ubuf[c, pl.ds(r0 + first_shift + j, CONV_ROWS), :] * dw_ref[j:j + 1, lanes]
            cbuf[pl.ds(r0, CONV_ROWS), lanes] = acc
        return carry

    lax.fori_loop(0, tm // CONV_ROWS, row_block, 0)

    cv = cbuf[...]
    mu = jnp.mean(cv, axis=-1, keepdims=True)
    cc = cv - mu
    var = jnp.mean(cc * cc, axis=-1, keepdims=True)
    hz = cc * lax.rsqrt(var + EPS) * (0.5 * lng_ref[...]) + 0.5 * lnb_ref[...]
    y = _dot((hz + hz * jnp.tanh(hz)).astype(_BF16), w2_ref[...]) + b2_ref[...]
    o_ref[0] = x_ref[0] + _rms(y) * (gate_ref[0] * g_ref[...])


def _conv_b(x, u, dw, dwb, lng, lnb, w2, b2, g, gate):
    bsz, seq, d = x.shape
    tm = PROJ_TILE
    assert seq % tm == 0 and tm % CONV_ROWS == 0 and d % LANES == 0 and CONV_WIDTH - 1 <= CONV_HALO
    ratio = tm // CONV_HALO
    vec = pl.BlockSpec((1, 1, d), lambda b, i: (b, 0, 0))
    tile = pl.BlockSpec((1, tm, d), lambda b, i: (b, i, 0))
    return pl.pallas_call(
        _conv_b_kernel,
        grid=(bsz, seq // tm),
        in_specs=[
            tile, tile,
            pl.BlockSpec((1, CONV_HALO, d), lambda b, i: (b, jnp.maximum(i * ratio - 1, 0), 0)),
            _const_spec((CONV_WIDTH, d)), _const_spec((1, d)), _const_spec((1, d)), _const_spec((1, d)),
            _const_spec((d, d)), _const_spec((1, d)), _const_spec((1, d)), vec,
        ],
        out_specs=tile,
        out_shape=jax.ShapeDtypeStruct((bsz, seq, d), _F32),
        scratch_shapes=[pltpu.VMEM((d // LANES, tm + CONV_HALO, LANES), _F32), pltpu.VMEM((tm, d), _F32)],
        compiler_params=_params("parallel", "parallel"),
        name="conv_b",
    )(x, u, u, dw, dwb, lng, lnb, w2, b2, g, gate)


def _ffn_kernel(*refs, has_oproj, has_q):
    refs = list(refs)
    x_ref = refs.pop(0)
    if has_oproj:
        a_ref, wo_ref, g1_ref, gatem_ref = refs[:4]
        refs = refs[4:]
    g_ref, sc_ref, sh_ref, win_ref, dw_ref, dwb_ref, wout_ref, g3_ref, gate_ref = refs[:9]
    refs = refs[9:]
    if has_q:
        gq_ref, scq_ref, shq_ref, wqt_ref = refs[:4]
        refs = refs[4:]
    o_ref = refs.pop(0)
    if has_q:
        q_ref = refs.pop(0)
    hbuf, ubuf, carry, pbuf = refs[:4]
    tm = x_ref.shape[1]
    f = wout_ref.shape[0]
    i = pl.program_id(1)
    slabs = FFN_CHUNK // LANES

    @pl.when(i == 0)
    def _():
        carry[...] = jnp.zeros_like(carry)

    if has_oproj:
        xmid = refs[4]
        xmid[...] = x_ref[0] + _rms(_dot(a_ref[0], wo_ref[...])) * (gatem_ref[0] * g1_ref[...])
        x_in = xmid
    else:
        x_in = x_ref.at[0]

    h = _rms(x_in[...]) * (g_ref[...] * (1.0 + sc_ref[0])) + sh_ref[0]
    hbuf[...] = h.astype(_BF16)

    def up_project(col0, slab0):
        u = _dot(hbuf[...], win_ref[:, col0:col0 + 2 * FFN_CHUNK])
        for k in range(2 * slabs):
            cols = slice(col0 + k * LANES, col0 + (k + 1) * LANES)
            uk = u[:, k * LANES:(k + 1) * LANES]
            ubuf[slab0 + k, 0:FFN_HALO, :] = carry[:, cols]
            ubuf[slab0 + k, FFN_HALO:, :] = uk
            carry[:, cols] = uk[tm - FFN_HALO:, :]

    def conv3(col0, slab):
        cols = slice(col0, col0 + LANES)
        out = dwb_ref[:, cols]
        for j in range(FFN_CONV_WIDTH):
            shift = FFN_CONV_WIDTH - 1 - j
            out = out + ubuf[slab, pl.ds(FFN_HALO - shift, tm), :] * dw_ref[j:j + 1, cols]
        return out

    for idx, c0 in enumerate(range(0, f, FFN_CHUNK)):
        base = (idx % 2) * 2 * slabs
        up_project(2 * c0, base)
        for k in range(slabs):
            gk = conv3(2 * c0 + k * LANES, base + k)
            vk = conv3(2 * c0 + FFN_CHUNK + k * LANES, base + slabs + k)
            pbuf[:, c0 + k * LANES:c0 + (k + 1) * LANES] = (_swish(gk) * vk).astype(_BF16)

    y = _dot(pbuf[...], wout_ref[...])
    x_out = x_in[...] + _rms(y) * (gate_ref[0] * g3_ref[...])
    o_ref[0] = x_out
    if has_q:
        _project_q(x_out, gq_ref, scq_ref, shq_ref, wqt_ref, q_ref)


def _ffn(x, g, sc, sh, w_in, dw, dwb, w_out, g3, gate, oproj=None, qproj=None):
    bsz, seq, d = x.shape
    f = w_out.shape[0]
    tm = ROW_TILE
    vec = pl.BlockSpec((1, 1, d), lambda b, i: (b, 0, 0))
    tile = pl.BlockSpec((1, tm, d), lambda b, i: (b, i, 0))
    single = dict(pipeline_mode=pl.Buffered(1))
    assert seq % tm == 0 and f % FFN_CHUNK == 0 and FFN_CONV_WIDTH - 1 <= FFN_HALO

    def interleave(a):
        lead = a.shape[:-1]
        return a.reshape(*lead, 2, f // FFN_CHUNK, FFN_CHUNK).swapaxes(-3, -2).reshape(*lead, 2 * f)

    w_in, dw, dwb = interleave(w_in), interleave(dw), interleave(dwb)
    operands, in_specs = [x], [tile]
    scratch = [
        pltpu.VMEM((tm, d), _BF16),
        pltpu.VMEM((4 * (FFN_CHUNK // LANES), tm + FFN_HALO, LANES), _F32),
        pltpu.VMEM((FFN_HALO, 2 * f), _F32),
        pltpu.VMEM((tm, f), _BF16),
    ]
    if oproj is not None:
        operands += list(oproj)
        in_specs += [tile, pl.BlockSpec((d, d), lambda b, i: (0, 0), **single), _const_spec((1, d)), vec]
        scratch.append(pltpu.VMEM((tm, d), _F32))
    operands += [g, sc, sh, w_in, dw, dwb, w_out, g3, gate]
    in_specs += [
        _const_spec((1, d)), vec, vec,
        pl.BlockSpec((d, 2 * f), lambda b, i: (0, 0), **single),
        _const_spec((FFN_CONV_WIDTH, 2 * f)), _const_spec((1, 2 * f)),
        pl.BlockSpec((f, d), lambda b, i: (0, 0), **single),
        _const_spec((1, d)), vec,
    ]
    out_specs, out_shape = [tile], [jax.ShapeDtypeStruct((bsz, seq, d), _F32)]
    if qproj is not None:
        tq = ATTN_TILE
        operands += list(qproj)
        in_specs += [_const_spec((1, d)), vec, vec, pl.BlockSpec((d, d), lambda b, i: (0, 0), **single)]
        out_specs.append(pl.BlockSpec((1, N_HEADS, tm // tq, V_HEAD_DIM, 2 * tq), lambda b, i: (b, 0, i, 0, 0)))
        out_shape.append(jax.ShapeDtypeStruct((bsz, N_HEADS, seq // tq, V_HEAD_DIM, 2 * tq), _BF16))
    outs = pl.pallas_call(
        functools.partial(_ffn_kernel, has_oproj=oproj is not None, has_q=qproj is not None),
        grid=(bsz, seq // tm),
        in_specs=in_specs,
        out_specs=out_specs,
        out_shape=out_shape,
        scratch_shapes=scratch,
        compiler_params=_params("arbitrary", "arbitrary"),
        name="ffn",
    )(*operands)
    return outs if qproj is not None else outs[0]


def _kv_kernel(x_ref, g_ref, wk_ref, wvt_ref, k_ref, vt_ref):
    tm = x_ref.shape[1]
    tk = vt_ref.shape[-1]
    hb = (_rms(x_ref[0]) * g_ref[...]).astype(_BF16)
    k = _dot(hb, wk_ref[...])
    vt = lax.dot_general(wvt_ref[...], hb, (((1,), (1,)), ((), ())), preferred_element_type=_F32)
    for hd in range(N_HEADS):
        rows = slice(hd * V_HEAD_DIM, (hd + 1) * V_HEAD_DIM)
        k_ref[0, hd] = k[:, rows].astype(_BF16)
        for c in range(tm // tk):
            vt_ref[0, hd, c, :V_HEAD_DIM, :] = vt[rows, c * tk:(c + 1) * tk].astype(_BF16)
            vt_ref[0, hd, c, V_HEAD_DIM:, :] = jnp.ones((ONES_ROWS, tk), _BF16)


def _kv(x, g, wk, wvt):
    bsz, seq, d = x.shape
    tm, tk = PROJ_TILE, ATTN_TILE
    assert seq % tm == 0 and tm % tk == 0 and d == N_HEADS * V_HEAD_DIM
    return pl.pallas_call(
        _kv_kernel,
        grid=(bsz, seq // tm),
        in_specs=[
            pl.BlockSpec((1, tm, d), lambda b, i: (b, i, 0)),
            _const_spec((1, d)), _const_spec((d, d)), _const_spec((d, d)),
        ],
        out_specs=[
            pl.BlockSpec((1, N_HEADS, tm, V_HEAD_DIM), lambda b, i: (b, 0, i, 0)),
            pl.BlockSpec((1, N_HEADS, tm // tk, V_ROWS, tk), lambda b, i: (b, 0, i, 0, 0)),
        ],
        out_shape=[
            jax.ShapeDtypeStruct((bsz, N_HEADS, seq, V_HEAD_DIM), _BF16),
            jax.ShapeDtypeStruct((bsz, N_HEADS, seq // tk, V_ROWS, tk), _BF16),
        ],
        compiler_params=_params("parallel", "parallel"),
        name="kv",
    )(x, g, wk, wvt)


def _project_q(x, g_ref, sc_ref, sh_ref, wqt_ref, q_ref):
    tm = x.shape[0]
    tq = q_ref.shape[-1] // 2
    h = _rms(x) * (g_ref[...] * (1.0 + sc_ref[0])) + sh_ref[0]
    qt = lax.dot_general(wqt_ref[...], h.astype(_BF16), (((1,), (1,)), ((), ())),
                         preferred_element_type=_F32) * (HEAD_DIM ** -0.5 * LOG2E)
    first = lax.broadcasted_iota(jnp.int32, (V_HEAD_DIM, tq), 0) < HEAD_DIM
    for hd in range(N_HEADS):
        for c in range(tm // tq):
            qh = qt[hd * V_HEAD_DIM:(hd + 1) * V_HEAD_DIM, c * tq:(c + 1) * tq]
            q_ref[0, hd, c, :, :tq] = jnp.where(first, qh, 0.0).astype(_BF16)
            q_ref[0, hd, c, :, tq:] = jnp.where(first, 0.0, qh).astype(_BF16)


def _bias_kernel(rb_ref, o_ref):
    hd = pl.program_id(0)
    t = o_ref.shape[-1]
    key = lax.broadcasted_iota(jnp.int32, (t, t), 0)
    qry = lax.broadcasted_iota(jnp.int32, (t, t), 1)
    segs = _bucket_segments()
    far = rb_ref[N_BUCKETS - 1, hd]
    for which in range(2):
        rel = qry - key + which * t
        val = jnp.full((t, t), 0.0, _F32)
        for k in range(len(segs) - 2, -1, -1):
            val = jnp.where(rel < segs[k + 1][0], (rb_ref[segs[k][1], hd] - far) * LOG2E, val)
        o_ref[0, which] = jnp.where(rel >= 0, val, MASK_VALUE)


def _bias_tiles(rel_bias):
    t = ATTN_TILE
    return pl.pallas_call(
        _bias_kernel,
        grid=(N_HEADS,),
        in_specs=[pl.BlockSpec(memory_space=pltpu.SMEM)],
        out_specs=pl.BlockSpec((1, 2, t, t), lambda h: (h, 0, 0, 0)),
        out_shape=jax.ShapeDtypeStruct((N_HEADS, 2, t, t), _F32),
        compiler_params=_params("arbitrary"),
        name="bias",
    )(rel_bias)


def _attn_kernel(q_ref, qn_ref, k_ref, vt_ref, bias_ref, lam_ref, sg_ref, o_ref, m_sc, acc_sc, s_a, s_b, x_a, x_b,
                 *, lambda_init):
    t = ATTN_TILE
    dv = V_HEAD_DIM
    heads = q_ref.shape[1]
    i = pl.program_id(2)
    m_sc[...] = jnp.full_like(m_sc, MASK_VALUE)
    acc_sc[...] = jnp.zeros_like(acc_sc)

    def scores_into(buf, j, hh, q_src=q_ref):
        s_buf, x_buf = buf
        rows = pl.ds(pl.multiple_of(j * t, t), t)
        s = _dot(k_ref[0, hh, rows, :], q_src[0, hh, 0])
        s_buf[hh] = s
        x_buf[hh] = jnp.max(s, axis=0, keepdims=True)

    def step(j, which, cur, nxt):
        if nxt is not None:
            scores_into(nxt, j + 1, 0)
        for hh in range(heads):
            if nxt is not None and hh + 1 < heads:
                scores_into(nxt, j + 1, hh + 1)
            s = cur[0][hh]
            if which is None:
                m_cur = cur[1][hh]
            else:
                bias = bias_ref[hh, which]
                s = jnp.concatenate([s[:, :t] + bias, s[:, t:] + bias], axis=1)
                m_cur = jnp.max(s, axis=0, keepdims=True)
            m_prev = m_sc[hh]
            m_new = jnp.maximum(m_prev, m_cur)
            alpha = jnp.exp2(m_prev - m_new)
            p = jnp.exp2(s - m_new)
            acc_sc[hh] = alpha * acc_sc[hh] + _dot(vt_ref[0, hh, j], p.astype(_BF16))
            m_sc[hh] = m_new

    buf_a, buf_b = (s_a, x_a), (s_b, x_b)

    def far_pair(jj, carry):
        ja = 2 * jj
        for hh in range(heads):
            m_prev = m_sc[hh]
            m_new = jnp.maximum(m_prev, jnp.maximum(x_a[hh], x_b[hh]))
            alpha = jnp.exp2(m_prev - m_new)
            pa = jnp.exp2(s_a[hh] - m_new).astype(_BF16)
            scores_into(buf_a, ja + 2, hh)
            pb = jnp.exp2(s_b[hh] - m_new).astype(_BF16)
            scores_into(buf_b, ja + 3, hh)
            vt2 = jnp.concatenate([vt_ref[0, hh, ja], vt_ref[0, hh, ja + 1]], axis=1)
            pv = _dot(vt2, jnp.concatenate([pa, pb], axis=0))
            acc_sc[hh] = alpha * acc_sc[hh] + pv
            m_sc[hh] = m_new
        return carry

    @pl.when(i == 0)
    def _():
        for hh in range(heads):
            scores_into(buf_a, 0, hh)

    lax.fori_loop(0, jnp.maximum((i - 1) // 2, 0), far_pair, 0)

    @pl.when(i % 2 == 0)
    def _():
        @pl.when(i >= 2)
        def _():
            step(i - 2, None, buf_a, None)
            step(i - 1, 1, buf_b, buf_a)

        step(i, 0, buf_a, None)

    @pl.when(i % 2 == 1)
    def _():
        step(i - 1, 1, buf_a, None)
        step(i, 0, buf_b, None)

    for hh in range(heads):
        scores_into(buf_a, 0, hh, qn_ref)
        scores_into(buf_b, 1, hh, qn_ref)

    lv = lam_ref[...]
    lam = (jnp.exp(jnp.sum(lv[0:1] * lv[1:2], axis=-1, keepdims=True))
           - jnp.exp(jnp.sum(lv[2:3] * lv[3:4], axis=-1, keepdims=True)) + lambda_init)
    for hh in range(heads):
        acc = acc_sc[hh]
        o = acc[:dv] / acc[dv:dv + 1]
        o = o[:, :t] - lam * o[:, t:]
        o = o * lax.rsqrt(jnp.mean(o * o, axis=0, keepdims=True) + EPS) * sg_ref[...] * (1.0 - lambda_init)
        o_ref[0, :, hh * dv:(hh + 1) * dv] = o.T.astype(o_ref.dtype)


def _attention(qt, k, vt, bias, lam, sg, lambda_init):
    bsz, n_heads, seq, dv = k.shape
    t = ATTN_TILE
    nt = seq // t
    hp = ATTN_HEADS_PER_STEP
    assert seq % t == 0 and n_heads % hp == 0 and dv == V_HEAD_DIM == 2 * HEAD_DIM and t >= MAX_DISTANCE
    single = dict(pipeline_mode=pl.Buffered(1))
    return pl.pallas_call(
        functools.partial(_attn_kernel, lambda_init=lambda_init),
        grid=(bsz, n_heads // hp, nt),
        in_specs=[
            pl.BlockSpec((1, hp, 1, dv, 2 * t), lambda b, h, i: (b, h, i, 0, 0)),
            pl.BlockSpec((1, hp, 1, dv, 2 * t), lambda b, h, i: (b, h, jnp.minimum(i + 1, nt - 1), 0, 0)),
            pl.BlockSpec((1, hp, seq, dv), lambda b, h, i: (b, h, 0, 0), **single),
            pl.BlockSpec((1, hp, nt, V_ROWS, t), lambda b, h, i: (b, h, 0, 0, 0), **single),
            pl.BlockSpec((hp, 2, t, t), lambda b, h, i: (h, 0, 0, 0), **single),
            pl.BlockSpec((4, HEAD_DIM), lambda b, h, i: (0, 0)),
            pl.BlockSpec((dv, 1), lambda b, h, i: (0, 0)),
        ],
        out_specs=pl.BlockSpec((1, t, hp * dv), lambda b, h, i: (b, i, h)),
        out_shape=jax.ShapeDtypeStruct((bsz, seq, n_heads * dv), _BF16),
        scratch_shapes=[pltpu.VMEM((hp, 1, 2 * t), _F32), pltpu.VMEM((hp, V_ROWS, 2 * t), _F32),
                        pltpu.VMEM((hp, t, 2 * t), _F32), pltpu.VMEM((hp, t, 2 * t), _F32),
                        pltpu.VMEM((hp, 1, 2 * t), _F32), pltpu.VMEM((hp, 1, 2 * t), _F32)],
        compiler_params=_params("parallel", "parallel", "arbitrary"),
        name="attn",
    )(qt, qt, k, vt, bias, lam, sg)


def kernel(x, c, mod_w, mod_b, norm_g, cm_w1, cm_b1, cm_dw, cm_dwb, cm_ln_g, cm_ln_b, cm_w2, cm_b2, kv_norm_g,
           w_k, w_v, w_q, lam, subln_g, w_o, rel_bias, ffn_w_in, ffn_dw, ffn_dwb, ffn_w_out):
    depth = mod_w.shape[0]
    n_conv = cm_w1.shape[0]
    bsz, seq, d = x.shape
    assert 1 <= n_conv < depth, "the first attention layer's Q is produced by the preceding layer's FFN"

    mod = _modulation(c, mod_w, mod_b)

    def row(v):
        return v.reshape(1, -1)

    def mod_vectors(l):
        return [m.reshape(bsz, 1, d) for m in jnp.split(mod[l], 6, axis=-1)]

    kk = vt = bias = qt = None
    for l in range(depth):
        sh_m, sc_m, g_m, sh_f, sc_f, g_f = mod_vectors(l)
        g0, g1, g2, g3 = [row(norm_g[l, k]) for k in range(4)]
        oproj = None
        if l < n_conv:
            u = _conv_a(x, g0, sc_m, sh_m, cm_w1[l].astype(_BF16), row(cm_b1[l]))
            x = _conv_b(x, u, cm_dw[l], row(cm_dwb[l]), row(cm_ln_g[l]), row(cm_ln_b[l]),
                        cm_w2[l].astype(_BF16), row(cm_b2[l]), g1, g_m)
        else:
            if l == n_conv:
                kk, vt = _kv(x, row(kv_norm_g), w_k.astype(_BF16), w_v.T.astype(_BF16))
                bias = _bias_tiles(rel_bias)
            j = l - n_conv
            lambda_init = 0.8 - 0.6 * math.exp(-0.3 * l)
            a = _attention(qt, kk, vt, bias, lam[j], subln_g[j].reshape(-1, 1), lambda_init)
            oproj = (a, w_o[j].astype(_BF16), g1, g_m)
        qproj = None
        if n_conv <= l + 1 < depth:
            nsh_m, nsc_m, _, _, _, _ = mod_vectors(l + 1)
            qproj = (row(norm_g[l + 1, 0]), nsc_m, nsh_m, w_q[l + 1 - n_conv].T.astype(_BF16))
        res = _ffn(x, g2, sc_f, sh_f, ffn_w_in[l].astype(_BF16), ffn_dw[l], row(ffn_dwb[l]),
                   ffn_w_out[l].astype(_BF16), g3, g_f, oproj=oproj, qproj=qproj)
        x, qt = res if qproj is not None else (res, None)
    return x
```
